```python
import jax, jax.numpy as jnp
from jax import lax
import numpy as np

D_MODEL = 1024
BATCH = 16
SEQ = 2048
DEPTH = 2

N_HEADS = 16
HEAD_DIM = D_MODEL // N_HEADS
D_FF = 2816
N_EXPERTS = 8
TOP_K = 2
D_FF_EXPERT = 2816
N_META = 16
BLOCK = 128
EPS = 1e-6
FORGET_BIAS_OFFSET = 3.0
N_A = DEPTH // 2
N_B = DEPTH - N_A
N_DENSE = (DEPTH + 1) // 2
N_MOE = DEPTH // 2

kernel_name = "yoco_stickbreak_fox_moe_trunk"


def rms_norm(x, g):
    xf = x.astype(jnp.float32)
    y = xf * lax.rsqrt(jnp.mean(xf * xf, axis=-1, keepdims=True) + EPS)
    return (y * g.astype(jnp.float32)).astype(x.dtype)


def swiglu(h, w_gate_up, w_down):
    gu = h @ w_gate_up
    g, u = jnp.split(gu, 2, axis=-1)
    return (jax.nn.silu(g) * u) @ w_down


def causal_block_sweep(block_fn, q_inputs, kv_inputs):
    B, L = q_inputs[0].shape[0], q_inputs[0].shape[1]
    pos = jnp.arange(L, dtype=jnp.int32)
    out_meta = block_fn([t[:, :N_META] for t in q_inputs], pos[:N_META],
                        [t[:, :N_META] for t in kv_inputs], pos[:N_META])
    n_blocks = (L - N_META) // BLOCK

    def to_blocks(t):
        r = t[:, N_META:]
        r = r.reshape((B, n_blocks, BLOCK) + r.shape[2:])
        return jnp.moveaxis(r, 1, 0)

    q_blocks = [to_blocks(t) for t in q_inputs]
    pos_blocks = pos[N_META:].reshape(n_blocks, BLOCK)
    out_blocks = lax.map(lambda a: block_fn(a[0], a[1], kv_inputs, pos), (q_blocks, pos_blocks))
    out_real = jnp.moveaxis(out_blocks, 0, 1).reshape((B, n_blocks * BLOCK) + out_blocks.shape[3:])
    return jnp.concatenate([out_meta, out_real], axis=1)


def stick_breaking_block(q_list, qpos, kv_list, kpos):
    (q,) = q_list
    k, v = kv_list
    z = jnp.einsum('bqhd,bkhd->bhqk', q.astype(jnp.float32), k.astype(jnp.float32)) * (HEAD_DIM ** -0.5)
    mask = kpos[None, :] < qpos[:, None]
    log1m = jnp.where(mask, -jax.nn.softplus(z), 0.0)
    after = lax.cumsum(log1m, axis=3, reverse=True) - log1m
    a = jnp.where(mask, jnp.exp(jax.nn.log_sigmoid(z) + after), 0.0)
    return jnp.einsum('bhqk,bkhd->bqhd', a, v.astype(jnp.float32)).astype(v.dtype)


def forgetting_block(q_list, qpos, kv_list, kpos):
    q, fq = q_list
    k, v, fk = kv_list
    logits = jnp.einsum('bqhd,bkhd->bhqk', q.astype(jnp.float32), k.astype(jnp.float32)) * (HEAD_DIM ** -0.5)
    decay = jnp.transpose(fq, (0, 2, 1))[:, :, :, None] - jnp.transpose(fk, (0, 2, 1))[:, :, None, :]
    mask = kpos[None, :] <= qpos[:, None]
    p = jax.nn.softmax(jnp.where(mask, logits + decay, -jnp.inf), axis=-1)
    return jnp.einsum('bhqk,bkhd->bqhd', p, v.astype(jnp.float32)).astype(v.dtype)


def stick_breaking_mixer(h, g, w_qkv, w_o):
    B, L, _ = h.shape
    qkv = (rms_norm(h, g) @ w_qkv).reshape(B, L, 3, N_HEADS, HEAD_DIM)
    q, k, v = qkv[:, :, 0], qkv[:, :, 1], qkv[:, :, 2]
    o = causal_block_sweep(stick_breaking_block, [q], [k, v])
    return o.reshape(B, L, D_MODEL) @ w_o


def shared_kv_state(h, g, w_kvf, b_f, k_norm):
    B, L, _ = h.shape
    u = rms_norm(h, g) @ w_kvf
    k = rms_norm(u[..., :D_MODEL].reshape(B, L, N_HEADS, HEAD_DIM), k_norm)
    v = u[..., D_MODEL:2 * D_MODEL].reshape(B, L, N_HEADS, HEAD_DIM)
    f_logit = (u[..., 2 * D_MODEL:] + b_f).astype(jnp.float32)
    F = lax.cumsum(jax.nn.log_sigmoid(f_logit), axis=1)
    return k, v, F


def forgetting_mixer(h, g, w_q, q_norm, w_o, k, v, F):
    B, L, _ = h.shape
    q = rms_norm((rms_norm(h, g) @ w_q).reshape(B, L, N_HEADS, HEAD_DIM), q_norm)
    o = causal_block_sweep(forgetting_block, [q, F], [k, v, F])
    return o.reshape(B, L, D_MODEL) @ w_o


def moe_swiglu(h, w_router, w_gu, w_down):
    B, L, D = h.shape
    t = h.reshape(B * L, D)
    logits = (t @ w_router).astype(jnp.float32)
    top_vals, top_idx = lax.top_k(logits, TOP_K)
    gates = jax.nn.softmax(top_vals, axis=-1)
    combine = jnp.sum(jax.nn.one_hot(top_idx, N_EXPERTS, dtype=jnp.float32) * gates[..., None], axis=1)
    y = jnp.zeros_like(t)
    for e in range(N_EXPERTS):
        y = y + combine[:, e:e + 1].astype(t.dtype) * swiglu(t, w_gu[e], w_down[e])
    return y.reshape(B, L, D)


def setup_inputs(seed: int = 0) -> dict:
    key = jax.random.key(seed)
    ks = jax.random.split(key, 24)
    D, H = D_MODEL, N_HEADS

    def w(k, shape, fan_in):
        return jax.random.normal(k, shape, jnp.float32) * (fan_in ** -0.5)

    def gain(k, shape):
        return 1.0 + 0.1 * jax.random.normal(k, shape, jnp.float32)

    return {
        "x": jax.random.normal(ks[0], (BATCH, SEQ, D), jnp.float32),
        "meta_tokens": jax.random.normal(ks[1], (N_META, D), jnp.float32),
        "norm_attn_a": gain(ks[2], (N_A, D)),
        "w_qkv_a": w(ks[3], (N_A, D, 3 * D), D),
        "w_o_a": w(ks[4], (N_A, D, D), D),
        "norm_kv": gain(ks[5], (D,)),
        "w_kvf": w(ks[6], (D, 2 * D + H), D),
        "b_f": FORGET_BIAS_OFFSET + 0.5 * jax.random.normal(ks[7], (H,), jnp.float32),
        "k_norm": gain(ks[8], (HEAD_DIM,)),
        "norm_attn_b": gain(ks[9], (N_B, D)),
        "w_q_b": w(ks[10], (N_B, D, D), D),
        "q_norm_b": gain(ks[11], (N_B, HEAD_DIM)),
        "w_o_b": w(ks[12], (N_B, D, D), D),
        "norm_ffn_dense": gain(ks[13], (N_DENSE, D)),
        "w_gu_dense": w(ks[14], (N_DENSE, D, 2 * D_FF), D),
        "w_down_dense": w(ks[15], (N_DENSE, D_FF, D), D_FF),
        "norm_ffn_moe": gain(ks[16], (N_MOE, D)),
        "w_router": w(ks[17], (N_MOE, D, N_EXPERTS), D),
        "w_gu_moe": w(ks[18], (N_MOE, N_EXPERTS, D, 2 * D_FF_EXPERT), D),
        "w_down_moe": w(ks[19], (N_MOE, N_EXPERTS, D_FF_EXPERT, D), D_FF_EXPERT),
    }


def reference(x, meta_tokens, norm_attn_a, w_qkv_a, w_o_a, norm_kv, w_kvf, b_f, k_norm,
              norm_attn_b, w_q_b, q_norm_b, w_o_b, norm_ffn_dense, w_gu_dense, w_down_dense,
              norm_ffn_moe, w_router, w_gu_moe, w_down_moe):
    B = x.shape[0]
    meta = jnp.broadcast_to(meta_tokens.astype(x.dtype)[None], (B, N_META, D_MODEL))
    h = jnp.concatenate([meta, x], axis=1)
    k_sh = v_sh = F_sh = None
    for layer in range(DEPTH):
        if layer < N_A:
            h = h + stick_breaking_mixer(h, norm_attn_a[layer], w_qkv_a[layer], w_o_a[layer])
        else:
            if layer == N_A:
                k_sh, v_sh, F_sh = shared_kv_state(h, norm_kv, w_kvf, b_f, k_norm)
            i = layer - N_A
            h = h + forgetting_mixer(h, norm_attn_b[i], w_q_b[i], q_norm_b[i], w_o_b[i], k_sh, v_sh, F_sh)
        j = layer // 2
        if layer % 2 == 0:
            h = h + swiglu(rms_norm(h, norm_ffn_dense[j]), w_gu_dense[j], w_down_dense[j])
        else:
            h = h + moe_swiglu(rms_norm(h, norm_ffn_moe[j]), w_router[j], w_gu_moe[j], w_down_moe[j])
    return h[:, N_META:]
```

```python
import functools

import jax
import jax.numpy as jnp
from jax import lax
from jax.experimental import pallas as pl
from jax.experimental.pallas import tpu as pltpu

F32 = jnp.float32
BF16 = jnp.bfloat16

EPS = 1e-6
TOP_K = 2
LANES = 128
VMEM_LIMIT = 56 * 1024 * 1024
ROW_TILE = 512
COL_CHUNK = 512
FF_CHUNK = 256
ATT_TQ = 256
ATT_TK = 128
MOE_TM = 512
CUM_CHUNK = 256


def _cparams(n_axes):
    return pltpu.CompilerParams(dimension_semantics=("arbitrary",) * n_axes,
                                vmem_limit_bytes=VMEM_LIMIT)


def _split3(x):
    h1 = x.astype(BF16)
    r1 = x - h1.astype(F32)
    h2 = r1.astype(BF16)
    h3 = (r1 - h2.astype(F32)).astype(BF16)
    return h1, h2, h3


def _split2(x):
    h1 = x.astype(BF16)
    h2 = (x - h1.astype(F32)).astype(BF16)
    return h1, h2


def _dot(a, b):
    return jnp.dot(a, b, preferred_element_type=F32)


def _dot_nt(a, b):
    return lax.dot_general(a, b, (((1,), (1,)), ((), ())), preferred_element_type=F32)


def _rms(xf, g):
    ms = jnp.mean(xf * xf, axis=-1, keepdims=True)
    return xf * lax.rsqrt(ms + EPS) * g


def _row_tile(rows, pref=ROW_TILE):
    return pref if rows % pref == 0 else rows


def _col_chunk(*widths):
    c = COL_CHUNK
    while any(w % c for w in widths):
        c //= 2
    assert c >= LANES
    return c


def _norm_matmul_kernel(x_ref, g_ref, w_ref, hg_ref, gm_ref, o_ref, *, n_chunk, head_cols):
    xn = _rms(x_ref[...], g_ref[...]).astype(BF16)
    n = w_ref.shape[1]
    for c0 in range(0, n, n_chunk):
        acc = _dot(xn, w_ref[:, c0:c0 + n_chunk])
        if c0 < head_cols:
            s1, s2 = _split2(acc * acc)
            ms = _dot(s1, gm_ref[...]) + _dot(s2, gm_ref[...])
            acc = acc * lax.rsqrt(ms + EPS) * hg_ref[:, c0:c0 + n_chunk]
        o_ref[:, c0:c0 + n_chunk] = acc.astype(o_ref.dtype)


def norm_matmul(x, g, w, out_dtype, *, head_gain=None, head_dim=None):
    rows, d = x.shape
    n = w.shape[1]
    tm = _row_tile(rows)
    if head_gain is None:
        n_chunk = _col_chunk(n)
        head_cols = 0
        head_gain = jnp.zeros((1, n_chunk), F32)
        head_dim = n_chunk
    else:
        head_cols = head_gain.shape[1]
        n_chunk = _col_chunk(n, head_cols)
        assert n_chunk % head_dim == 0
    grp = jnp.arange(n_chunk) // head_dim
    gm = ((grp[:, None] == grp[None, :]).astype(F32) / head_dim).astype(BF16)
    kern = functools.partial(_norm_matmul_kernel, n_chunk=n_chunk, head_cols=head_cols)
    return pl.pallas_call(
        kern,
        grid=(rows // tm,),
        in_specs=[
            pl.BlockSpec((tm, d), lambda i: (i, 0)),
            pl.BlockSpec((1, d), lambda i: (0, 0)),
            pl.BlockSpec((d, n), lambda i: (0, 0)),
            pl.BlockSpec(head_gain.shape, lambda i: (0, 0)),
            pl.BlockSpec((n_chunk, n_chunk), lambda i: (0, 0)),
        ],
        out_specs=pl.BlockSpec((tm, n), lambda i: (i, 0)),
        out_shape=jax.ShapeDtypeStruct((rows, n), out_dtype),
        compiler_params=_cparams(1),
        name="norm_matmul",
    )(x, g, w, head_gain, gm)


def _matmul_res_kernel(a_ref, w_ref, r_ref, o_ref, *, n_chunk):
    a = a_ref[...]
    n = w_ref.shape[1]
    for c0 in range(0, n, n_chunk):
        o_ref[:, c0:c0 + n_chunk] = r_ref[:, c0:c0 + n_chunk] + _dot(a, w_ref[:, c0:c0 + n_chunk])


def matmul_residual(a, w, res):
    rows, k = a.shape
    n = w.shape[1]
    tm = _row_tile(rows)
    n_chunk = _col_chunk(n)
    return pl.pallas_call(
        functools.partial(_matmul_res_kernel, n_chunk=n_chunk),
        grid=(rows // tm,),
        in_specs=[
            pl.BlockSpec((tm, k), lambda i: (i, 0)),
            pl.BlockSpec((k, n), lambda i: (0, 0)),
            pl.BlockSpec((tm, n), lambda i: (i, 0)),
        ],
        out_specs=pl.BlockSpec((tm, n), lambda i: (i, 0)),
        out_shape=jax.ShapeDtypeStruct((rows, n), F32),
        compiler_params=_cparams(1),
        name="matmul_residual",
    )(a, w, res)


def _swiglu_acc(xn, wgu_ref, wd_ref, ff, f_chunk, widx=()):
    acc = None
    for c0 in range(0, ff, f_chunk):
        gt = _dot(xn, wgu_ref[widx + (slice(None), slice(c0, c0 + f_chunk))])
        up = _dot(xn, wgu_ref[widx + (slice(None), slice(ff + c0, ff + c0 + f_chunk))])
        act = (gt * (1.0 / (1.0 + jnp.exp(-gt))) * up).astype(BF16)
        part = _dot(act, wd_ref[widx + (slice(c0, c0 + f_chunk), slice(None))])
        acc = part if acc is None else acc + part
    return acc


def _dense_ffn_kernel(x_ref, g_ref, wgu_ref, wd_ref, o_ref, *, ff, f_chunk):
    xf = x_ref[...]
    xn = _rms(xf, g_ref[...]).astype(BF16)
    o_ref[...] = xf + _swiglu_acc(xn, wgu_ref, wd_ref, ff, f_chunk)


def dense_ffn(x, g, wgu, wd):
    rows, d = x.shape
    ff = wd.shape[0]
    tm = _row_tile(rows)
    return pl.pallas_call(
        functools.partial(_dense_ffn_kernel, ff=ff, f_chunk=FF_CHUNK),
        grid=(rows // tm,),
        in_specs=[
            pl.BlockSpec((tm, d), lambda i: (i, 0)),
            pl.BlockSpec((1, d), lambda i: (0, 0)),
            pl.BlockSpec((d, 2 * ff), lambda i: (0, 0)),
            pl.BlockSpec((ff, d), lambda i: (0, 0)),
        ],
        out_specs=pl.BlockSpec((tm, d), lambda i: (i, 0)),
        out_shape=jax.ShapeDtypeStruct((rows, d), F32),
        compiler_params=_cparams(1),
        name="dense_ffn",
    )(x, g, wgu, wd)


def _sb_tile(z, mask, run, cs_ref):
    tk = z.shape[1]
    lse = jnp.log(1.0 + jnp.exp(-jnp.abs(z)))
    log1m = -(jnp.maximum(z, 0.0) + lse)
    if mask is not None:
        log1m = jnp.where(mask, log1m, 0.0)
    h1, h2 = _split2(log1m)
    cs = _dot(h1, cs_ref[...]) + _dot(h2, cs_ref[...])
    after = run + cs[:, :tk] - log1m
    a = jnp.exp(jnp.minimum(z, 0.0) - lse + after)
    if mask is not None:
        a = jnp.where(mask, a, 0.0)
    return a, run + cs[:, tk:]


def _sb_kernel(*refs, tq, tk, hd, n_meta, scale):
    if n_meta:
        q_ref, k_ref, v_ref, km_ref, vm_ref, cs_ref, csm_ref, o_ref = refs
    else:
        q_ref, k_ref, v_ref, cs_ref, o_ref = refs
    s_len = q_ref.shape[1]
    n_q = s_len // tq
    n_diag = tq // tk
    lane = lax.broadcasted_iota(jnp.int32, (tq, LANES), 1)
    head0 = lane < hd
    row_in = lax.broadcasted_iota(jnp.int32, (tq, tk), 0)
    col_in = lax.broadcasted_iota(jnp.int32, (tq, tk), 1)

    def q_tile(i, carry):
        r0 = pl.multiple_of(i * tq, tq)
        q = q_ref[0, pl.ds(r0, tq), :] * scale
        qh = (jnp.where(head0, q, jnp.zeros_like(q)), jnp.where(head0, jnp.zeros_like(q), q))

        def step(c0, mask, state):
            k = k_ref[0, pl.ds(c0, tk), :]
            v = v_ref[0, pl.ds(c0, tk), :]
            out = []
            for h in range(2):
                run, acc = state[h]
                a, run = _sb_tile(_dot_nt(qh[h], k), mask, run, cs_ref)
                out.append((run, acc + _dot(a.astype(BF16), v)))
            return tuple(out)

        zero = (jnp.zeros((tq, tk), F32), jnp.zeros((tq, LANES), F32))
        state = (zero, zero)
        for dgi in range(n_diag - 1, -1, -1):
            mask = (col_in + dgi * tk) < row_in
            state = step(pl.multiple_of(r0 + dgi * tk, tk), mask, state)

        def body(j, st):
            c0 = pl.multiple_of(r0 - (j + 1) * tk, tk)
            return step(c0, None, st)

        state = lax.fori_loop(0, i * n_diag, body, state)

        if n_meta:
            out = []
            for h in range(2):
                run, acc = state[h]
                a, _ = _sb_tile(_dot_nt(qh[h], km_ref[...]), None, run[:, :n_meta], csm_ref)
                out.append((run, acc + _dot(a.astype(BF16), vm_ref[...])))
            state = tuple(out)

        o_ref[0, pl.ds(r0, tq), :] = jnp.where(head0, state[0][1], state[1][1]).astype(o_ref.dtype)
        return carry

    lax.fori_loop(0, n_q, q_tile, 0)


def _suffix_matrix(t):
    r = jnp.arange(t)
    upper = (r[:, None] >= r[None, :]).astype(BF16)
    return jnp.concatenate([upper, jnp.ones((t, t), BF16)], axis=1)


def stickbreak_attention(qkv, qkv_meta, d_model, hd):
    b, s_len, _ = qkv.shape
    n_pair = d_model // LANES
    tq = min(ATT_TQ, s_len)
    tk = min(ATT_TK, s_len)
    n_meta = 0 if qkv_meta is None else qkv_meta.shape[0]
    col = lambda which: (lambda bi, hp: (bi, 0, which * n_pair + hp))
    in_specs = [pl.BlockSpec((1, s_len, LANES), col(0)),
                pl.BlockSpec((1, s_len, LANES), col(1)),
                pl.BlockSpec((1, s_len, LANES), col(2))]
    args = [qkv, qkv, qkv]
    if n_meta:
        in_specs += [pl.BlockSpec((n_meta, LANES), lambda bi, hp: (0, n_pair + hp)),
                     pl.BlockSpec((n_meta, LANES), lambda bi, hp: (0, 2 * n_pair + hp))]
        args += [qkv_meta, qkv_meta]
    in_specs.append(pl.BlockSpec((tk, 2 * tk), lambda bi, hp: (0, 0)))
    args.append(_suffix_matrix(tk))
    if n_meta:
        in_specs.append(pl.BlockSpec((n_meta, 2 * n_meta), lambda bi, hp: (0, 0)))
        args.append(_suffix_matrix(n_meta))
    kern = functools.partial(_sb_kernel, tq=tq, tk=tk, hd=hd, n_meta=n_meta, scale=hd ** -0.5)
    return pl.pallas_call(
        kern,
        grid=(b, n_pair),
        in_specs=in_specs,
        out_specs=pl.BlockSpec((1, s_len, LANES), lambda bi, hp: (bi, 0, hp)),
        out_shape=jax.ShapeDtypeStruct((b, s_len, d_model), BF16),
        compiler_params=_cparams(2),
        name="stickbreak_attention",
    )(*args)


def _log_sigmoid(y):
    return jnp.minimum(y, 0.0) - jnp.log(1.0 + jnp.exp(-jnp.abs(y)))


def _tri_cumsum(tri, x):
    h1, h2, h3 = _split3(x)
    return _dot(tri, h1) + _dot(tri, h2) + _dot(tri, h3)


def _forget_cumsum_kernel(f_ref, fm_ref, b_ref, tri_ref, trim_ref, o_ref, om_ref, *, chunk):
    bias = b_ref[...]
    fm = _tri_cumsum(trim_ref[...], _log_sigmoid(fm_ref[...] + bias))
    for p, part in enumerate(_split3(fm)):
        om_ref[p] = part
    n_meta = fm.shape[0]
    carry = fm[n_meta - 1:n_meta, :]
    s_len = f_ref.shape[1]
    for c0 in range(0, s_len, chunk):
        fc = _tri_cumsum(tri_ref[...], _log_sigmoid(f_ref[0, c0:c0 + chunk, :] + bias)) + carry
        for p, part in enumerate(_split3(fc)):
            o_ref[p, 0, c0:c0 + chunk, :] = part
        carry = fc[chunk - 1:chunk, :]


def forget_cumsum(f_logit, f_logit_meta, bias):
    b, s_len, w = f_logit.shape
    n_meta = f_logit_meta.shape[0]
    chunk = min(CUM_CHUNK, s_len)
    tri = lambda t: (jnp.arange(t)[:, None] >= jnp.arange(t)[None, :]).astype(BF16)
    return pl.pallas_call(
        functools.partial(_forget_cumsum_kernel, chunk=chunk),
        grid=(b,),
        in_specs=[
            pl.BlockSpec((1, s_len, w), lambda bi: (bi, 0, 0)),
            pl.BlockSpec((n_meta, w), lambda bi: (0, 0)),
            pl.BlockSpec((1, w), lambda bi: (0, 0)),
            pl.BlockSpec((chunk, chunk), lambda bi: (0, 0)),
            pl.BlockSpec((n_meta, n_meta), lambda bi: (0, 0)),
        ],
        out_specs=[pl.BlockSpec((3, 1, s_len, w), lambda bi: (0, bi, 0, 0)),
                   pl.BlockSpec((3, n_meta, w), lambda bi: (0, 0, 0))],
        out_shape=[jax.ShapeDtypeStruct((3, b, s_len, w), BF16),
                   jax.ShapeDtypeStruct((3, n_meta, w), BF16)],
        compiler_params=_cparams(1),
        name="forget_cumsum",
    )(f_logit, f_logit_meta, bias, tri(chunk), tri(n_meta))


def _fox_kernel(q_ref, k_ref, v_ref, km_ref, vm_ref, o_ref, *, tq, tk, hd):
    s_len = q_ref.shape[1]
    n_q = s_len // tq
    n_diag = tq // tk
    lane = lax.broadcasted_iota(jnp.int32, (tq, LANES), 1)
    head0 = lane < hd
    row_in = lax.broadcasted_iota(jnp.int32, (tq, tk), 0)
    col_in = lax.broadcasted_iota(jnp.int32, (tq, tk), 1)

    def q_tile(i, carry):
        r0 = pl.multiple_of(i * tq, tq)
        qh = [q_ref[0, pl.ds(r0, tq), h * LANES:(h + 1) * LANES] for h in range(2)]

        def update(h, z, v, st):
            m, acc = st
            m_new = jnp.maximum(m, jnp.max(z, axis=-1, keepdims=True))
            p = jnp.exp(z - m_new)
            return m_new, acc * jnp.exp(m - m_new) + _dot(p.astype(BF16), v)

        state = []
        for h in range(2):
            z = _dot_nt(qh[h], km_ref[:, h * LANES:(h + 1) * LANES])
            m0 = jnp.max(z, axis=-1, keepdims=True)
            p = jnp.exp(z - m0)
            state.append((m0, _dot(p.astype(BF16), vm_ref[:, h * LANES:(h + 1) * LANES])))
        state = tuple(state)

        def step(c0, mask, st):
            out = []
            for h in range(2):
                z = _dot_nt(qh[h], k_ref[0, pl.ds(c0, tk), h * LANES:(h + 1) * LANES])
                if mask is not None:
                    z = jnp.where(mask, z, -jnp.inf)
                out.append(update(h, z, v_ref[0, pl.ds(c0, tk), h * LANES:(h + 1) * LANES], st[h]))
            return tuple(out)

        def body(j, st):
            return step(pl.multiple_of(j * tk, tk), None, st)

        state = lax.fori_loop(0, i * n_diag, body, state)
        for dgi in range(n_diag):
            mask = (col_in + dgi * tk) <= row_in
            state = step(pl.multiple_of(r0 + dgi * tk, tk), mask, state)

        res0, res1 = state[0][1], state[1][1]
        rot0 = pltpu.roll(res0, hd, 1)
        rot1 = pltpu.roll(res1, hd, 1)
        o_ref[0, pl.ds(r0, tq), :] = jnp.where(head0, res0 / rot0, rot1 / res1).astype(o_ref.dtype)
        return carry

    lax.fori_loop(0, n_q, q_tile, 0)


def forgetting_attention(q_ext, k_ext, v_ext, km_ext, vm_ext, d_model, hd):
    b, s_len, _ = q_ext.shape
    n_pair = d_model // LANES
    n_meta = km_ext.shape[0]
    tq = min(ATT_TQ, s_len)
    tk = min(ATT_TK, s_len)
    blk = pl.BlockSpec((1, s_len, 2 * LANES), lambda bi, hp: (bi, 0, hp))
    mblk = pl.BlockSpec((n_meta, 2 * LANES), lambda bi, hp: (0, hp))
    return pl.pallas_call(
        functools.partial(_fox_kernel, tq=tq, tk=tk, hd=hd),
        grid=(b, n_pair),
        in_specs=[blk, blk, blk, mblk, mblk],
        out_specs=pl.BlockSpec((1, s_len, LANES), lambda bi, hp: (bi, 0, hp)),
        out_shape=jax.ShapeDtypeStruct((b, s_len, d_model), BF16),
        compiler_params=_cparams(2),
        name="forgetting_attention",
    )(q_ext, k_ext, v_ext, km_ext, vm_ext)


def _fox_operands(q, k, v, f_q, f_k, hd):
    n_heads = q.shape[-1] // hd
    lead_q, lead_k = q.shape[:-1], k.shape[:-1]
    parts = lambda f: jnp.moveaxis(f, 0, -1)

    pad = LANES - hd - 2 * 3
    qh = q.reshape(lead_q + (n_heads, hd)) * jnp.asarray(hd ** -0.5, BF16)
    q_ext = jnp.concatenate([qh, parts(f_q), jnp.ones(lead_q + (n_heads, 3), BF16),
                             jnp.zeros(lead_q + (n_heads, pad), BF16)], axis=-1)
    kh = k.reshape(lead_k + (n_heads, hd))
    k_ext = jnp.concatenate([kh, jnp.ones(lead_k + (n_heads, 3), BF16), parts(-f_k),
                             jnp.zeros(lead_k + (n_heads, pad), BF16)], axis=-1)
    vh = v.reshape(lead_k + (n_heads, hd))
    v_ext = jnp.concatenate([vh, jnp.ones(lead_k + (n_heads, LANES - hd), BF16)], axis=-1)
    flat = lambda t: t.reshape(t.shape[:-2] + (n_heads * LANES,))
    return flat(q_ext), flat(k_ext), flat(v_ext)


def _router_kernel(x_ref, g_ref, wr_ref, xn_ref, cw_ref, *, n_experts):
    xn = _rms(x_ref[...], g_ref[...])
    xn_ref[...] = xn.astype(BF16)
    a1, a2, a3 = _split3(xn)
    b1, b2, b3 = _split3(wr_ref[...])
    logits = (_dot(a1, b1) + _dot(a1, b2) + _dot(a2, b1)
              + _dot(a2, b2) + _dot(a1, b3) + _dot(a3, b1))
    lane = lax.broadcasted_iota(jnp.int32, logits.shape, 1)
    lg = jnp.where(lane < n_experts, logits, -jnp.inf)
    m1 = jnp.max(lg, axis=-1, keepdims=True)
    i1 = jnp.min(jnp.where(lg == m1, lane, LANES), axis=-1, keepdims=True)
    lg2 = jnp.where(lane == i1, -jnp.inf, lg)
    m2 = jnp.max(lg2, axis=-1, keepdims=True)
    i2 = jnp.min(jnp.where(lg2 == m2, lane, LANES), axis=-1, keepdims=True)
    e2 = jnp.exp(m2 - m1)
    den = 1.0 + e2
    cw = (jnp.where(lane == i1, 1.0 / den, 0.0) + jnp.where(lane == i2, e2 / den, 0.0)
          + jnp.where(lane == n_experts, i1.astype(F32), 0.0)
          + jnp.where(lane == n_experts + 1, i2.astype(F32), 0.0))
    cw_ref[...] = cw


def moe_router(x, g, w_router):
    rows, d = x.shape
    n_experts = w_router.shape[1]
    wr = jnp.zeros((d, LANES), F32).at[:, :n_experts].set(w_router)
    tm = _row_tile(rows)
    return pl.pallas_call(
        functools.partial(_router_kernel, n_experts=n_experts),
        grid=(rows // tm,),
        in_specs=[
            pl.BlockSpec((tm, d), lambda i: (i, 0)),
            pl.BlockSpec((1, d), lambda i: (0, 0)),
            pl.BlockSpec((d, LANES), lambda i: (0, 0)),
        ],
        out_specs=[pl.BlockSpec((tm, d), lambda i: (i, 0)),
                   pl.BlockSpec((tm, LANES), lambda i: (i, 0))],
        out_shape=[jax.ShapeDtypeStruct((rows, d), BF16),
                   jax.ShapeDtypeStruct((rows, LANES), F32)],
        compiler_params=_cparams(1),
        name="moe_router",
    )(x, g, wr)


def _moe_ffn_kernel(te_ref, nt_ref, x_ref, gate_ref, wgu_ref, wd_ref, o_ref, *, ff, f_chunk):
    i = pl.program_id(0)

    @pl.when(i < nt_ref[0])
    def _():
        acc = _swiglu_acc(x_ref[...], wgu_ref, wd_ref, ff, f_chunk, widx=(0,))
        gate = gate_ref[...]
        d = acc.shape[1]
        for c0 in range(0, d, LANES):
            o_ref[:, c0:c0 + LANES] = (acc[:, c0:c0 + LANES] * gate).astype(o_ref.dtype)

    @pl.when(i >= nt_ref[0])
    def _():
        o_ref[...] = jnp.zeros_like(o_ref)


def moe_grouped_ffn(xs, gate, tile_expert, n_tiles_used, wgu, wd):
    p_rows, d = xs.shape
    n_experts, ff, _ = wd.shape
    tm = MOE_TM
    grid_spec = pltpu.PrefetchScalarGridSpec(
        num_scalar_prefetch=2,
        grid=(p_rows // tm,),
        in_specs=[
            pl.BlockSpec((tm, d), lambda i, te, nt: (i, 0)),
            pl.BlockSpec((tm, LANES), lambda i, te, nt: (i, 0)),
            pl.BlockSpec((1, d, 2 * ff), lambda i, te, nt: (te[i], 0, 0)),
            pl.BlockSpec((1, ff, d), lambda i, te, nt: (te[i], 0, 0)),
        ],
        out_specs=pl.BlockSpec((tm, d), lambda i, te, nt: (i, 0)),
    )
    return pl.pallas_call(
        functools.partial(_moe_ffn_kernel, ff=ff, f_chunk=FF_CHUNK),
        grid_spec=grid_spec,
        out_shape=jax.ShapeDtypeStruct((p_rows, d), BF16),
        compiler_params=_cparams(1),
        name="moe_grouped_ffn",
    )(tile_expert, n_tiles_used, xs, gate, wgu, wd)


def _combine_kernel(h_ref, y1_ref, y2_ref, o_ref):
    o_ref[...] = h_ref[...] + y1_ref[...].astype(F32) + y2_ref[...].astype(F32)


def moe_block(h, g, w_router, wgu, wd):
    rows, d = h.shape
    n_experts = w_router.shape[1]
    tm = MOE_TM
    xn, cw = moe_router(h, g, w_router)

    eid = cw[:, n_experts:n_experts + TOP_K].astype(jnp.int32)
    gates = jnp.take_along_axis(cw[:, :n_experts], eid, axis=1)
    sel = (eid[:, :, None] == jnp.arange(n_experts)[None, None, :]).any(axis=1)
    rank = jnp.cumsum(sel.astype(jnp.int32), axis=0) - 1
    cnt = rank[-1] + 1
    cnt_pad = ((cnt + tm - 1) // tm) * tm
    ends = jnp.cumsum(cnt_pad)
    offs = ends - cnt_pad
    pos = offs[eid] + jnp.take_along_axis(rank, eid, axis=1)
    p_rows = rows * TOP_K + n_experts * tm
    tok = jnp.broadcast_to(jnp.arange(rows, dtype=jnp.int32)[:, None], (rows, TOP_K))
    tok_sorted = jnp.zeros((p_rows,), jnp.int32).at[pos.reshape(-1)].set(tok.reshape(-1))
    gate_sorted = jnp.zeros((p_rows,), F32).at[pos.reshape(-1)].set(gates.reshape(-1))
    n_tiles = p_rows // tm
    tile_start = jnp.arange(n_tiles, dtype=jnp.int32) * tm
    tile_expert = jnp.minimum((tile_start[:, None] >= ends[None, :]).sum(axis=1), n_experts - 1)
    n_used = (ends[-1] // tm).astype(jnp.int32).reshape(1)

    xs = jnp.take(xn, tok_sorted, axis=0)
    gate_rep = jnp.broadcast_to(gate_sorted[:, None], (p_rows, LANES))
    ys = moe_grouped_ffn(xs, gate_rep, tile_expert.astype(jnp.int32), n_used, wgu, wd)
    y1 = jnp.take(ys, pos[:, 0], axis=0)
    y2 = jnp.take(ys, pos[:, 1], axis=0)
    tr = _row_tile(rows)
    spec = pl.BlockSpec((tr, d), lambda i: (i, 0))
    return pl.pallas_call(
        _combine_kernel,
        grid=(rows // tr,),
        in_specs=[spec, spec, spec],
        out_specs=spec,
        out_shape=jax.ShapeDtypeStruct((rows, d), F32),
        compiler_params=_cparams(1),
        name="moe_combine",
    )(h, y1, y2)


def kernel(x, meta_tokens, norm_attn_a, w_qkv_a, w_o_a, norm_kv, w_kvf, b_f, k_norm, norm_attn_b,
           w_q_b, q_norm_b, w_o_b, norm_ffn_dense, w_gu_dense, w_down_dense, norm_ffn_moe,
           w_router, w_gu_moe, w_down_moe):
    b, s_len, d = x.shape
    n_heads = b_f.shape[0]
    hd = d // n_heads
    n_meta = meta_tokens.shape[0]
    depth = norm_attn_a.shape[0] + norm_attn_b.shape[0]
    n_a = norm_attn_a.shape[0]
    assert 2 * hd == LANES and d % LANES == 0
    row = lambda v: v.reshape(1, -1).astype(F32)

    h = x.reshape(b * s_len, d)
    hm = meta_tokens.astype(x.dtype)
    k_sh = None
    for layer in range(depth):
        last = layer == depth - 1
        if layer < n_a:
            w_qkv = w_qkv_a[layer].astype(BF16)
            g = row(norm_attn_a[layer])
            qkv = norm_matmul(h, g, w_qkv, BF16).reshape(b, s_len, 3 * d)
            qkv_m = norm_matmul(hm, g, w_qkv, BF16)
            o = stickbreak_attention(qkv, qkv_m, d, hd).reshape(b * s_len, d)
            o_m = stickbreak_attention(qkv_m[None], None, d, hd)[0]
            w_o = w_o_a[layer].astype(BF16)
            h = matmul_residual(o, w_o, h)
            hm = matmul_residual(o_m, w_o, hm)
        else:
            if layer == n_a:
                g = row(norm_kv)
                w_kv = w_kvf[:, :2 * d].astype(BF16)
                w_f = jnp.zeros((d, LANES), F32).at[:, :n_heads].set(w_kvf[:, 2 * d:]).astype(BF16)
                kgain = row(jnp.tile(k_norm, n_heads))
                kv = norm_matmul(h, g, w_kv, BF16, head_gain=kgain, head_dim=hd)
                kv_m = norm_matmul(hm, g, w_kv, BF16, head_gain=kgain, head_dim=hd)
                fl = norm_matmul(h, g, w_f, F32).reshape(b, s_len, LANES)
                fl_m = norm_matmul(hm, g, w_f, F32)
                bias = jnp.zeros((1, LANES), F32).at[0, :n_heads].set(b_f.astype(F32))
                f_cum, f_cum_m = forget_cumsum(fl, fl_m, bias)
                f_cum = f_cum[..., :n_heads].reshape(3, b * s_len, n_heads)
                f_cum_m = f_cum_m[..., :n_heads]
                k_sh, v_sh = kv[:, :d], kv[:, d:]
                k_sh_m, v_sh_m = kv_m[:, :d], kv_m[:, d:]
            i = layer - n_a
            qgain = row(jnp.tile(q_norm_b[i], n_heads))
            q = norm_matmul(h, row(norm_attn_b[i]), w_q_b[i].astype(BF16), BF16,
                            head_gain=qgain, head_dim=hd)
            q_ext, k_ext, v_ext = _fox_operands(q, k_sh, v_sh, f_cum, f_cum, hd)
            _, km_ext, vm_ext = _fox_operands(k_sh_m, k_sh_m, v_sh_m, f_cum_m, f_cum_m, hd)
            three = lambda t: t.reshape(b, s_len, -1)
            o = forgetting_attention(three(q_ext), three(k_ext), three(v_ext), km_ext, vm_ext, d, hd)
            h = matmul_residual(o.reshape(b * s_len, d), w_o_b[i].astype(BF16), h)
            if not last:
                raise NotImplementedError("meta-row queries in forgetting layers before the last")
        j = layer // 2
        if layer % 2 == 0:
            wgu = w_gu_dense[j].astype(BF16)
            wd = w_down_dense[j].astype(BF16)
            g = row(norm_ffn_dense[j])
            h = dense_ffn(h, g, wgu, wd)
            if not last:
                hm = dense_ffn(hm, g, wgu, wd)
        else:
            h = moe_block(h, row(norm_ffn_moe[j]), w_router[j],
                          w_gu_moe[j].astype(BF16), w_down_moe[j].astype(BF16))
            if not last:
                raise NotImplementedError("meta rows through a MoE layer before the last")
    return h.reshape(b, s_len, d)
```

```python
import functools

import jax
import jax.numpy as jnp
from jax import lax
from jax.experimental import pallas as pl
from jax.experimental.pallas import tpu as pltpu

F32 = jnp.float32
BF16 = jnp.bfloat16

EPS = 1e-6
TOP_K = 2
LANES = 128
VMEM_LIMIT = 56 * 1024 * 1024
ROW_TILE = 512
COL_CHUNK = 512
FF_CHUNK = 256
ATT_TQ = 256
ATT_TK = 128
MOE_TM = 512
CUM_CHUNK = 256


def _cparams(n_axes):
    return pltpu.CompilerParams(dimension_semantics=("arbitrary",) * n_axes,
                                vmem_limit_bytes=VMEM_LIMIT)


def _split3(x):
    h1 = x.astype(BF16)
    r1 = x - h1.astype(F32)
    h2 = r1.astype(BF16)
    h3 = (r1 - h2.astype(F32)).astype(BF16)
    return h1, h2, h3


def _split2(x):
    h1 = x.astype(BF16)
    h2 = (x - h1.astype(F32)).astype(BF16)
    return h1, h2


def _dot(a, b):
    return jnp.dot(a, b, preferred_element_type=F32)


def _dot_nt(a, b):
    return lax.dot_general(a, b, (((1,), (1,)), ((), ())), preferred_element_type=F32)


def _rms(xf, g):
    ms = jnp.mean(xf * xf, axis=-1, keepdims=True)
    return xf * lax.rsqrt(ms + EPS) * g


def _row_tile(rows, pref=ROW_TILE):
    return pref if rows % pref == 0 else rows


def _col_chunk(*widths):
    c = COL_CHUNK
    while any(w % c for w in widths):
        c //= 2
    assert c >= LANES
    return c


def _norm_matmul_kernel(x_ref, g_ref, w_ref, hg_ref, gm_ref, o_ref, *, n_chunk, head_cols):
    xn = _rms(x_ref[...], g_ref[...]).astype(BF16)
    n = w_ref.shape[1]
    for c0 in range(0, n, n_chunk):
        acc = _dot(xn, w_ref[:, c0:c0 + n_chunk])
        if c0 < head_cols:
            s1, s2 = _split2(acc * acc)
            ms = _dot(s1, gm_ref[...]) + _dot(s2, gm_ref[...])
            acc = acc * lax.rsqrt(ms + EPS) * hg_ref[:, c0:c0 + n_chunk]
        o_ref[:, c0:c0 + n_chunk] = acc.astype(o_ref.dtype)


def norm_matmul(x, g, w, out_dtype, *, head_gain=None, head_dim=None):
    rows, d = x.shape
    n = w.shape[1]
    tm = _row_tile(rows)
    if head_gain is None:
        n_chunk = _col_chunk(n)
        head_cols = 0
        head_gain = jnp.zeros((1, n_chunk), F32)
        head_dim = n_chunk
    else:
        head_cols = head_gain.shape[1]
        n_chunk = _col_chunk(n, head_cols)
        assert n_chunk % head_dim == 0
    grp = jnp.arange(n_chunk) // head_dim
    gm = ((grp[:, None] == grp[None, :]).astype(F32) / head_dim).astype(BF16)
    kern = functools.partial(_norm_matmul_kernel, n_chunk=n_chunk, head_cols=head_cols)
    return pl.pallas_call(
        kern,
        grid=(rows // tm,),
        in_specs=[
            pl.BlockSpec((tm, d), lambda i: (i, 0)),
            pl.BlockSpec((1, d), lambda i: (0, 0)),
            pl.BlockSpec((d, n), lambda i: (0, 0)),
            pl.BlockSpec(head_gain.shape, lambda i: (0, 0)),
            pl.BlockSpec((n_chunk, n_chunk), lambda i: (0, 0)),
        ],
        out_specs=pl.BlockSpec((tm, n), lambda i: (i, 0)),
        out_shape=jax.ShapeDtypeStruct((rows, n), out_dtype),
        compiler_params=_cparams(1),
        name="norm_matmul",
    )(x, g, w, head_gain, gm)


def _matmul_res_kernel(a_ref, w_ref, r_ref, o_ref, *, n_chunk):
    a = a_ref[...]
    n = w_ref.shape[1]
    for c0 in range(0, n, n_chunk):
        o_ref[:, c0:c0 + n_chunk] = r_ref[:, c0:c0 + n_chunk] + _dot(a, w_ref[:, c0:c0 + n_chunk])


def matmul_residual(a, w, res):
    rows, k = a.shape
    n = w.shape[1]
    tm = _row_tile(rows)
    n_chunk = _col_chunk(n)
    return pl.pallas_call(
        functools.partial(_matmul_res_kernel, n_chunk=n_chunk),
        grid=(rows // tm,),
        in_specs=[
            pl.BlockSpec((tm, k), lambda i: (i, 0)),
            pl.BlockSpec((k, n), lambda i: (0, 0)),
            pl.BlockSpec((tm, n), lambda i: (i, 0)),
        ],
        out_specs=pl.BlockSpec((tm, n), lambda i: (i, 0)),
        out_shape=jax.ShapeDtypeStruct((rows, n), F32),
        compiler_params=_cparams(1),
        name="matmul_residual",
    )(a, w, res)


def _swiglu_acc(xn, wgu_ref, wd_ref, ff, f_chunk, widx=()):
    acc = None
    for c0 in range(0, ff, f_chunk):
        gt = _dot(xn, wgu_ref[widx + (slice(None), slice(c0, c0 + f_chunk))])
        up = _dot(xn, wgu_ref[widx + (slice(None), slice(ff + c0, ff + c0 + f_chunk))])
        act = (gt * (1.0 / (1.0 + jnp.exp(-gt))) * up).astype(BF16)
        part = _dot(act, wd_ref[widx + (slice(c0, c0 + f_chunk), slice(None))])
        acc = part if acc is None else acc + part
    return acc


def _dense_ffn_kernel(x_ref, g_ref, wgu_ref, wd_ref, o_ref, *, ff, f_chunk):
    xf = x_ref[...]
    xn = _rms(xf, g_ref[...]).astype(BF16)
    o_ref[...] = xf + _swiglu_acc(xn, wgu_ref, wd_ref, ff, f_chunk)


def dense_ffn(x, g, wgu, wd):
    rows, d = x.shape
    ff = wd.shape[0]
    tm = _row_tile(rows)
    return pl.pallas_call(
        functools.partial(_dense_ffn_kernel, ff=ff, f_chunk=FF_CHUNK),
        grid=(rows // tm,),
        in_specs=[
            pl.BlockSpec((tm, d), lambda i: (i, 0)),
            pl.BlockSpec((1, d), lambda i: (0, 0)),
            pl.BlockSpec((d, 2 * ff), lambda i: (0, 0)),
            pl.BlockSpec((ff, d), lambda i: (0, 0)),
        ],
        out_specs=pl.BlockSpec((tm, d), lambda i: (i, 0)),
        out_shape=jax.ShapeDtypeStruct((rows, d), F32),
        compiler_params=_cparams(1),
        name="dense_ffn",
    )(x, g, wgu, wd)


def _sb_suffix(z, mask, u_ref):
    log1m = -(jnp.maximum(z, 0.0) + jnp.log(1.0 + jnp.exp(-jnp.abs(z))))
    if mask is not None:
        log1m = jnp.where(mask, log1m, 0.0)
    h1, h2 = _split2(log1m)
    tq = z.shape[0]
    cs = _dot(jnp.concatenate([h1, h2], axis=0), u_ref[...])
    return cs[:tq] + cs[tq:]


def _sb_kernel(*refs, tq, tk, hd, n_meta, scale):
    z_scr, cs_scr, run_scr, acc_scr = refs[-4:]
    if n_meta:
        q_ref, k_ref, v_ref, km_ref, vm_ref, u_ref, o_ref = refs[:-4]
    else:
        q_ref, k_ref, v_ref, u_ref, o_ref = refs[:-4]
    s_len = q_ref.shape[1]
    n_q = s_len // tq
    n_diag = tq // tk
    has_meta = 1 if n_meta else 0
    lane = lax.broadcasted_iota(jnp.int32, (tq, LANES), 1)
    head0 = lane < hd
    row_in = lax.broadcasted_iota(jnp.int32, (tq, tk), 0)
    col_in = lax.broadcasted_iota(jnp.int32, (tq, tk), 1)
    META = "meta"

    def q_tile(i, n_real, static_tiles):
        r0 = pl.multiple_of(i * tq, tq)
        q = q_ref[0, pl.ds(r0, tq), :] * scale
        qh = (jnp.where(head0, q, jnp.zeros_like(q)), jnp.where(head0, jnp.zeros_like(q), q))

        def offset(n):
            return pl.multiple_of(r0 + (n_diag - 1 - n) * tk, tk)

        def mask_of(n):
            if n is META:
                return col_in < n_meta
            if isinstance(n, int) and n < n_diag:
                return (col_in + (n_diag - 1 - n) * tk) < row_in
            return None

        def load(n, ref, mref):
            return mref[...] if n is META else ref[0, pl.ds(offset(n), tk), :]

        def step(par, t1, t2, t3):
            if t3 is not None:
                v, m = load(t3, v_ref, vm_ref if n_meta else None), mask_of(t3)
                for h in range(2):
                    g = run_scr[h] + cs_scr[h]
                    a = jnp.exp(z_scr[par, h] + g)
                    if m is not None:
                        a = jnp.where(m, a, 0.0)
                    run_scr[h] = jnp.broadcast_to(g[:, 0:1], g.shape)
                    acc_scr[h] += _dot(a.astype(BF16), v)
            if t2 is not None:
                m = mask_of(t2)
                for h in range(2):
                    cs_scr[h] = _sb_suffix(z_scr[1 - par, h], m, u_ref)
            if t1 is not None:
                k = load(t1, k_ref, km_ref if n_meta else None)
                for h in range(2):
                    z_scr[par, h] = _dot_nt(qh[h], k)

        run_scr[...] = jnp.zeros_like(run_scr)
        acc_scr[...] = jnp.zeros_like(acc_scr)

        if static_tiles:
            tiles = list(range(n_real)) + ([META] if has_meta else [])
            for s in range(len(tiles) + 2):
                pick = lambda j: tiles[j] if 0 <= j < len(tiles) else None
                step(s % 2, pick(s), pick(s - 1), pick(s - 2))
        else:
            n_pro = n_diag + 2
            assert n_pro % 2 == 0
            for s in range(n_pro):
                pick = lambda j: j if j >= 0 else None
                step(s % 2, pick(s), pick(s - 1), pick(s - 2))

            def body(j, carry):
                s = n_pro + 2 * j
                step(0, s, s - 1, s - 2)
                step(1, s + 1, s, s - 1)
                return carry

            lax.fori_loop(0, (n_real - n_pro) // 2, body, 0)
            last = n_real - 1
            step(0, META if has_meta else None, last, last - 1)
            step(1, None, META if has_meta else None, last)
            if has_meta:
                step(0, None, None, META)

        o_ref[0, pl.ds(r0, tq), :] = jnp.where(head0, acc_scr[0], acc_scr[1]).astype(o_ref.dtype)

    q_tile(0, n_diag, True)
    if n_q > 1:
        assert n_diag >= 2, "the pipelined prologue needs two diagonal key tiles per query tile"

        def outer(i, carry):
            q_tile(i, n_diag * (i + 1), False)
            return carry

        lax.fori_loop(1, n_q, outer, 0)


def _suffix_matrix(t):
    r = jnp.arange(t)
    return (r[:, None] >= r[None, :]).astype(BF16)


def stickbreak_attention(qkv, qkv_meta, d_model, hd):
    b, s_len, _ = qkv.shape
    n_pair = d_model // LANES
    tq = min(ATT_TQ, s_len)
    tk = min(ATT_TK, s_len)
    n_meta = 0 if qkv_meta is None else qkv_meta.shape[0]
    col = lambda which: (lambda bi, hp: (bi, 0, which * n_pair + hp))
    in_specs = [pl.BlockSpec((1, s_len, LANES), col(0)),
                pl.BlockSpec((1, s_len, LANES), col(1)),
                pl.BlockSpec((1, s_len, LANES), col(2))]
    args = [qkv, qkv, qkv]
    if n_meta:
        assert n_meta <= tk
        meta_pad = jnp.zeros((tk, qkv_meta.shape[1]), qkv_meta.dtype).at[:n_meta].set(qkv_meta)
        in_specs += [pl.BlockSpec((tk, LANES), lambda bi, hp: (0, n_pair + hp)),
                     pl.BlockSpec((tk, LANES), lambda bi, hp: (0, 2 * n_pair + hp))]
        args += [meta_pad, meta_pad]
    in_specs.append(pl.BlockSpec((tk, tk), lambda bi, hp: (0, 0)))
    args.append(_suffix_matrix(tk))
    kern = functools.partial(_sb_kernel, tq=tq, tk=tk, hd=hd, n_meta=n_meta, scale=hd ** -0.5)
    return pl.pallas_call(
        kern,
        grid=(b, n_pair),
        in_specs=in_specs,
        out_specs=pl.BlockSpec((1, s_len, LANES), lambda bi, hp: (bi, 0, hp)),
        out_shape=jax.ShapeDtypeStruct((b, s_len, d_model), BF16),
        scratch_shapes=[pltpu.VMEM((2, 2, tq, tk), F32), pltpu.VMEM((2, tq, tk), F32),
                        pltpu.VMEM((2, tq, tk), F32), pltpu.VMEM((2, tq, LANES), F32)],
        compiler_params=_cparams(2),
        name="stickbreak_attention",
    )(*args)


def _log_sigmoid(y):
    return jnp.minimum(y, 0.0) - jnp.log(1.0 + jnp.exp(-jnp.abs(y)))


def _tri_cumsum(tri, x):
    h1, h2, h3 = _split3(x)
    return _dot(tri, h1) + _dot(tri, h2) + _dot(tri, h3)


def _forget_cumsum_kernel(f_ref, fm_ref, b_ref, tri_ref, trim_ref, o_ref, om_ref, *, chunk):
    bias = b_ref[...]
    fm = _tri_cumsum(trim_ref[...], _log_sigmoid(fm_ref[...] + bias))
    for p, part in enumerate(_split3(fm)):
        om_ref[p] = part
    n_meta = fm.shape[0]
    carry = fm[n_meta - 1:n_meta, :]
    s_len = f_ref.shape[1]
    for c0 in range(0, s_len, chunk):
        fc = _tri_cumsum(tri_ref[...], _log_sigmoid(f_ref[0, c0:c0 + chunk, :] + bias)) + carry
        for p, part in enumerate(_split3(fc)):
            o_ref[p, 0, c0:c0 + chunk, :] = part
        carry = fc[chunk - 1:chunk, :]


def forget_cumsum(f_logit, f_logit_meta, bias):
    b, s_len, w = f_logit.shape
    n_meta = f_logit_meta.shape[0]
    chunk = min(CUM_CHUNK, s_len)
    tri = lambda t: (jnp.arange(t)[:, None] >= jnp.arange(t)[None, :]).astype(BF16)
    return pl.pallas_call(
        functools.partial(_forget_cumsum_kernel, chunk=chunk),
        grid=(b,),
        in_specs=[
            pl.BlockSpec((1, s_len, w), lambda bi: (bi, 0, 0)),
            pl.BlockSpec((n_meta, w), lambda bi: (0, 0)),
            pl.BlockSpec((1, w), lambda bi: (0, 0)),
            pl.BlockSpec((chunk, chunk), lambda bi: (0, 0)),
            pl.BlockSpec((n_meta, n_meta), lambda bi: (0, 0)),
        ],
        out_specs=[pl.BlockSpec((3, 1, s_len, w), lambda bi: (0, bi, 0, 0)),
                   pl.BlockSpec((3, n_meta, w), lambda bi: (0, 0, 0))],
        out_shape=[jax.ShapeDtypeStruct((3, b, s_len, w), BF16),
                   jax.ShapeDtypeStruct((3, n_meta, w), BF16)],
        compiler_params=_cparams(1),
        name="forget_cumsum",
    )(f_logit, f_logit_meta, bias, tri(chunk), tri(n_meta))


def _fox_kernel(q_ref, k_ref, v_ref, km_ref, vm_ref, o_ref, z_scr, p_scr, m_scr, alpha_scr, acc_scr,
                *, tq, tk, hd, n_meta):
    s_len = q_ref.shape[1]
    n_q = s_len // tq
    n_diag = tq // tk
    lane = lax.broadcasted_iota(jnp.int32, (tq, LANES), 1)
    head0 = lane < hd
    row_in = lax.broadcasted_iota(jnp.int32, (tq, tk), 0)
    col_in = lax.broadcasted_iota(jnp.int32, (tq, tk), 1)
    META = "meta"
    hsl = lambda h: slice(h * LANES, (h + 1) * LANES)

    def q_tile(i, n_real, static_tiles):
        r0 = pl.multiple_of(i * tq, tq)
        qh = [q_ref[0, pl.ds(r0, tq), hsl(h)] for h in range(2)]

        def mask_of(t):
            if t is META:
                return col_in < n_meta
            return None if t[1] is None else (col_in + t[1] * tk) <= row_in

        def load(t, ref, mref, h):
            if t is META:
                return mref[:, hsl(h)]
            return ref[0, pl.ds(pl.multiple_of(t[0] * tk, tk), tk), hsl(h)]

        def step(par, t1, t2, t3):
            if t3 is not None:
                for h in range(2):
                    acc_scr[h] = acc_scr[h] * alpha_scr[h] + _dot(p_scr[h], load(t3, v_ref, vm_ref, h))
            if t2 is not None:
                m = mask_of(t2)
                for h in range(2):
                    z = z_scr[1 - par, h]
                    if m is not None:
                        z = jnp.where(m, z, -jnp.inf)
                    m_old = m_scr[h]
                    m_new = jnp.maximum(m_old, jnp.max(z, axis=-1, keepdims=True))
                    p_scr[h] = jnp.exp(z - m_new).astype(BF16)
                    alpha_scr[h] = jnp.exp(m_old - m_new)[:, :LANES]
                    m_scr[h] = m_new
            if t1 is not None:
                for h in range(2):
                    z_scr[par, h] = _dot_nt(qh[h], load(t1, k_ref, km_ref, h))

        m_scr[...] = jnp.full(m_scr.shape, -jnp.inf, F32)
        acc_scr[...] = jnp.zeros_like(acc_scr)

        if static_tiles:
            tiles = [META] + [(n, n - (n_real - n_diag) if n >= n_real - n_diag else None)
                              for n in range(n_real)]
            for s in range(len(tiles) + 2):
                pick = lambda j: tiles[j] if 0 <= j < len(tiles) else None
                step(s % 2, pick(s), pick(s - 1), pick(s - 2))
        else:
            pos = lambda p: META if (isinstance(p, int) and p == 0) else (p - 1, None)
            n_pro = 4
            for s in range(n_pro):
                pick = lambda j: pos(j) if j >= 0 else None
                step(s % 2, pick(s), pick(s - 1), pick(s - 2))

            def body(j, carry):
                s = n_pro + 2 * j
                step(0, pos(s), pos(s - 1), pos(s - 2))
                step(1, pos(s + 1), pos(s), pos(s - 1))
                return carry

            lax.fori_loop(0, (n_real - n_pro) // 2, body, 0)
            a, b = n_real - 2, n_real - 1
            step(0, (b, None), (a, 0), (a - 1, None))
            step(1, None, (b, 1), (a, None))
            step(0, None, None, (b, None))

        res0, res1 = acc_scr[0], acc_scr[1]
        rot0 = pltpu.roll(res0, hd, 1)
        rot1 = pltpu.roll(res1, hd, 1)
        o_ref[0, pl.ds(r0, tq), :] = jnp.where(head0, res0 / rot0, rot1 / res1).astype(o_ref.dtype)

    q_tile(0, n_diag, True)
    if n_q > 1:
        assert n_diag == 2, "the pipelined sweep is written for two diagonal key tiles per query tile"

        def outer(i, carry):
            q_tile(i, n_diag * (i + 1), False)
            return carry

        lax.fori_loop(1, n_q, outer, 0)


def forgetting_attention(q_ext, k_ext, v_ext, km_ext, vm_ext, d_model, hd):
    b, s_len, _ = q_ext.shape
    n_pair = d_model // LANES
    n_meta = km_ext.shape[0]
    tq = min(ATT_TQ, s_len)
    tk = min(ATT_TK, s_len)
    assert n_meta <= tk
    pad = lambda t: jnp.zeros((tk, t.shape[1]), t.dtype).at[:n_meta].set(t)
    blk = pl.BlockSpec((1, s_len, 2 * LANES), lambda bi, hp: (bi, 0, hp))
    mblk = pl.BlockSpec((tk, 2 * LANES), lambda bi, hp: (0, hp))
    return pl.pallas_call(
        functools.partial(_fox_kernel, tq=tq, tk=tk, hd=hd, n_meta=n_meta),
        grid=(b, n_pair),
        in_specs=[blk, blk, blk, mblk, mblk],
        out_specs=pl.BlockSpec((1, s_len, LANES), lambda bi, hp: (bi, 0, hp)),
        out_shape=jax.ShapeDtypeStruct((b, s_len, d_model), BF16),
        scratch_shapes=[pltpu.VMEM((2, 2, tq, tk), F32), pltpu.VMEM((2, tq, tk), BF16),
                        pltpu.VMEM((2, tq, tk), F32), pltpu.VMEM((2, tq, LANES), F32),
                        pltpu.VMEM((2, tq, LANES), F32)],
        compiler_params=_cparams(2),
        name="forgetting_attention",
    )(q_ext, k_ext, v_ext, pad(km_ext), pad(vm_ext))


def _fox_operands(q, k, v, f_q, f_k, hd):
    n_heads = q.shape[-1] // hd
    lead_q, lead_k = q.shape[:-1], k.shape[:-1]
    parts = lambda f: jnp.moveaxis(f, 0, -1)

    pad = LANES - hd - 2 * 3
    qh = q.reshape(lead_q + (n_heads, hd)) * jnp.asarray(hd ** -0.5, BF16)
    q_ext = jnp.concatenate([qh, parts(f_q), jnp.ones(lead_q + (n_heads, 3), BF16),
                             jnp.zeros(lead_q + (n_heads, pad), BF16)], axis=-1)
    kh = k.reshape(lead_k + (n_heads, hd))
    k_ext = jnp.concatenate([kh, jnp.ones(lead_k + (n_heads, 3), BF16), parts(-f_k),
                             jnp.zeros(lead_k + (n_heads, pad), BF16)], axis=-1)
    vh = v.reshape(lead_k + (n_heads, hd))
    v_ext = jnp.concatenate([vh, jnp.ones(lead_k + (n_heads, LANES - hd), BF16)], axis=-1)
    flat = lambda t: t.reshape(t.shape[:-2] + (n_heads * LANES,))
    return flat(q_ext), flat(k_ext), flat(v_ext)


def _router_kernel(x_ref, g_ref, wr_ref, xn_ref, cw_ref, *, n_experts):
    xn = _rms(x_ref[...], g_ref[...])
    xn_ref[...] = xn.astype(BF16)
    a1, a2, a3 = _split3(xn)
    b1, b2, b3 = _split3(wr_ref[...])
    logits = (_dot(a1, b1) + _dot(a1, b2) + _dot(a2, b1)
              + _dot(a2, b2) + _dot(a1, b3) + _dot(a3, b1))
    lane = lax.broadcasted_iota(jnp.int32, logits.shape, 1)
    lg = jnp.where(lane < n_experts, logits, -jnp.inf)
    m1 = jnp.max(lg, axis=-1, keepdims=True)
    i1 = jnp.min(jnp.where(lg == m1, lane, LANES), axis=-1, keepdims=True)
    lg2 = jnp.where(lane == i1, -jnp.inf, lg)
    m2 = jnp.max(lg2, axis=-1, keepdims=True)
    i2 = jnp.min(jnp.where(lg2 == m2, lane, LANES), axis=-1, keepdims=True)
    e2 = jnp.exp(m2 - m1)
    den = 1.0 + e2
    cw = (jnp.where(lane == i1, 1.0 / den, 0.0) + jnp.where(lane == i2, e2 / den, 0.0)
          + jnp.where(lane == n_experts, i1.astype(F32), 0.0)
          + jnp.where(lane == n_experts + 1, i2.astype(F32), 0.0))
    cw_ref[...] = cw


def moe_router(x, g, w_router):
    rows, d = x.shape
    n_experts = w_router.shape[1]
    wr = jnp.zeros((d, LANES), F32).at[:, :n_experts].set(w_router)
    tm = _row_tile(rows)
    return pl.pallas_call(
        functools.partial(_router_kernel, n_experts=n_experts),
        grid=(rows // tm,),
        in_specs=[
            pl.BlockSpec((tm, d), lambda i: (i, 0)),
            pl.BlockSpec((1, d), lambda i: (0, 0)),
            pl.BlockSpec((d, LANES), lambda i: (0, 0)),
        ],
        out_specs=[pl.BlockSpec((tm, d), lambda i: (i, 0)),
                   pl.BlockSpec((tm, LANES), lambda i: (i, 0))],
        out_shape=[jax.ShapeDtypeStruct((rows, d), BF16),
                   jax.ShapeDtypeStruct((rows, LANES), F32)],
        compiler_params=_cparams(1),
        name="moe_router",
    )(x, g, wr)


def _moe_ffn_kernel(te_ref, nt_ref, x_ref, gate_ref, wgu_ref, wd_ref, o_ref, *, ff, f_chunk):
    i = pl.program_id(0)

    @pl.when(i < nt_ref[0])
    def _():
        acc = _swiglu_acc(x_ref[...], wgu_ref, wd_ref, ff, f_chunk, widx=(0,))
        gate = gate_ref[...]
        d = acc.shape[1]
        for c0 in range(0, d, LANES):
            o_ref[:, c0:c0 + LANES] = (acc[:, c0:c0 + LANES] * gate).astype(o_ref.dtype)

    @pl.when(i >= nt_ref[0])
    def _():
        o_ref[...] = jnp.zeros_like(o_ref)


def moe_grouped_ffn(xs, gate, tile_expert, n_tiles_used, wgu, wd):
    p_rows, d = xs.shape
    n_experts, ff, _ = wd.shape
    tm = MOE_TM
    grid_spec = pltpu.PrefetchScalarGridSpec(
        num_scalar_prefetch=2,
        grid=(p_rows // tm,),
        in_specs=[
            pl.BlockSpec((tm, d), lambda i, te, nt: (i, 0)),
            pl.BlockSpec((tm, LANES), lambda i, te, nt: (i, 0)),
            pl.BlockSpec((1, d, 2 * ff), lambda i, te, nt: (te[i], 0, 0)),
            pl.BlockSpec((1, ff, d), lambda i, te, nt: (te[i], 0, 0)),
        ],
        out_specs=pl.BlockSpec((tm, d), lambda i, te, nt: (i, 0)),
    )
    return pl.pallas_call(
        functools.partial(_moe_ffn_kernel, ff=ff, f_chunk=FF_CHUNK),
        grid_spec=grid_spec,
        out_shape=jax.ShapeDtypeStruct((p_rows, d), BF16),
        compiler_params=_cparams(1),
        name="moe_grouped_ffn",
    )(tile_expert, n_tiles_used, xs, gate, wgu, wd)


def _combine_kernel(h_ref, y1_ref, y2_ref, o_ref):
    o_ref[...] = h_ref[...] + y1_ref[...].astype(F32) + y2_ref[...].astype(F32)


def moe_block(h, g, w_router, wgu, wd):
    rows, d = h.shape
    n_experts = w_router.shape[1]
    tm = MOE_TM
    xn, cw = moe_router(h, g, w_router)

    eid = cw[:, n_experts:n_experts + TOP_K].astype(jnp.int32)
    gates = jnp.take_along_axis(cw[:, :n_experts], eid, axis=1)
    sel = (eid[:, :, None] == jnp.arange(n_experts)[None, None, :]).any(axis=1)
    rank = jnp.cumsum(sel.astype(jnp.int32), axis=0) - 1
    cnt = rank[-1] + 1
    cnt_pad = ((cnt + tm - 1) // tm) * tm
    ends = jnp.cumsum(cnt_pad)
    offs = ends - cnt_pad
    pos = offs[eid] + jnp.take_along_axis(rank, eid, axis=1)
    p_rows = rows * TOP_K + n_experts * tm
    tok = jnp.broadcast_to(jnp.arange(rows, dtype=jnp.int32)[:, None], (rows, TOP_K))
    tok_sorted = jnp.zeros((p_rows,), jnp.int32).at[pos.reshape(-1)].set(tok.reshape(-1))
    gate_sorted = jnp.zeros((p_rows,), F32).at[pos.reshape(-1)].set(gates.reshape(-1))
    n_tiles = p_rows // tm
    tile_start = jnp.arange(n_tiles, dtype=jnp.int32) * tm
    tile_expert = jnp.minimum((tile_start[:, None] >= ends[None, :]).sum(axis=1), n_experts - 1)
    n_used = (ends[-1] // tm).astype(jnp.int32).reshape(1)

    xs = jnp.take(xn, tok_sorted, axis=0)
    gate_rep = jnp.broadcast_to(gate_sorted[:, None], (p_rows, LANES))
    ys = moe_grouped_ffn(xs, gate_rep, tile_expert.astype(jnp.int32), n_used, wgu, wd)
    y1 = jnp.take(ys, pos[:, 0], axis=0)
    y2 = jnp.take(ys, pos[:, 1], axis=0)
    tr = _row_tile(rows)
    spec = pl.BlockSpec((tr, d), lambda i: (i, 0))
    return pl.pallas_call(
        _combine_kernel,
        grid=(rows // tr,),
        in_specs=[spec, spec, spec],
        out_specs=spec,
        out_shape=jax.ShapeDtypeStruct((rows, d), F32),
        compiler_params=_cparams(1),
        name="moe_combine",
    )(h, y1, y2)


def kernel(x, meta_tokens, norm_attn_a, w_qkv_a, w_o_a, norm_kv, w_kvf, b_f, k_norm, norm_attn_b,
           w_q_b, q_norm_b, w_o_b, norm_ffn_dense, w_gu_dense, w_down_dense, norm_ffn_moe,
           w_router, w_gu_moe, w_down_moe):
    b, s_len, d = x.shape
    n_heads = b_f.shape[0]
    hd = d // n_heads
    n_meta = meta_tokens.shape[0]
    depth = norm_attn_a.shape[0] + norm_attn_b.shape[0]
    n_a = norm_attn_a.shape[0]
    assert 2 * hd == LANES and d % LANES == 0
    row = lambda v: v.reshape(1, -1).astype(F32)

    h = x.reshape(b * s_len, d)
    hm = meta_tokens.astype(x.dtype)
    k_sh = None
    for layer in range(depth):
        last = layer == depth - 1
        if layer < n_a:
            w_qkv = w_qkv_a[layer].astype(BF16)
            g = row(norm_attn_a[layer])
            qkv = norm_matmul(h, g, w_qkv, BF16).reshape(b, s_len, 3 * d)
            qkv_m = norm_matmul(hm, g, w_qkv, BF16)
            o = stickbreak_attention(qkv, qkv_m, d, hd).reshape(b * s_len, d)
            o_m = stickbreak_attention(qkv_m[None], None, d, hd)[0]
            w_o = w_o_a[layer].astype(BF16)
            h = matmul_residual(o, w_o, h)
            hm = matmul_residual(o_m, w_o, hm)
        else:
            if layer == n_a:
                g = row(norm_kv)
                w_kv = w_kvf[:, :2 * d].astype(BF16)
                w_f = jnp.zeros((d, LANES), F32).at[:, :n_heads].set(w_kvf[:, 2 * d:]).astype(BF16)
                kgain = row(jnp.tile(k_norm, n_heads))
                kv = norm_matmul(h, g, w_kv, BF16, head_gain=kgain, head_dim=hd)
                kv_m = norm_matmul(hm, g, w_kv, BF16, head_gain=kgain, head_dim=hd)
                fl = norm_matmul(h, g, w_f, F32).reshape(b, s_len, LANES)
                fl_m = norm_matmul(hm, g, w_f, F32)
                bias = jnp.zeros((1, LANES), F32).at[0, :n_heads].set(b_f.astype(F32))
                f_cum, f_cum_m = forget_cumsum(fl, fl_m, bias)
                f_cum = f_cum[..., :n_heads].reshape(3, b * s_len, n_heads)
                f_cum_m = f_cum_m[..., :n_heads]
                k_sh, v_sh = kv[:, :d], kv[:, d:]
                k_sh_m, v_sh_m = kv_m[:, :d], kv_m[:, d:]
            i = layer - n_a
            qgain = row(jnp.tile(q_norm_b[i], n_heads))
            q = norm_matmul(h, row(norm_attn_b[i]), w_q_b[i].astype(BF16), BF16,
                            head_gain=qgain, head_dim=hd)
            q_ext, k_ext, v_ext = _fox_operands(q, k_sh, v_sh, f_cum, f_cum, hd)
            _, km_ext, vm_ext = _fox_operands(k_sh_m, k_sh_m, v_sh_m, f_cum_m, f_cum_m, hd)
            three = lambda t: t.reshape(b, s_len, -1)
            o = forgetting_attention(three(q_ext), three(k_ext), three(v_ext), km_ext, vm_ext, d, hd)
            h = matmul_residual(o.reshape(b * s_len, d), w_o_b[i].astype(BF16), h)
            if not last:
                raise NotImplementedError("meta-row queries in forgetting layers before the last")
        j = layer // 2
        if layer % 2 == 0:
            wgu = w_gu_dense[j].astype(BF16)
            wd = w_down_dense[j].astype(BF16)
            g = row(norm_ffn_dense[j])
            h = dense_ffn(h, g, wgu, wd)
            if not last:
                hm = dense_ffn(hm, g, wgu, wd)
        else:
            h = moe_block(h, row(norm_ffn_moe[j]), w_router[j],
                          w_gu_moe[j].astype(BF16), w_down_moe[j].astype(BF16))
            if not last:
                raise NotImplementedError("meta rows through a MoE layer before the last")
    return h.reshape(b, s_len, d)
```

```python
import functools

import jax
import jax.numpy as jnp
import numpy as np
from jax import lax
from jax.experimental import pallas as pl
from jax.experimental.pallas import tpu as pltpu

F32 = jnp.float32
BF16 = jnp.bfloat16

EPS = 1e-6
TOP_K = 2
LANES = 128
VMEM_LIMIT = 56 * 1024 * 1024
ROW_TILE = 512
COL_CHUNK = 512
FF_CHUNK = 256
ATT_TQ = 256
ATT_TK = 128
MOE_TM = 512
CUM_CHUNK = 256


def _cparams(n_axes):
    return pltpu.CompilerParams(dimension_semantics=("arbitrary",) * n_axes,
                                vmem_limit_bytes=VMEM_LIMIT)


def _split3(x):
    h1 = x.astype(BF16)
    r1 = x - h1.astype(F32)
    h2 = r1.astype(BF16)
    h3 = (r1 - h2.astype(F32)).astype(BF16)
    return h1, h2, h3


def _split2(x):
    h1 = x.astype(BF16)
    h2 = (x - h1.astype(F32)).astype(BF16)
    return h1, h2


def _dot(a, b):
    return jnp.dot(a, b, preferred_element_type=F32)


def _dot_nt(a, b):
    return lax.dot_general(a, b, (((1,), (1,)), ((), ())), preferred_element_type=F32)


def _rms(xf, g):
    ms = jnp.mean(xf * xf, axis=-1, keepdims=True)
    return xf * lax.rsqrt(ms + EPS) * g


def _row_tile(rows, pref=ROW_TILE):
    return pref if rows % pref == 0 else rows


def _col_chunk(*widths):
    c = COL_CHUNK
    while any(w % c for w in widths):
        c //= 2
    assert c >= LANES
    return c


def _norm_matmul_kernel(x_ref, g_ref, w_ref, hg_ref, gm_ref, o_ref, *, n_chunk, head_cols):
    xn = _rms(x_ref[...], g_ref[...]).astype(BF16)
    n = w_ref.shape[1]
    for c0 in range(0, n, n_chunk):
        acc = _dot(xn, w_ref[:, c0:c0 + n_chunk])
        if c0 < head_cols:
            s1, s2 = _split2(acc * acc)
            ms = _dot(s1, gm_ref[...]) + _dot(s2, gm_ref[...])
            acc = acc * lax.rsqrt(ms + EPS) * hg_ref[:, c0:c0 + n_chunk]
        o_ref[:, c0:c0 + n_chunk] = acc.astype(o_ref.dtype)


def norm_matmul(x, g, w, out_dtype, *, head_gain=None, head_dim=None):
    rows, d = x.shape
    n = w.shape[1]
    tm = _row_tile(rows)
    if head_gain is None:
        n_chunk = _col_chunk(n)
        head_cols = 0
        head_gain = jnp.zeros((1, n_chunk), F32)
        head_dim = n_chunk
    else:
        head_cols = head_gain.shape[1]
        n_chunk = _col_chunk(n, head_cols)
        assert n_chunk % head_dim == 0
    grp = jnp.arange(n_chunk) // head_dim
    gm = ((grp[:, None] == grp[None, :]).astype(F32) / head_dim).astype(BF16)
    kern = functools.partial(_norm_matmul_kernel, n_chunk=n_chunk, head_cols=head_cols)
    return pl.pallas_call(
        kern,
        grid=(rows // tm,),
        in_specs=[
            pl.BlockSpec((tm, d), lambda i: (i, 0)),
            pl.BlockSpec((1, d), lambda i: (0, 0)),
            pl.BlockSpec((d, n), lambda i: (0, 0)),
            pl.BlockSpec(head_gain.shape, lambda i: (0, 0)),
            pl.BlockSpec((n_chunk, n_chunk), lambda i: (0, 0)),
        ],
        out_specs=pl.BlockSpec((tm, n), lambda i: (i, 0)),
        out_shape=jax.ShapeDtypeStruct((rows, n), out_dtype),
        compiler_params=_cparams(1),
        name="norm_matmul",
    )(x, g, w, head_gain, gm)


def _matmul_res_kernel(a_ref, w_ref, r_ref, o_ref, *, n_chunk):
    a = a_ref[...]
    n = w_ref.shape[1]
    for c0 in range(0, n, n_chunk):
        o_ref[:, c0:c0 + n_chunk] = r_ref[:, c0:c0 + n_chunk] + _dot(a, w_ref[:, c0:c0 + n_chunk])


def matmul_residual(a, w, res):
    rows, k = a.shape
    n = w.shape[1]
    tm = _row_tile(rows)
    n_chunk = _col_chunk(n)
    return pl.pallas_call(
        functools.partial(_matmul_res_kernel, n_chunk=n_chunk),
        grid=(rows // tm,),
        in_specs=[
            pl.BlockSpec((tm, k), lambda i: (i, 0)),
            pl.BlockSpec((k, n), lambda i: (0, 0)),
            pl.BlockSpec((tm, n), lambda i: (i, 0)),
        ],
        out_specs=pl.BlockSpec((tm, n), lambda i: (i, 0)),
        out_shape=jax.ShapeDtypeStruct((rows, n), F32),
        compiler_params=_cparams(1),
        name="matmul_residual",
    )(a, w, res)


def _swiglu_acc(xn, wgu_ref, wd_ref, ff, f_chunk, widx=()):
    acc = None
    for c0 in range(0, ff, f_chunk):
        gt = _dot(xn, wgu_ref[widx + (slice(None), slice(c0, c0 + f_chunk))])
        up = _dot(xn, wgu_ref[widx + (slice(None), slice(ff + c0, ff + c0 + f_chunk))])
        act = (gt * (1.0 / (1.0 + jnp.exp(-gt))) * up).astype(BF16)
        part = _dot(act, wd_ref[widx + (slice(c0, c0 + f_chunk), slice(None))])
        acc = part if acc is None else acc + part
    return acc


def _dense_ffn_kernel(x_ref, g_ref, wgu_ref, wd_ref, o_ref, *, ff, f_chunk):
    xf = x_ref[...]
    xn = _rms(xf, g_ref[...]).astype(BF16)
    o_ref[...] = xf + _swiglu_acc(xn, wgu_ref, wd_ref, ff, f_chunk)


def dense_ffn(x, g, wgu, wd):
    rows, d = x.shape
    ff = wd.shape[0]
    tm = _row_tile(rows)
    return pl.pallas_call(
        functools.partial(_dense_ffn_kernel, ff=ff, f_chunk=FF_CHUNK),
        grid=(rows // tm,),
        in_specs=[
            pl.BlockSpec((tm, d), lambda i: (i, 0)),
            pl.BlockSpec((1, d), lambda i: (0, 0)),
            pl.BlockSpec((d, 2 * ff), lambda i: (0, 0)),
            pl.BlockSpec((ff, d), lambda i: (0, 0)),
        ],
        out_specs=pl.BlockSpec((tm, d), lambda i: (i, 0)),
        out_shape=jax.ShapeDtypeStruct((rows, d), F32),
        compiler_params=_cparams(1),
        name="dense_ffn",
    )(x, g, wgu, wd)


def _sb_suffix(z, mask, u_ref):
    log1m = -(jnp.maximum(z, 0.0) + jnp.log(1.0 + jnp.exp(-jnp.abs(z))))
    if mask is not None:
        log1m = jnp.where(mask, log1m, 0.0)
    h1, h2 = _split2(log1m)
    tq = z.shape[0]
    cs = _dot(jnp.concatenate([h1, h2], axis=0), u_ref[...])
    return cs[:tq] + cs[tq:]


def _sb_kernel(*refs, tq, tk, hd, n_meta, scale):
    z_scr, cs_scr, run_scr, acc_scr = refs[-4:]
    if n_meta:
        q_ref, k_ref, v_ref, km_ref, vm_ref, u_ref, o_ref = refs[:-4]
    else:
        q_ref, k_ref, v_ref, u_ref, o_ref = refs[:-4]
    s_len = q_ref.shape[1]
    n_q = s_len // tq
    n_diag = tq // tk
    has_meta = 1 if n_meta else 0
    lane = lax.broadcasted_iota(jnp.int32, (tq, LANES), 1)
    head0 = lane < hd
    row_in = lax.broadcasted_iota(jnp.int32, (tq, tk), 0)
    col_in = lax.broadcasted_iota(jnp.int32, (tq, tk), 1)
    META = "meta"

    def q_tile(i, n_real, static_tiles):
        r0 = pl.multiple_of(i * tq, tq)
        q = q_ref[0, pl.ds(r0, tq), :] * scale
        qh = (jnp.where(head0, q, jnp.zeros_like(q)), jnp.where(head0, jnp.zeros_like(q), q))

        def offset(n):
            return pl.multiple_of(r0 + (n_diag - 1 - n) * tk, tk)

        def mask_of(n):
            if n is META:
                return col_in < n_meta
            if isinstance(n, int) and n < n_diag:
                return (col_in + (n_diag - 1 - n) * tk) < row_in
            return None

        def load(n, ref, mref):
            return mref[...] if n is META else ref[0, pl.ds(offset(n), tk), :]

        def step(par, t1, t2, t3):
            if t3 is not None:
                v, m = load(t3, v_ref, vm_ref if n_meta else None), mask_of(t3)
                for h in range(2):
                    g = run_scr[h] + cs_scr[h]
                    a = jnp.exp(z_scr[par, h] + g)
                    if m is not None:
                        a = jnp.where(m, a, 0.0)
                    run_scr[h] = jnp.broadcast_to(g[:, 0:1], g.shape)
                    acc_scr[h] += _dot(a.astype(BF16), v)
            if t2 is not None:
                m = mask_of(t2)
                for h in range(2):
                    cs_scr[h] = _sb_suffix(z_scr[1 - par, h], m, u_ref)
            if t1 is not None:
                k = load(t1, k_ref, km_ref if n_meta else None)
                for h in range(2):
                    z_scr[par, h] = _dot_nt(qh[h], k)

        run_scr[...] = jnp.zeros_like(run_scr)
        acc_scr[...] = jnp.zeros_like(acc_scr)

        if static_tiles:
            tiles = list(range(n_real)) + ([META] if has_meta else [])
            for s in range(len(tiles) + 2):
                pick = lambda j: tiles[j] if 0 <= j < len(tiles) else None
                step(s % 2, pick(s), pick(s - 1), pick(s - 2))
        else:
            n_pro = n_diag + 2
            assert n_pro % 2 == 0
            for s in range(n_pro):
                pick = lambda j: j if j >= 0 else None
                step(s % 2, pick(s), pick(s - 1), pick(s - 2))

            def body(j, carry):
                s = n_pro + 2 * j
                step(0, s, s - 1, s - 2)
                step(1, s + 1, s, s - 1)
                return carry

            lax.fori_loop(0, (n_real - n_pro) // 2, body, 0)
            last = n_real - 1
            step(0, META if has_meta else None, last, last - 1)
            step(1, None, META if has_meta else None, last)
            if has_meta:
                step(0, None, None, META)

        o_ref[0, pl.ds(r0, tq), :] = jnp.where(head0, acc_scr[0], acc_scr[1]).astype(o_ref.dtype)

    q_tile(0, n_diag, True)
    if n_q > 1:
        assert n_diag >= 2, "the pipelined prologue needs two diagonal key tiles per query tile"

        def outer(i, carry):
            q_tile(i, n_diag * (i + 1), False)
            return carry

        lax.fori_loop(1, n_q, outer, 0)


def _suffix_matrix(t):
    r = jnp.arange(t)
    return (r[:, None] >= r[None, :]).astype(BF16)


def stickbreak_attention(qkv, qkv_meta, d_model, hd):
    b, s_len, _ = qkv.shape
    n_pair = d_model // LANES
    tq = min(ATT_TQ, s_len)
    tk = min(ATT_TK, s_len)
    n_meta = 0 if qkv_meta is None else qkv_meta.shape[0]
    col = lambda which: (lambda bi, hp: (bi, 0, which * n_pair + hp))
    in_specs = [pl.BlockSpec((1, s_len, LANES), col(0)),
                pl.BlockSpec((1, s_len, LANES), col(1)),
                pl.BlockSpec((1, s_len, LANES), col(2))]
    args = [qkv, qkv, qkv]
    if n_meta:
        assert n_meta <= tk
        meta_pad = jnp.zeros((tk, qkv_meta.shape[1]), qkv_meta.dtype).at[:n_meta].set(qkv_meta)
        in_specs += [pl.BlockSpec((tk, LANES), lambda bi, hp: (0, n_pair + hp)),
                     pl.BlockSpec((tk, LANES), lambda bi, hp: (0, 2 * n_pair + hp))]
        args += [meta_pad, meta_pad]
    in_specs.append(pl.BlockSpec((tk, tk), lambda bi, hp: (0, 0)))
    args.append(_suffix_matrix(tk))
    kern = functools.partial(_sb_kernel, tq=tq, tk=tk, hd=hd, n_meta=n_meta, scale=hd ** -0.5)
    return pl.pallas_call(
        kern,
        grid=(b, n_pair),
        in_specs=in_specs,
        out_specs=pl.BlockSpec((1, s_len, LANES), lambda bi, hp: (bi, 0, hp)),
        out_shape=jax.ShapeDtypeStruct((b, s_len, d_model), BF16),
        scratch_shapes=[pltpu.VMEM((2, 2, tq, tk), F32), pltpu.VMEM((2, tq, tk), F32),
                        pltpu.VMEM((2, tq, tk), F32), pltpu.VMEM((2, tq, LANES), F32)],
        compiler_params=_cparams(2),
        name="stickbreak_attention",
    )(*args)


def _log_sigmoid(y):
    return jnp.minimum(y, 0.0) - jnp.log(1.0 + jnp.exp(-jnp.abs(y)))


def _tri_cumsum(tri, x):
    h1, h2, h3 = _split3(x)
    return _dot(tri, h1) + _dot(tri, h2) + _dot(tri, h3)


def _forget_cumsum_kernel(f_ref, fm_ref, b_ref, tri_ref, trim_ref, o_ref, om_ref, *, chunk):
    bias = b_ref[...]
    fm = _tri_cumsum(trim_ref[...], _log_sigmoid(fm_ref[...] + bias))
    for p, part in enumerate(_split3(fm)):
        om_ref[p] = part
    n_meta = fm.shape[0]
    carry = fm[n_meta - 1:n_meta, :]
    s_len = f_ref.shape[1]
    for c0 in range(0, s_len, chunk):
        fc = _tri_cumsum(tri_ref[...], _log_sigmoid(f_ref[0, c0:c0 + chunk, :] + bias)) + carry
        for p, part in enumerate(_split3(fc)):
            o_ref[p, 0, c0:c0 + chunk, :] = part
        carry = fc[chunk - 1:chunk, :]


def forget_cumsum(f_logit, f_logit_meta, bias):
    b, s_len, w = f_logit.shape
    n_meta = f_logit_meta.shape[0]
    chunk = min(CUM_CHUNK, s_len)
    tri = lambda t: (jnp.arange(t)[:, None] >= jnp.arange(t)[None, :]).astype(BF16)
    return pl.pallas_call(
        functools.partial(_forget_cumsum_kernel, chunk=chunk),
        grid=(b,),
        in_specs=[
            pl.BlockSpec((1, s_len, w), lambda bi: (bi, 0, 0)),
            pl.BlockSpec((n_meta, w), lambda bi: (0, 0)),
            pl.BlockSpec((1, w), lambda bi: (0, 0)),
            pl.BlockSpec((chunk, chunk), lambda bi: (0, 0)),
            pl.BlockSpec((n_meta, n_meta), lambda bi: (0, 0)),
        ],
        out_specs=[pl.BlockSpec((3, 1, s_len, w), lambda bi: (0, bi, 0, 0)),
                   pl.BlockSpec((3, n_meta, w), lambda bi: (0, 0, 0))],
        out_shape=[jax.ShapeDtypeStruct((3, b, s_len, w), BF16),
                   jax.ShapeDtypeStruct((3, n_meta, w), BF16)],
        compiler_params=_cparams(1),
        name="forget_cumsum",
    )(f_logit, f_logit_meta, bias, tri(chunk), tri(n_meta))


def _fox_kernel(qc_ref, kc_ref, vc_ref, f_ref, kmc_ref, vmc_ref, fm_ref, pq_ref, pk_ref, pv_ref,
                cq_ref, ck_ref, cv_ref, o_ref,
                q_ref, k_ref, v_ref, km_ref, vm_ref, z_scr, p_scr, m_scr, alpha_scr, acc_scr,
                *, tq, tk, hd, n_meta):
    s_len = qc_ref.shape[1]
    n_q = s_len // tq
    n_diag = tq // tk
    lane = lax.broadcasted_iota(jnp.int32, (tq, LANES), 1)
    head0 = lane < hd
    row_in = lax.broadcasted_iota(jnp.int32, (tq, tk), 0)
    col_in = lax.broadcasted_iota(jnp.int32, (tq, tk), 1)
    META = "meta"
    hsl = lambda h: slice(h * LANES, (h + 1) * LANES)

    def place(main, fparts, p_ref, c_ref):
        lhs = main if fparts is None else jnp.concatenate([main] + fparts, axis=1)
        return (_dot(lhs, p_ref[0]) + c_ref[...]).astype(BF16)

    for c0 in range(0, s_len, tq):
        rows = slice(c0, c0 + tq)
        fp = [f_ref[p, 0, rows, :] for p in range(3)]
        q_ref[rows, :] = place(qc_ref[0, rows, :], fp, pq_ref, cq_ref)
        k_ref[rows, :] = place(kc_ref[0, rows, :], fp, pk_ref, ck_ref)
        v_ref[rows, :] = place(vc_ref[0, rows, :], None, pv_ref, cv_ref)
    fpm = [fm_ref[p] for p in range(3)]
    km_ref[...] = place(kmc_ref[...], fpm, pk_ref, ck_ref)
    vm_ref[...] = place(vmc_ref[...], None, pv_ref, cv_ref)

    def q_tile(i, n_real, static_tiles):
        r0 = pl.multiple_of(i * tq, tq)
        qh = [q_ref[pl.ds(r0, tq), hsl(h)] for h in range(2)]

        def mask_of(t):
            if t is META:
                return col_in < n_meta
            return None if t[1] is None else (col_in + t[1] * tk) <= row_in

        def load(t, ref, mref, h):
            if t is META:
                return mref[:, hsl(h)]
            return ref[pl.ds(pl.multiple_of(t[0] * tk, tk), tk), hsl(h)]

        def step(par, t1, t2, t3):
            if t3 is not None:
                for h in range(2):
                    acc_scr[h] = acc_scr[h] * alpha_scr[h] + _dot(p_scr[h], load(t3, v_ref, vm_ref, h))
            if t2 is not None:
                m = mask_of(t2)
                for h in range(2):
                    z = z_scr[1 - par, h]
                    if m is not None:
                        z = jnp.where(m, z, -jnp.inf)
                    m_old = m_scr[h]
                    m_new = jnp.maximum(m_old, jnp.max(z, axis=-1, keepdims=True))
                    p_scr[h] = jnp.exp(z - m_new).astype(BF16)
                    alpha_scr[h] = jnp.exp(m_old - m_new)[:, :LANES]
                    m_scr[h] = m_new
            if t1 is not None:
                for h in range(2):
                    z_scr[par, h] = _dot_nt(qh[h], load(t1, k_ref, km_ref, h))

        m_scr[...] = jnp.full(m_scr.shape, -jnp.inf, F32)
        acc_scr[...] = jnp.zeros_like(acc_scr)

        if static_tiles:
            tiles = [META] + [(n, n - (n_real - n_diag) if n >= n_real - n_diag else None)
                              for n in range(n_real)]
            for s in range(len(tiles) + 2):
                pick = lambda j: tiles[j] if 0 <= j < len(tiles) else None
                step(s % 2, pick(s), pick(s - 1), pick(s - 2))
        else:
            pos = lambda p: META if (isinstance(p, int) and p == 0) else (p - 1, None)
            n_pro = 4
            for s in range(n_pro):
                pick = lambda j: pos(j) if j >= 0 else None
                step(s % 2, pick(s), pick(s - 1), pick(s - 2))

            def body(j, carry):
                s = n_pro + 2 * j
                step(0, pos(s), pos(s - 1), pos(s - 2))
                step(1, pos(s + 1), pos(s), pos(s - 1))
                return carry

            lax.fori_loop(0, (n_real - n_pro) // 2, body, 0)
            a, b = n_real - 2, n_real - 1
            step(0, (b, None), (a, 0), (a - 1, None))
            step(1, None, (b, 1), (a, None))
            step(0, None, None, (b, None))

        res0, res1 = acc_scr[0], acc_scr[1]
        rot0 = pltpu.roll(res0, hd, 1)
        rot1 = pltpu.roll(res1, hd, 1)
        o_ref[0, pl.ds(r0, tq), :] = jnp.where(head0, res0 / rot0, rot1 / res1).astype(o_ref.dtype)

    q_tile(0, n_diag, True)
    if n_q > 1:
        assert n_diag == 2, "the pipelined sweep is written for two diagonal key tiles per query tile"

        def outer(i, carry):
            q_tile(i, n_diag * (i + 1), False)
            return carry

        lax.fori_loop(1, n_q, outer, 0)


def _fox_placement(n_pair, hd):
    pq = np.zeros((n_pair, 4 * LANES, 2 * LANES), np.float32)
    pk = np.zeros((n_pair, 4 * LANES, 2 * LANES), np.float32)
    pv = np.zeros((1, LANES, 2 * LANES), np.float32)
    cq = np.zeros((1, 2 * LANES), np.float32)
    ck = np.zeros((1, 2 * LANES), np.float32)
    cv = np.zeros((1, 2 * LANES), np.float32)
    for h in range(2):
        base = h * LANES
        for j in range(hd):
            pq[:, h * hd + j, base + j] = hd ** -0.5
            pk[:, h * hd + j, base + j] = 1.0
            pv[:, h * hd + j, base + j] = 1.0
        for p in range(3):
            for hp in range(n_pair):
                pq[hp, LANES + p * LANES + 2 * hp + h, base + hd + p] = 1.0
                pk[hp, LANES + p * LANES + 2 * hp + h, base + hd + 3 + p] = -1.0
        cq[0, base + hd + 3:base + hd + 6] = 1.0
        ck[0, base + hd:base + hd + 3] = 1.0
        cv[0, base + hd:base + LANES] = 1.0
    bf = lambda a: jnp.asarray(a, BF16)
    return bf(pq), bf(pk), bf(pv), jnp.asarray(cq), jnp.asarray(ck), jnp.asarray(cv)


def forgetting_attention(q, kv, kv_meta, fparts, fparts_meta, d_model, hd):
    b, s_len, _ = q.shape
    n_pair = d_model // LANES
    n_meta = kv_meta.shape[0]
    tq = min(ATT_TQ, s_len)
    tk = min(ATT_TK, s_len)
    assert n_meta <= tk and tk == LANES and 2 * n_pair <= LANES
    kvm = jnp.zeros((tk, kv_meta.shape[1]), kv_meta.dtype).at[:n_meta].set(kv_meta)
    fpm = jnp.zeros((3, tk, LANES), fparts_meta.dtype).at[:, :n_meta].set(fparts_meta)
    pq, pk, pv, cq, ck, cv = _fox_placement(n_pair, hd)
    blk = lambda off: pl.BlockSpec((1, s_len, LANES), lambda bi, hp: (bi, 0, off + hp))
    mblk = lambda off: pl.BlockSpec((tk, LANES), lambda bi, hp: (0, off + hp))
    crow = pl.BlockSpec((1, 2 * LANES), lambda bi, hp: (0, 0))
    in_specs = [blk(0), blk(0), blk(n_pair),
                pl.BlockSpec((3, 1, s_len, LANES), lambda bi, hp: (0, bi, 0, 0)),
                mblk(0), mblk(n_pair),
                pl.BlockSpec((3, tk, LANES), lambda bi, hp: (0, 0, 0)),
                pl.BlockSpec((1, 4 * LANES, 2 * LANES), lambda bi, hp: (hp, 0, 0)),
                pl.BlockSpec((1, 4 * LANES, 2 * LANES), lambda bi, hp: (hp, 0, 0)),
                pl.BlockSpec((1, LANES, 2 * LANES), lambda bi, hp: (0, 0, 0)),
                crow, crow, crow]
    ext = lambda rows: pltpu.VMEM((rows, 2 * LANES), BF16)
    return pl.pallas_call(
        functools.partial(_fox_kernel, tq=tq, tk=tk, hd=hd, n_meta=n_meta),
        grid=(b, n_pair),
        in_specs=in_specs,
        out_specs=pl.BlockSpec((1, s_len, LANES), lambda bi, hp: (bi, 0, hp)),
        out_shape=jax.ShapeDtypeStruct((b, s_len, d_model), BF16),
        scratch_shapes=[ext(s_len), ext(s_len), ext(s_len), ext(tk), ext(tk),
                        pltpu.VMEM((2, 2, tq, tk), F32), pltpu.VMEM((2, tq, tk), BF16),
                        pltpu.VMEM((2, tq, tk), F32), pltpu.VMEM((2, tq, LANES), F32),
                        pltpu.VMEM((2, tq, LANES), F32)],
        compiler_params=_cparams(2),
        name="forgetting_attention",
    )(q, kv, kv, fparts, kvm, kvm, fpm, pq, pk, pv, cq, ck, cv)


def _router_kernel(x_ref, g_ref, wr_ref, xn_ref, id_ref, g1_ref, g2_ref, *, n_experts):
    xn = _rms(x_ref[...], g_ref[...])
    xn_ref[...] = xn.astype(BF16)
    a1, a2, a3 = _split3(xn)
    b1, b2, b3 = _split3(wr_ref[...])
    logits = (_dot(a1, b1) + _dot(a1, b2) + _dot(a2, b1)
              + _dot(a2, b2) + _dot(a1, b3) + _dot(a3, b1))
    lane = lax.broadcasted_iota(jnp.int32, logits.shape, 1)
    lg = jnp.where(lane < n_experts, logits, -jnp.inf)
    m1 = jnp.max(lg, axis=-1, keepdims=True)
    i1 = jnp.min(jnp.where(lg == m1, lane, LANES), axis=-1, keepdims=True)
    lg2 = jnp.where(lane == i1, -jnp.inf, lg)
    m2 = jnp.max(lg2, axis=-1, keepdims=True)
    i2 = jnp.min(jnp.where(lg2 == m2, lane, LANES), axis=-1, keepdims=True)
    e2 = jnp.exp(m2 - m1)
    den = 1.0 + e2
    id_ref[...] = jnp.where(lane == 0, i1, jnp.where(lane == 1, i2, 0))
    g1_ref[...] = jnp.broadcast_to(1.0 / den, logits.shape)
    g2_ref[...] = jnp.broadcast_to(e2 / den, logits.shape)


def moe_router(x, g, w_router):
    rows, d = x.shape
    n_experts = w_router.shape[1]
    wr = jnp.zeros((d, LANES), F32).at[:, :n_experts].set(w_router)
    tm = _row_tile(rows)
    return pl.pallas_call(
        functools.partial(_router_kernel, n_experts=n_experts),
        grid=(rows // tm,),
        in_specs=[
            pl.BlockSpec((tm, d), lambda i: (i, 0)),
            pl.BlockSpec((1, d), lambda i: (0, 0)),
            pl.BlockSpec((d, LANES), lambda i: (0, 0)),
        ],
        out_specs=[pl.BlockSpec((tm, d), lambda i: (i, 0)),
                   pl.BlockSpec((tm, LANES), lambda i: (i, 0)),
                   pl.BlockSpec((tm, LANES), lambda i: (i, 0)),
                   pl.BlockSpec((tm, LANES), lambda i: (i, 0))],
        out_shape=[jax.ShapeDtypeStruct((rows, d), BF16),
                   jax.ShapeDtypeStruct((rows, LANES), jnp.int32),
                   jax.ShapeDtypeStruct((rows, LANES), F32),
                   jax.ShapeDtypeStruct((rows, LANES), F32)],
        compiler_params=_cparams(1),
        name="moe_router",
    )(x, g, wr)


def _moe_ffn_kernel(te_ref, nt_ref, x_ref, wgu_ref, wd_ref, o_ref, *, ff, f_chunk):
    i = pl.program_id(0)

    @pl.when(i < nt_ref[0])
    def _():
        acc = _swiglu_acc(x_ref[...], wgu_ref, wd_ref, ff, f_chunk, widx=(0,))
        o_ref[...] = acc.astype(o_ref.dtype)

    @pl.when(i >= nt_ref[0])
    def _():
        o_ref[...] = jnp.zeros_like(o_ref)


def moe_grouped_ffn(xs, tile_expert, n_tiles_used, wgu, wd):
    p_rows, d = xs.shape
    n_experts, ff, _ = wd.shape
    tm = MOE_TM
    grid_spec = pltpu.PrefetchScalarGridSpec(
        num_scalar_prefetch=2,
        grid=(p_rows // tm,),
        in_specs=[
            pl.BlockSpec((tm, d), lambda i, te, nt: (i, 0)),
            pl.BlockSpec((1, d, 2 * ff), lambda i, te, nt: (te[i], 0, 0)),
            pl.BlockSpec((1, ff, d), lambda i, te, nt: (te[i], 0, 0)),
        ],
        out_specs=pl.BlockSpec((tm, d), lambda i, te, nt: (i, 0)),
    )
    return pl.pallas_call(
        functools.partial(_moe_ffn_kernel, ff=ff, f_chunk=FF_CHUNK),
        grid_spec=grid_spec,
        out_shape=jax.ShapeDtypeStruct((p_rows, d), BF16),
        compiler_params=_cparams(1),
        name="moe_grouped_ffn",
    )(tile_expert, n_tiles_used, xs, wgu, wd)


def _combine_kernel(h_ref, y1_ref, y2_ref, g1_ref, g2_ref, o_ref):
    d = h_ref.shape[1]
    g1, g2 = g1_ref[...], g2_ref[...]
    for c0 in range(0, d, LANES):
        cs = slice(c0, c0 + LANES)
        o_ref[:, cs] = (h_ref[:, cs] + g1 * y1_ref[:, cs].astype(F32)
                        + g2 * y2_ref[:, cs].astype(F32))


def moe_block(h, g, w_router, wgu, wd):
    rows, d = h.shape
    n_experts = w_router.shape[1]
    tm = MOE_TM
    xn, ids, g1, g2 = moe_router(h, g, w_router)

    eid = ids[:, :TOP_K]
    sel = (eid[:, :, None] == jnp.arange(n_experts)[None, None, :]).any(axis=1)
    rank = jnp.cumsum(sel.astype(jnp.int32), axis=0) - 1
    cnt = rank[-1] + 1
    cnt_pad = ((cnt + tm - 1) // tm) * tm
    ends = jnp.cumsum(cnt_pad)
    offs = ends - cnt_pad
    pos = offs[eid] + jnp.take_along_axis(rank, eid, axis=1)
    p_rows = rows * TOP_K + n_experts * tm
    n_tiles = p_rows // tm
    tile_start = jnp.arange(n_tiles, dtype=jnp.int32) * tm
    tile_expert = jnp.minimum((tile_start[:, None] >= ends[None, :]).sum(axis=1), n_experts - 1)
    n_used = (ends[-1] // tm).astype(jnp.int32).reshape(1)

    scatter = dict(mode="promise_in_bounds", unique_indices=True)
    xs = jnp.zeros((p_rows, d), BF16)
    for k in range(TOP_K):
        xs = xs.at[pos[:, k]].set(xn, **scatter)
    ys = moe_grouped_ffn(xs, tile_expert.astype(jnp.int32), n_used, wgu, wd)
    y1 = ys.at[pos[:, 0]].get(mode="promise_in_bounds", unique_indices=True)
    y2 = ys.at[pos[:, 1]].get(mode="promise_in_bounds", unique_indices=True)
    tr = _row_tile(rows)
    spec = pl.BlockSpec((tr, d), lambda i: (i, 0))
    gspec = pl.BlockSpec((tr, LANES), lambda i: (i, 0))
    return pl.pallas_call(
        _combine_kernel,
        grid=(rows // tr,),
        in_specs=[spec, spec, spec, gspec, gspec],
        out_specs=spec,
        out_shape=jax.ShapeDtypeStruct((rows, d), F32),
        compiler_params=_cparams(1),
        name="moe_combine",
    )(h, y1, y2, g1, g2)


def kernel(x, meta_tokens, norm_attn_a, w_qkv_a, w_o_a, norm_kv, w_kvf, b_f, k_norm, norm_attn_b,
           w_q_b, q_norm_b, w_o_b, norm_ffn_dense, w_gu_dense, w_down_dense, norm_ffn_moe,
           w_router, w_gu_moe, w_down_moe):
    b, s_len, d = x.shape
    n_heads = b_f.shape[0]
    hd = d // n_heads
    n_meta = meta_tokens.shape[0]
    depth = norm_attn_a.shape[0] + norm_attn_b.shape[0]
    n_a = norm_attn_a.shape[0]
    assert 2 * hd == LANES and d % LANES == 0
    row = lambda v: v.reshape(1, -1).astype(F32)

    h = x.reshape(b * s_len, d)
    hm = meta_tokens.astype(x.dtype)
    for layer in range(depth):
        last = layer == depth - 1
        if layer < n_a:
            w_qkv = w_qkv_a[layer].astype(BF16)
            g = row(norm_attn_a[layer])
            qkv = norm_matmul(h, g, w_qkv, BF16).reshape(b, s_len, 3 * d)
            qkv_m = norm_matmul(hm, g, w_qkv, BF16)
            o = stickbreak_attention(qkv, qkv_m, d, hd).reshape(b * s_len, d)
            o_m = stickbreak_attention(qkv_m[None], None, d, hd)[0]
            w_o = w_o_a[layer].astype(BF16)
            h = matmul_residual(o, w_o, h)
            hm = matmul_residual(o_m, w_o, hm)
        else:
            if layer == n_a:
                g = row(norm_kv)
                w_kv = w_kvf[:, :2 * d].astype(BF16)
                w_f = jnp.zeros((d, LANES), F32).at[:, :n_heads].set(w_kvf[:, 2 * d:]).astype(BF16)
                kgain = row(jnp.tile(k_norm, n_heads))
                kv = norm_matmul(h, g, w_kv, BF16, head_gain=kgain, head_dim=hd)
                kv_m = norm_matmul(hm, g, w_kv, BF16, head_gain=kgain, head_dim=hd)
                fl = norm_matmul(h, g, w_f, F32).reshape(b, s_len, LANES)
                fl_m = norm_matmul(hm, g, w_f, F32)
                bias = jnp.zeros((1, LANES), F32).at[0, :n_heads].set(b_f.astype(F32))
                f_cum, f_cum_m = forget_cumsum(fl, fl_m, bias)
                kv_sh = kv.reshape(b, s_len, 2 * d)
            i = layer - n_a
            qgain = row(jnp.tile(q_norm_b[i], n_heads))
            q = norm_matmul(h, row(norm_attn_b[i]), w_q_b[i].astype(BF16), BF16,
                            head_gain=qgain, head_dim=hd)
            o = forgetting_attention(q.reshape(b, s_len, d), kv_sh, kv_m, f_cum, f_cum_m, d, hd)
            h = matmul_residual(o.reshape(b * s_len, d), w_o_b[i].astype(BF16), h)
            if not last:
                raise NotImplementedError("meta-row queries in forgetting layers before the last")
        j = layer // 2
        if layer % 2 == 0:
            wgu = w_gu_dense[j].astype(BF16)
            wd = w_down_dense[j].astype(BF16)
            g = row(norm_ffn_dense[j])
            h = dense_ffn(h, g, wgu, wd)
            if not last:
                hm = dense_ffn(hm, g, wgu, wd)
        else:
            h = moe_block(h, row(norm_ffn_moe[j]), w_router[j],
                          w_gu_moe[j].astype(BF16), w_down_moe[j].astype(BF16))
            if not last:
                raise NotImplementedError("meta rows through a MoE layer before the last")
    return h.reshape(b, s_len, d)
```

```python
import functools

import jax
import jax.numpy as jnp
import numpy as np
from jax import lax
from jax.experimental import pallas as pl
from jax.experimental.pallas import tpu as pltpu

F32 = jnp.float32
BF16 = jnp.bfloat16

EPS = 1e-6
LOG2E = 1.4426950408889634
TOP_K = 2
LANES = 128
VMEM_LIMIT = 56 * 1024 * 1024
ROW_TILE = 512
COL_CHUNK = 512
FF_CHUNK = 256
ATT_TQ = 256
ATT_TK = 128
MOE_TM = 512
CUM_CHUNK = 256


def _cparams(n_axes):
    return pltpu.CompilerParams(dimension_semantics=("arbitrary",) * n_axes,
                                vmem_limit_bytes=VMEM_LIMIT)


def _split3(x):
    h1 = x.astype(BF16)
    r1 = x - h1.astype(F32)
    h2 = r1.astype(BF16)
    h3 = (r1 - h2.astype(F32)).astype(BF16)
    return h1, h2, h3


def _split2(x):
    h1 = x.astype(BF16)
    h2 = (x - h1.astype(F32)).astype(BF16)
    return h1, h2


def _dot(a, b):
    return jnp.dot(a, b, preferred_element_type=F32)


def _dot_nt(a, b):
    return lax.dot_general(a, b, (((1,), (1,)), ((), ())), preferred_element_type=F32)


def _rms(xf, g):
    ms = jnp.mean(xf * xf, axis=-1, keepdims=True)
    return xf * lax.rsqrt(ms + EPS) * g


def _row_tile(rows, pref=ROW_TILE):
    return pref if rows % pref == 0 else rows


def _col_chunk(*widths):
    c = COL_CHUNK
    while any(w % c for w in widths):
        c //= 2
    assert c >= LANES
    return c


def _norm_matmul_kernel(x_ref, g_ref, w_ref, hg_ref, gm_ref, o_ref, *, n_chunk, head_cols):
    xn = _rms(x_ref[...], g_ref[...]).astype(BF16)
    n = w_ref.shape[1]
    for c0 in range(0, n, n_chunk):
        acc = _dot(xn, w_ref[:, c0:c0 + n_chunk])
        if c0 < head_cols:
            s1, s2 = _split2(acc * acc)
            ms = _dot(s1, gm_ref[...]) + _dot(s2, gm_ref[...])
            acc = acc * lax.rsqrt(ms + EPS) * hg_ref[:, c0:c0 + n_chunk]
        o_ref[:, c0:c0 + n_chunk] = acc.astype(o_ref.dtype)


def norm_matmul(x, g, w, out_dtype, *, head_gain=None, head_dim=None):
    rows, d = x.shape
    n = w.shape[1]
    tm = _row_tile(rows)
    if head_gain is None:
        n_chunk = _col_chunk(n)
        head_cols = 0
        head_gain = jnp.zeros((1, n_chunk), F32)
        head_dim = n_chunk
    else:
        head_cols = head_gain.shape[1]
        n_chunk = _col_chunk(n, head_cols)
        assert n_chunk % head_dim == 0
    grp = jnp.arange(n_chunk) // head_dim
    gm = ((grp[:, None] == grp[None, :]).astype(F32) / head_dim).astype(BF16)
    kern = functools.partial(_norm_matmul_kernel, n_chunk=n_chunk, head_cols=head_cols)
    return pl.pallas_call(
        kern,
        grid=(rows // tm,),
        in_specs=[
            pl.BlockSpec((tm, d), lambda i: (i, 0)),
            pl.BlockSpec((1, d), lambda i: (0, 0)),
            pl.BlockSpec((d, n), lambda i: (0, 0)),
            pl.BlockSpec(head_gain.shape, lambda i: (0, 0)),
            pl.BlockSpec((n_chunk, n_chunk), lambda i: (0, 0)),
        ],
        out_specs=pl.BlockSpec((tm, n), lambda i: (i, 0)),
        out_shape=jax.ShapeDtypeStruct((rows, n), out_dtype),
        compiler_params=_cparams(1),
        name="norm_matmul",
    )(x, g, w, head_gain, gm)


def _matmul_res_kernel(a_ref, w_ref, r_ref, o_ref, *, n_chunk):
    a = a_ref[...]
    n = w_ref.shape[1]
    for c0 in range(0, n, n_chunk):
        o_ref[:, c0:c0 + n_chunk] = r_ref[:, c0:c0 + n_chunk] + _dot(a, w_ref[:, c0:c0 + n_chunk])


def matmul_residual(a, w, res):
    rows, k = a.shape
    n = w.shape[1]
    tm = _row_tile(rows)
    n_chunk = _col_chunk(n)
    return pl.pallas_call(
        functools.partial(_matmul_res_kernel, n_chunk=n_chunk),
        grid=(rows // tm,),
        in_specs=[
            pl.BlockSpec((tm, k), lambda i: (i, 0)),
            pl.BlockSpec((k, n), lambda i: (0, 0)),
            pl.BlockSpec((tm, n), lambda i: (i, 0)),
        ],
        out_specs=pl.BlockSpec((tm, n), lambda i: (i, 0)),
        out_shape=jax.ShapeDtypeStruct((rows, n), F32),
        compiler_params=_cparams(1),
        name="matmul_residual",
    )(a, w, res)


def _swiglu_acc(xn, wgu_ref, wd_ref, ff, f_chunk, widx=()):
    acc = None
    for c0 in range(0, ff, f_chunk):
        gt = _dot(xn, wgu_ref[widx + (slice(None), slice(c0, c0 + f_chunk))])
        up = _dot(xn, wgu_ref[widx + (slice(None), slice(ff + c0, ff + c0 + f_chunk))])
        act = (gt * (1.0 / (1.0 + jnp.exp(-gt))) * up).astype(BF16)
        part = _dot(act, wd_ref[widx + (slice(c0, c0 + f_chunk), slice(None))])
        acc = part if acc is None else acc + part
    return acc


def _dense_ffn_kernel(x_ref, g_ref, wgu_ref, wd_ref, o_ref, *, ff, f_chunk):
    xf = x_ref[...]
    xn = _rms(xf, g_ref[...]).astype(BF16)
    o_ref[...] = xf + _swiglu_acc(xn, wgu_ref, wd_ref, ff, f_chunk)


def dense_ffn(x, g, wgu, wd):
    rows, d = x.shape
    ff = wd.shape[0]
    tm = _row_tile(rows)
    return pl.pallas_call(
        functools.partial(_dense_ffn_kernel, ff=ff, f_chunk=FF_CHUNK),
        grid=(rows // tm,),
        in_specs=[
            pl.BlockSpec((tm, d), lambda i: (i, 0)),
            pl.BlockSpec((1, d), lambda i: (0, 0)),
            pl.BlockSpec((d, 2 * ff), lambda i: (0, 0)),
            pl.BlockSpec((ff, d), lambda i: (0, 0)),
        ],
        out_specs=pl.BlockSpec((tm, d), lambda i: (i, 0)),
        out_shape=jax.ShapeDtypeStruct((rows, d), F32),
        compiler_params=_cparams(1),
        name="dense_ffn",
    )(x, g, wgu, wd)


def _sb_suffix(z, mask, u_ref):
    sp = jnp.maximum(z, 0.0) + jnp.log(1.0 + jnp.exp2(jnp.abs(z) * -LOG2E))
    if mask is not None:
        sp = jnp.where(mask, sp, 0.0)
    h1, h2 = _split2(sp)
    tq = z.shape[0]
    cs = _dot(jnp.concatenate([h1, h2], axis=0), u_ref[...])
    return cs[:tq] + cs[tq:]


def _sb_kernel(*refs, tq, tk, hd, n_meta, scale):
    z_scr, cs_scr, run_scr, acc_scr = refs[-4:]
    if n_meta:
        q_ref, k_ref, v_ref, km_ref, vm_ref, u_ref, o_ref = refs[:-4]
    else:
        q_ref, k_ref, v_ref, u_ref, o_ref = refs[:-4]
    s_len = q_ref.shape[1]
    n_q = s_len // tq
    n_diag = tq // tk
    has_meta = 1 if n_meta else 0
    lane = lax.broadcasted_iota(jnp.int32, (tq, LANES), 1)
    head0 = lane < hd
    row_in = lax.broadcasted_iota(jnp.int32, (tq, tk), 0)
    col_in = lax.broadcasted_iota(jnp.int32, (tq, tk), 1)
    META = "meta"

    def q_tile(i, n_real, static_tiles):
        r0 = pl.multiple_of(i * tq, tq)
        q = q_ref[0, pl.ds(r0, tq), :] * scale
        qh = (jnp.where(head0, q, jnp.zeros_like(q)), jnp.where(head0, jnp.zeros_like(q), q))

        def offset(n):
            return pl.multiple_of(r0 + (n_diag - 1 - n) * tk, tk)

        def mask_of(n):
            if n is META:
                return col_in < n_meta
            if isinstance(n, int) and n < n_diag:
                return (col_in + (n_diag - 1 - n) * tk) < row_in
            return None

        def load(n, ref, mref):
            return mref[...] if n is META else ref[0, pl.ds(offset(n), tk), :]

        def step(par, t1, t2, t3):
            if t3 is not None:
                v, m = load(t3, v_ref, vm_ref if n_meta else None), mask_of(t3)
                for h in range(2):
                    g = run_scr[h] + cs_scr[h]
                    a = jnp.exp(z_scr[par, h] + g)
                    if m is not None:
                        a = jnp.where(m, a, 0.0)
                    run_scr[h] = jnp.broadcast_to(g[:, 0:1], g.shape)
                    acc_scr[h] += _dot(a.astype(BF16), v)
            if t2 is not None:
                m = mask_of(t2)
                for h in range(2):
                    cs_scr[h] = _sb_suffix(z_scr[1 - par, h], m, u_ref)
            if t1 is not None:
                k = load(t1, k_ref, km_ref if n_meta else None)
                for h in range(2):
                    z_scr[par, h] = _dot_nt(qh[h], k)

        run_scr[...] = jnp.zeros_like(run_scr)
        acc_scr[...] = jnp.zeros_like(acc_scr)

        if static_tiles:
            tiles = list(range(n_real)) + ([META] if has_meta else [])
            for s in range(len(tiles) + 2):
                pick = lambda j: tiles[j] if 0 <= j < len(tiles) else None
                step(s % 2, pick(s), pick(s - 1), pick(s - 2))
        else:
            n_pro = n_diag + 2
            assert n_pro % 2 == 0
            for s in range(n_pro):
                pick = lambda j: j if j >= 0 else None
                step(s % 2, pick(s), pick(s - 1), pick(s - 2))

            def body(j, carry):
                s = n_pro + 2 * j
                step(0, s, s - 1, s - 2)
                step(1, s + 1, s, s - 1)
                return carry

            lax.fori_loop(0, (n_real - n_pro) // 2, body, 0)
            last = n_real - 1
            step(0, META if has_meta else None, last, last - 1)
            step(1, None, META if has_meta else None, last)
            if has_meta:
                step(0, None, None, META)

        o_ref[0, pl.ds(r0, tq), :] = jnp.where(head0, acc_scr[0], acc_scr[1]).astype(o_ref.dtype)

    q_tile(0, n_diag, True)
    if n_q > 1:
        assert n_diag >= 2, "the pipelined prologue needs two diagonal key tiles per query tile"

        def outer(i, carry):
            q_tile(i, n_diag * (i + 1), False)
            return carry

        lax.fori_loop(1, n_q, outer, 0)


def _suffix_matrix(t):
    r = jnp.arange(t)
    return -(r[:, None] >= r[None, :]).astype(BF16)


def stickbreak_attention(qkv, qkv_meta, d_model, hd):
    b, s_len, _ = qkv.shape
    n_pair = d_model // LANES
    tq = min(ATT_TQ, s_len)
    tk = min(ATT_TK, s_len)
    n_meta = 0 if qkv_meta is None else qkv_meta.shape[0]
    col = lambda which: (lambda bi, hp: (bi, 0, which * n_pair + hp))
    in_specs = [pl.BlockSpec((1, s_len, LANES), col(0)),
                pl.BlockSpec((1, s_len, LANES), col(1)),
                pl.BlockSpec((1, s_len, LANES), col(2))]
    args = [qkv, qkv, qkv]
    if n_meta:
        assert n_meta <= tk
        meta_pad = jnp.zeros((tk, qkv_meta.shape[1]), qkv_meta.dtype).at[:n_meta].set(qkv_meta)
        in_specs += [pl.BlockSpec((tk, LANES), lambda bi, hp: (0, n_pair + hp)),
                     pl.BlockSpec((tk, LANES), lambda bi, hp: (0, 2 * n_pair + hp))]
        args += [meta_pad, meta_pad]
    in_specs.append(pl.BlockSpec((tk, tk), lambda bi, hp: (0, 0)))
    args.append(_suffix_matrix(tk))
    kern = functools.partial(_sb_kernel, tq=tq, tk=tk, hd=hd, n_meta=n_meta, scale=hd ** -0.5)
    return pl.pallas_call(
        kern,
        grid=(b, n_pair),
        in_specs=in_specs,
        out_specs=pl.BlockSpec((1, s_len, LANES), lambda bi, hp: (bi, 0, hp)),
        out_shape=jax.ShapeDtypeStruct((b, s_len, d_model), BF16),
        scratch_shapes=[pltpu.VMEM((2, 2, tq, tk), F32), pltpu.VMEM((2, tq, tk), F32),
                        pltpu.VMEM((2, tq, tk), F32), pltpu.VMEM((2, tq, LANES), F32)],
        compiler_params=_cparams(2),
        name="stickbreak_attention",
    )(*args)


def _log_sigmoid(y):
    return jnp.minimum(y, 0.0) - jnp.log(1.0 + jnp.exp(-jnp.abs(y)))


def _tri_cumsum(tri, x):
    h1, h2, h3 = _split3(x)
    return _dot(tri, h1) + _dot(tri, h2) + _dot(tri, h3)


def _forget_cumsum_kernel(f_ref, fm_ref, b_ref, tri_ref, trim_ref, o_ref, om_ref, *, chunk):
    bias = b_ref[...]
    fm = _tri_cumsum(trim_ref[...], _log_sigmoid(fm_ref[...] + bias))
    for p, part in enumerate(_split3(fm)):
        om_ref[p] = part
    n_meta = fm.shape[0]
    carry = fm[n_meta - 1:n_meta, :]
    s_len = f_ref.shape[1]
    for c0 in range(0, s_len, chunk):
        fc = _tri_cumsum(tri_ref[...], _log_sigmoid(f_ref[0, c0:c0 + chunk, :] + bias)) + carry
        for p, part in enumerate(_split3(fc)):
            o_ref[p, 0, c0:c0 + chunk, :] = part
        carry = fc[chunk - 1:chunk, :]


def forget_cumsum(f_logit, f_logit_meta, bias):
    b, s_len, w = f_logit.shape
    n_meta = f_logit_meta.shape[0]
    chunk = min(CUM_CHUNK, s_len)
    tri = lambda t: (jnp.arange(t)[:, None] >= jnp.arange(t)[None, :]).astype(BF16)
    return pl.pallas_call(
        functools.partial(_forget_cumsum_kernel, chunk=chunk),
        grid=(b,),
        in_specs=[
            pl.BlockSpec((1, s_len, w), lambda bi: (bi, 0, 0)),
            pl.BlockSpec((n_meta, w), lambda bi: (0, 0)),
            pl.BlockSpec((1, w), lambda bi: (0, 0)),
            pl.BlockSpec((chunk, chunk), lambda bi: (0, 0)),
            pl.BlockSpec((n_meta, n_meta), lambda bi: (0, 0)),
        ],
        out_specs=[pl.BlockSpec((3, 1, s_len, w), lambda bi: (0, bi, 0, 0)),
                   pl.BlockSpec((3, n_meta, w), lambda bi: (0, 0, 0))],
        out_shape=[jax.ShapeDtypeStruct((3, b, s_len, w), BF16),
                   jax.ShapeDtypeStruct((3, n_meta, w), BF16)],
        compiler_params=_cparams(1),
        name="forget_cumsum",
    )(f_logit, f_logit_meta, bias, tri(chunk), tri(n_meta))


def _fox_kernel(qc_ref, kc_ref, vc_ref, f_ref, kmc_ref, vmc_ref, fm_ref, pq_ref, pk_ref, pv_ref,
                cq_ref, ck_ref, cv_ref, o_ref,
                q_ref, k_ref, v_ref, km_ref, vm_ref, z_scr, p_scr, m_scr, alpha_scr, acc_scr,
                *, tq, tk, hd, n_meta):
    s_len = qc_ref.shape[1]
    n_q = s_len // tq
    n_diag = tq // tk
    lane = lax.broadcasted_iota(jnp.int32, (tq, LANES), 1)
    head0 = lane < hd
    row_in = lax.broadcasted_iota(jnp.int32, (tq, tk), 0)
    col_in = lax.broadcasted_iota(jnp.int32, (tq, tk), 1)
    META = "meta"
    hsl = lambda h: slice(h * LANES, (h + 1) * LANES)

    def place(main, fparts, p_ref, c_ref):
        lhs = main if fparts is None else jnp.concatenate([main] + fparts, axis=1)
        return (_dot(lhs, p_ref[0]) + c_ref[...]).astype(BF16)

    for c0 in range(0, s_len, tq):
        rows = slice(c0, c0 + tq)
        fp = [f_ref[p, 0, rows, :] for p in range(3)]
        q_ref[rows, :] = place(qc_ref[0, rows, :], fp, pq_ref, cq_ref)
        k_ref[rows, :] = place(kc_ref[0, rows, :], fp, pk_ref, ck_ref)
        v_ref[rows, :] = place(vc_ref[0, rows, :], None, pv_ref, cv_ref)
    fpm = [fm_ref[p] for p in range(3)]
    km_ref[...] = place(kmc_ref[...], fpm, pk_ref, ck_ref)
    vm_ref[...] = place(vmc_ref[...], None, pv_ref, cv_ref)

    def q_tile(i, n_real, static_tiles):
        r0 = pl.multiple_of(i * tq, tq)
        qh = [q_ref[pl.ds(r0, tq), hsl(h)] for h in range(2)]

        def mask_of(t):
            if t is META:
                return col_in < n_meta
            return None if t[1] is None else (col_in + t[1] * tk) <= row_in

        def load(t, ref, mref, h):
            if t is META:
                return mref[:, hsl(h)]
            return ref[pl.ds(pl.multiple_of(t[0] * tk, tk), tk), hsl(h)]

        def step(par, t1, t2, t3):
            if t3 is not None:
                for h in range(2):
                    acc_scr[h] = acc_scr[h] * alpha_scr[h] + _dot(p_scr[h], load(t3, v_ref, vm_ref, h))
            if t2 is not None:
                m = mask_of(t2)
                for h in range(2):
                    z = z_scr[1 - par, h]
                    if m is not None:
                        z = jnp.where(m, z, -jnp.inf)
                    m_old = m_scr[h]
                    m_new = jnp.maximum(m_old, jnp.max(z, axis=-1, keepdims=True))
                    p_scr[h] = jnp.exp(z - m_new).astype(BF16)
                    alpha_scr[h] = jnp.exp(m_old - m_new)[:, :LANES]
                    m_scr[h] = m_new
            if t1 is not None:
                for h in range(2):
                    z_scr[par, h] = _dot_nt(qh[h], load(t1, k_ref, km_ref, h))

        m_scr[...] = jnp.full(m_scr.shape, -jnp.inf, F32)
        acc_scr[...] = jnp.zeros_like(acc_scr)

        if static_tiles:
            tiles = [META] + [(n, n - (n_real - n_diag) if n >= n_real - n_diag else None)
                              for n in range(n_real)]
            for s in range(len(tiles) + 2):
                pick = lambda j: tiles[j] if 0 <= j < len(tiles) else None
                step(s % 2, pick(s), pick(s - 1), pick(s - 2))
        else:
            pos = lambda p: META if (isinstance(p, int) and p == 0) else (p - 1, None)
            n_pro = 4
            for s in range(n_pro):
                pick = lambda j: pos(j) if j >= 0 else None
                step(s % 2, pick(s), pick(s - 1), pick(s - 2))

            def body(j, carry):
                s = n_pro + 2 * j
                step(0, pos(s), pos(s - 1), pos(s - 2))
                step(1, pos(s + 1), pos(s), pos(s - 1))
                return carry

            lax.fori_loop(0, (n_real - n_pro) // 2, body, 0)
            a, b = n_real - 2, n_real - 1
            step(0, (b, None), (a, 0), (a - 1, None))
            step(1, None, (b, 1), (a, None))
            step(0, None, None, (b, None))

        res0, res1 = acc_scr[0], acc_scr[1]
        rot0 = pltpu.roll(res0, hd, 1)
        rot1 = pltpu.roll(res1, hd, 1)
        o_ref[0, pl.ds(r0, tq), :] = jnp.where(head0, res0 / rot0, rot1 / res1).astype(o_ref.dtype)

    q_tile(0, n_diag, True)
    if n_q > 1:
        assert n_diag == 2, "the pipelined sweep is written for two diagonal key tiles per query tile"

        def outer(i, carry):
            q_tile(i, n_diag * (i + 1), False)
            return carry

        lax.fori_loop(1, n_q, outer, 0)


def _fox_placement(n_pair, hd):
    pq = np.zeros((n_pair, 4 * LANES, 2 * LANES), np.float32)
    pk = np.zeros((n_pair, 4 * LANES, 2 * LANES), np.float32)
    pv = np.zeros((1, LANES, 2 * LANES), np.float32)
    cq = np.zeros((1, 2 * LANES), np.float32)
    ck = np.zeros((1, 2 * LANES), np.float32)
    cv = np.zeros((1, 2 * LANES), np.float32)
    for h in range(2):
        base = h * LANES
        for j in range(hd):
            pq[:, h * hd + j, base + j] = hd ** -0.5
            pk[:, h * hd + j, base + j] = 1.0
            pv[:, h * hd + j, base + j] = 1.0
        for p in range(3):
            for hp in range(n_pair):
                pq[hp, LANES + p * LANES + 2 * hp + h, base + hd + p] = 1.0
                pk[hp, LANES + p * LANES + 2 * hp + h, base + hd + 3 + p] = -1.0
        cq[0, base + hd + 3:base + hd + 6] = 1.0
        ck[0, base + hd:base + hd + 3] = 1.0
        cv[0, base + hd:base + LANES] = 1.0
    bf = lambda a: jnp.asarray(a, BF16)
    return bf(pq), bf(pk), bf(pv), jnp.asarray(cq), jnp.asarray(ck), jnp.asarray(cv)


def forgetting_attention(q, kv, kv_meta, fparts, fparts_meta, d_model, hd):
    b, s_len, _ = q.shape
    n_pair = d_model // LANES
    n_meta = kv_meta.shape[0]
    tq = min(ATT_TQ, s_len)
    tk = min(ATT_TK, s_len)
    assert n_meta <= tk and tk == LANES and 2 * n_pair <= LANES
    kvm = jnp.zeros((tk, kv_meta.shape[1]), kv_meta.dtype).at[:n_meta].set(kv_meta)
    fpm = jnp.zeros((3, tk, LANES), fparts_meta.dtype).at[:, :n_meta].set(fparts_meta)
    pq, pk, pv, cq, ck, cv = _fox_placement(n_pair, hd)
    blk = lambda off: pl.BlockSpec((1, s_len, LANES), lambda bi, hp: (bi, 0, off + hp))
    mblk = lambda off: pl.BlockSpec((tk, LANES), lambda bi, hp: (0, off + hp))
    crow = pl.BlockSpec((1, 2 * LANES), lambda bi, hp: (0, 0))
    in_specs = [blk(0), blk(0), blk(n_pair),
                pl.BlockSpec((3, 1, s_len, LANES), lambda bi, hp: (0, bi, 0, 0)),
                mblk(0), mblk(n_pair),
                pl.BlockSpec((3, tk, LANES), lambda bi, hp: (0, 0, 0)),
                pl.BlockSpec((1, 4 * LANES, 2 * LANES), lambda bi, hp: (hp, 0, 0)),
                pl.BlockSpec((1, 4 * LANES, 2 * LANES), lambda bi, hp: (hp, 0, 0)),
                pl.BlockSpec((1, LANES, 2 * LANES), lambda bi, hp: (0, 0, 0)),
                crow, crow, crow]
    ext = lambda rows: pltpu.VMEM((rows, 2 * LANES), BF16)
    return pl.pallas_call(
        functools.partial(_fox_kernel, tq=tq, tk=tk, hd=hd, n_meta=n_meta),
        grid=(b, n_pair),
        in_specs=in_specs,
        out_specs=pl.BlockSpec((1, s_len, LANES), lambda bi, hp: (bi, 0, hp)),
        out_shape=jax.ShapeDtypeStruct((b, s_len, d_model), BF16),
        scratch_shapes=[ext(s_len), ext(s_len), ext(s_len), ext(tk), ext(tk),
                        pltpu.VMEM((2, 2, tq, tk), F32), pltpu.VMEM((2, tq, tk), BF16),
                        pltpu.VMEM((2, tq, tk), F32), pltpu.VMEM((2, tq, LANES), F32),
                        pltpu.VMEM((2, tq, LANES), F32)],
        compiler_params=_cparams(2),
        name="forgetting_attention",
    )(q, kv, kv, fparts, kvm, kvm, fpm, pq, pk, pv, cq, ck, cv)


def _router_kernel(x_ref, g_ref, wr_ref, xn_ref, id_ref, g1_ref, g2_ref, *, n_experts):
    xn = _rms(x_ref[...], g_ref[...])
    xn_ref[...] = xn.astype(BF16)
    a1, a2, a3 = _split3(xn)
    b1, b2, b3 = _split3(wr_ref[...])
    logits = (_dot(a1, b1) + _dot(a1, b2) + _dot(a2, b1)
              + _dot(a2, b2) + _dot(a1, b3) + _dot(a3, b1))
    lane = lax.broadcasted_iota(jnp.int32, logits.shape, 1)
    lg = jnp.where(lane < n_experts, logits, -jnp.inf)
    m1 = jnp.max(lg, axis=-1, keepdims=True)
    i1 = jnp.min(jnp.where(lg == m1, lane, LANES), axis=-1, keepdims=True)
    lg2 = jnp.where(lane == i1, -jnp.inf, lg)
    m2 = jnp.max(lg2, axis=-1, keepdims=True)
    i2 = jnp.min(jnp.where(lg2 == m2, lane, LANES), axis=-1, keepdims=True)
    e2 = jnp.exp(m2 - m1)
    den = 1.0 + e2
    id_ref[...] = jnp.where(lane == 0, i1, jnp.where(lane == 1, i2, 0))
    g1_ref[...] = jnp.broadcast_to(1.0 / den, logits.shape)
    g2_ref[...] = jnp.broadcast_to(e2 / den, logits.shape)


def moe_router(x, g, w_router):
    rows, d = x.shape
    n_experts = w_router.shape[1]
    wr = jnp.zeros((d, LANES), F32).at[:, :n_experts].set(w_router)
    tm = _row_tile(rows)
    return pl.pallas_call(
        functools.partial(_router_kernel, n_experts=n_experts),
        grid=(rows // tm,),
        in_specs=[
            pl.BlockSpec((tm, d), lambda i: (i, 0)),
            pl.BlockSpec((1, d), lambda i: (0, 0)),
            pl.BlockSpec((d, LANES), lambda i: (0, 0)),
        ],
        out_specs=[pl.BlockSpec((tm, d), lambda i: (i, 0)),
                   pl.BlockSpec((tm, LANES), lambda i: (i, 0)),
                   pl.BlockSpec((tm, LANES), lambda i: (i, 0)),
                   pl.BlockSpec((tm, LANES), lambda i: (i, 0))],
        out_shape=[jax.ShapeDtypeStruct((rows, d), BF16),
                   jax.ShapeDtypeStruct((rows, LANES), jnp.int32),
                   jax.ShapeDtypeStruct((rows, LANES), F32),
                   jax.ShapeDtypeStruct((rows, LANES), F32)],
        compiler_params=_cparams(1),
        name="moe_router",
    )(x, g, wr)


def _moe_ffn_kernel(te_ref, nt_ref, x_ref, wgu_ref, wd_ref, o_ref, *, ff, f_chunk):
    i = pl.program_id(0)

    @pl.when(i < nt_ref[0])
    def _():
        acc = _swiglu_acc(x_ref[...], wgu_ref, wd_ref, ff, f_chunk, widx=(0,))
        o_ref[...] = acc.astype(o_ref.dtype)

    @pl.when(i >= nt_ref[0])
    def _():
        o_ref[...] = jnp.zeros_like(o_ref)


def moe_grouped_ffn(xs, tile_expert, n_tiles_used, wgu, wd):
    p_rows, d = xs.shape
    n_experts, ff, _ = wd.shape
    tm = MOE_TM
    grid_spec = pltpu.PrefetchScalarGridSpec(
        num_scalar_prefetch=2,
        grid=(p_rows // tm,),
        in_specs=[
            pl.BlockSpec((tm, d), lambda i, te, nt: (i, 0)),
            pl.BlockSpec((1, d, 2 * ff), lambda i, te, nt: (te[i], 0, 0)),
            pl.BlockSpec((1, ff, d), lambda i, te, nt: (te[i], 0, 0)),
        ],
        out_specs=pl.BlockSpec((tm, d), lambda i, te, nt: (i, 0)),
    )
    return pl.pallas_call(
        functools.partial(_moe_ffn_kernel, ff=ff, f_chunk=FF_CHUNK),
        grid_spec=grid_spec,
        out_shape=jax.ShapeDtypeStruct((p_rows, d), BF16),
        compiler_params=_cparams(1),
        name="moe_grouped_ffn",
    )(tile_expert, n_tiles_used, xs, wgu, wd)


def _combine_kernel(h_ref, y1_ref, y2_ref, g1_ref, g2_ref, o_ref):
    d = h_ref.shape[1]
    g1, g2 = g1_ref[...], g2_ref[...]
    for c0 in range(0, d, LANES):
        cs = slice(c0, c0 + LANES)
        o_ref[:, cs] = (h_ref[:, cs] + g1 * y1_ref[:, cs].astype(F32)
                        + g2 * y2_ref[:, cs].astype(F32))


def moe_block(h, g, w_router, wgu, wd):
    rows, d = h.shape
    n_experts = w_router.shape[1]
    tm = MOE_TM
    xn, ids, g1, g2 = moe_router(h, g, w_router)

    eid = ids[:, :TOP_K]
    sel = (eid[:, :, None] == jnp.arange(n_experts)[None, None, :]).any(axis=1)
    rank = jnp.cumsum(sel.astype(jnp.int32), axis=0) - 1
    cnt = rank[-1] + 1
    cnt_pad = ((cnt + tm - 1) // tm) * tm
    ends = jnp.cumsum(cnt_pad)
    offs = ends - cnt_pad
    pos = offs[eid] + jnp.take_along_axis(rank, eid, axis=1)
    p_rows = rows * TOP_K + n_experts * tm
    n_tiles = p_rows // tm
    tile_start = jnp.arange(n_tiles, dtype=jnp.int32) * tm
    tile_expert = jnp.minimum((tile_start[:, None] >= ends[None, :]).sum(axis=1), n_experts - 1)
    n_used = (ends[-1] // tm).astype(jnp.int32).reshape(1)

    tok = jnp.arange(rows, dtype=jnp.int32)[:, None]
    tok_c = jnp.sort((eid * rows + tok).reshape(-1)) % rows
    tok_c = jnp.concatenate([tok_c, jnp.zeros((p_rows - rows * TOP_K,), jnp.int32)])
    cstart = jnp.cumsum(cnt) - cnt
    slot = jnp.arange(p_rows, dtype=jnp.int32)
    tok_sorted = jnp.zeros((p_rows,), jnp.int32)
    for e in range(n_experts):
        inside = (slot >= offs[e]) & (slot < offs[e] + cnt[e])
        tok_sorted = jnp.where(inside, jnp.roll(tok_c, offs[e] - cstart[e]), tok_sorted)
    xs = xn.at[tok_sorted].get(mode="promise_in_bounds")
    ys = moe_grouped_ffn(xs, tile_expert.astype(jnp.int32), n_used, wgu, wd)
    y1 = ys.at[pos[:, 0]].get(mode="promise_in_bounds", unique_indices=True)
    y2 = ys.at[pos[:, 1]].get(mode="promise_in_bounds", unique_indices=True)
    tr = _row_tile(rows)
    spec = pl.BlockSpec((tr, d), lambda i: (i, 0))
    gspec = pl.BlockSpec((tr, LANES), lambda i: (i, 0))
    return pl.pallas_call(
        _combine_kernel,
        grid=(rows // tr,),
        in_specs=[spec, spec, spec, gspec, gspec],
        out_specs=spec,
        out_shape=jax.ShapeDtypeStruct((rows, d), F32),
        compiler_params=_cparams(1),
        name="moe_combine",
    )(h, y1, y2, g1, g2)


def kernel(x, meta_tokens, norm_attn_a, w_qkv_a, w_o_a, norm_kv, w_kvf, b_f, k_norm, norm_attn_b,
           w_q_b, q_norm_b, w_o_b, norm_ffn_dense, w_gu_dense, w_down_dense, norm_ffn_moe,
           w_router, w_gu_moe, w_down_moe):
    b, s_len, d = x.shape
    n_heads = b_f.shape[0]
    hd = d // n_heads
    n_meta = meta_tokens.shape[0]
    depth = norm_attn_a.shape[0] + norm_attn_b.shape[0]
    n_a = norm_attn_a.shape[0]
    assert 2 * hd == LANES and d % LANES == 0
    row = lambda v: v.reshape(1, -1).astype(F32)

    h = x.reshape(b * s_len, d)
    hm = meta_tokens.astype(x.dtype)
    for layer in range(depth):
        last = layer == depth - 1
        if layer < n_a:
            w_qkv = w_qkv_a[layer].astype(BF16)
            g = row(norm_attn_a[layer])
            qkv = norm_matmul(h, g, w_qkv, BF16).reshape(b, s_len, 3 * d)
            qkv_m = norm_matmul(hm, g, w_qkv, BF16)
            o = stickbreak_attention(qkv, qkv_m, d, hd).reshape(b * s_len, d)
            o_m = stickbreak_attention(qkv_m[None], None, d, hd)[0]
            w_o = w_o_a[layer].astype(BF16)
            h = matmul_residual(o, w_o, h)
            hm = matmul_residual(o_m, w_o, hm)
        else:
            if layer == n_a:
                g = row(norm_kv)
                w_kv = w_kvf[:, :2 * d].astype(BF16)
                w_f = jnp.zeros((d, LANES), F32).at[:, :n_heads].set(w_kvf[:, 2 * d:]).astype(BF16)
                kgain = row(jnp.tile(k_norm, n_heads))
                kv = norm_matmul(h, g, w_kv, BF16, head_gain=kgain, head_dim=hd)
                kv_m = norm_matmul(hm, g, w_kv, BF16, head_gain=kgain, head_dim=hd)
                fl = norm_matmul(h, g, w_f, F32).reshape(b, s_len, LANES)
                fl_m = norm_matmul(hm, g, w_f, F32)
                bias = jnp.zeros((1, LANES), F32).at[0, :n_heads].set(b_f.astype(F32))
                f_cum, f_cum_m = forget_cumsum(fl, fl_m, bias)
                kv_sh = kv.reshape(b, s_len, 2 * d)
            i = layer - n_a
            qgain = row(jnp.tile(q_norm_b[i], n_heads))
            q = norm_matmul(h, row(norm_attn_b[i]), w_q_b[i].astype(BF16), BF16,
                            head_gain=qgain, head_dim=hd)
            o = forgetting_attention(q.reshape(b, s_len, d), kv_sh, kv_m, f_cum, f_cum_m, d, hd)
            h = matmul_residual(o.reshape(b * s_len, d), w_o_b[i].astype(BF16), h)
            if not last:
                raise NotImplementedError("meta-row queries in forgetting layers before the last")
        j = layer // 2
        if layer % 2 == 0:
            wgu = w_gu_dense[j].astype(BF16)
            wd = w_down_dense[j].astype(BF16)
            g = row(norm_ffn_dense[j])
            h = dense_ffn(h, g, wgu, wd)
            if not last:
                hm = dense_ffn(hm, g, wgu, wd)
        else:
            h = moe_block(h, row(norm_ffn_moe[j]), w_router[j],
                          w_gu_moe[j].astype(BF16), w_down_moe[j].astype(BF16))
            if not last:
                raise NotImplementedError("meta rows through a MoE layer before the last")
    return h.reshape(b, s_len, d)
```

```python
import functools

import jax
import jax.numpy as jnp
import numpy as np
from jax import lax
from jax.experimental import pallas as pl
from jax.experimental.pallas import tpu as pltpu

F32 = jnp.float32
BF16 = jnp.bfloat16

EPS = 1e-6
LOG2E = 1.4426950408889634
TOP_K = 2
LANES = 128
VMEM_LIMIT = 56 * 1024 * 1024
ROW_TILE = 512
COL_CHUNK = 512
FF_CHUNK = 256
ATT_TQ = 256
ATT_TK = 128
ATT_PAIRS = 4
FOX_PAIRS = 2
MOE_TM = 512
CUM_CHUNK = 256


def _cparams(n_axes):
    return pltpu.CompilerParams(dimension_semantics=("arbitrary",) * n_axes,
                                vmem_limit_bytes=VMEM_LIMIT)


def _split3(x):
    h1 = x.astype(BF16)
    r1 = x - h1.astype(F32)
    h2 = r1.astype(BF16)
    h3 = (r1 - h2.astype(F32)).astype(BF16)
    return h1, h2, h3


def _split2(x):
    h1 = x.astype(BF16)
    h2 = (x - h1.astype(F32)).astype(BF16)
    return h1, h2


def _dot(a, b):
    return jnp.dot(a, b, preferred_element_type=F32)


def _dot_nt(a, b):
    return lax.dot_general(a, b, (((1,), (1,)), ((), ())), preferred_element_type=F32)


def _rms(xf, g):
    ms = jnp.mean(xf * xf, axis=-1, keepdims=True)
    return xf * lax.rsqrt(ms + EPS) * g


def _row_tile(rows, pref=ROW_TILE):
    return pref if rows % pref == 0 else rows


def _col_chunk(*widths):
    c = COL_CHUNK
    while any(w % c for w in widths):
        c //= 2
    assert c >= LANES
    return c


def _norm_matmul_kernel(x_ref, g_ref, w_ref, hg_ref, gm_ref, o_ref, *, n_chunk, head_cols):
    xn = _rms(x_ref[...], g_ref[...]).astype(BF16)
    n = w_ref.shape[1]
    for c0 in range(0, n, n_chunk):
        acc = _dot(xn, w_ref[:, c0:c0 + n_chunk])
        if c0 < head_cols:
            s1, s2 = _split2(acc * acc)
            ms = _dot(s1, gm_ref[...]) + _dot(s2, gm_ref[...])
            acc = acc * lax.rsqrt(ms + EPS) * hg_ref[:, c0:c0 + n_chunk]
        o_ref[:, c0:c0 + n_chunk] = acc.astype(o_ref.dtype)


def norm_matmul(x, g, w, out_dtype, *, head_gain=None, head_dim=None):
    rows, d = x.shape
    n = w.shape[1]
    tm = _row_tile(rows)
    if head_gain is None:
        n_chunk = _col_chunk(n)
        head_cols = 0
        head_gain = jnp.zeros((1, n_chunk), F32)
        head_dim = n_chunk
    else:
        head_cols = head_gain.shape[1]
        n_chunk = _col_chunk(n, head_cols)
        assert n_chunk % head_dim == 0
    grp = jnp.arange(n_chunk) // head_dim
    gm = ((grp[:, None] == grp[None, :]).astype(F32) / head_dim).astype(BF16)
    kern = functools.partial(_norm_matmul_kernel, n_chunk=n_chunk, head_cols=head_cols)
    return pl.pallas_call(
        kern,
        grid=(rows // tm,),
        in_specs=[
            pl.BlockSpec((tm, d), lambda i: (i, 0)),
            pl.BlockSpec((1, d), lambda i: (0, 0)),
            pl.BlockSpec((d, n), lambda i: (0, 0)),
            pl.BlockSpec(head_gain.shape, lambda i: (0, 0)),
            pl.BlockSpec((n_chunk, n_chunk), lambda i: (0, 0)),
        ],
        out_specs=pl.BlockSpec((tm, n), lambda i: (i, 0)),
        out_shape=jax.ShapeDtypeStruct((rows, n), out_dtype),
        compiler_params=_cparams(1),
        name="norm_matmul",
    )(x, g, w, head_gain, gm)


def _matmul_res_kernel(a_ref, w_ref, r_ref, o_ref, *, n_chunk):
    a = a_ref[...]
    n = w_ref.shape[1]
    for c0 in range(0, n, n_chunk):
        o_ref[:, c0:c0 + n_chunk] = r_ref[:, c0:c0 + n_chunk] + _dot(a, w_ref[:, c0:c0 + n_chunk])


def matmul_residual(a, w, res):
    rows, k = a.shape
    n = w.shape[1]
    tm = _row_tile(rows)
    n_chunk = _col_chunk(n)
    return pl.pallas_call(
        functools.partial(_matmul_res_kernel, n_chunk=n_chunk),
        grid=(rows // tm,),
        in_specs=[
            pl.BlockSpec((tm, k), lambda i: (i, 0)),
            pl.BlockSpec((k, n), lambda i: (0, 0)),
            pl.BlockSpec((tm, n), lambda i: (i, 0)),
        ],
        out_specs=pl.BlockSpec((tm, n), lambda i: (i, 0)),
        out_shape=jax.ShapeDtypeStruct((rows, n), F32),
        compiler_params=_cparams(1),
        name="matmul_residual",
    )(a, w, res)


def _swiglu_acc(xn, wgu_ref, wd_ref, ff, f_chunk, widx=()):
    acc = None
    for c0 in range(0, ff, f_chunk):
        gt = _dot(xn, wgu_ref[widx + (slice(None), slice(c0, c0 + f_chunk))])
        up = _dot(xn, wgu_ref[widx + (slice(None), slice(ff + c0, ff + c0 + f_chunk))])
        act = (gt * (1.0 / (1.0 + jnp.exp(-gt))) * up).astype(BF16)
        part = _dot(act, wd_ref[widx + (slice(c0, c0 + f_chunk), slice(None))])
        acc = part if acc is None else acc + part
    return acc


def _dense_ffn_kernel(x_ref, g_ref, wgu_ref, wd_ref, o_ref, *, ff, f_chunk):
    xf = x_ref[...]
    xn = _rms(xf, g_ref[...]).astype(BF16)
    o_ref[...] = xf + _swiglu_acc(xn, wgu_ref, wd_ref, ff, f_chunk)


def dense_ffn(x, g, wgu, wd):
    rows, d = x.shape
    ff = wd.shape[0]
    tm = _row_tile(rows)
    return pl.pallas_call(
        functools.partial(_dense_ffn_kernel, ff=ff, f_chunk=FF_CHUNK),
        grid=(rows // tm,),
        in_specs=[
            pl.BlockSpec((tm, d), lambda i: (i, 0)),
            pl.BlockSpec((1, d), lambda i: (0, 0)),
            pl.BlockSpec((d, 2 * ff), lambda i: (0, 0)),
            pl.BlockSpec((ff, d), lambda i: (0, 0)),
        ],
        out_specs=pl.BlockSpec((tm, d), lambda i: (i, 0)),
        out_shape=jax.ShapeDtypeStruct((rows, d), F32),
        compiler_params=_cparams(1),
        name="dense_ffn",
    )(x, g, wgu, wd)


def _sb_suffix(z, mask, u_ref):
    zb = z.astype(BF16)
    one, zero = jnp.asarray(1.0, BF16), jnp.asarray(0.0, BF16)
    sp = jnp.maximum(zb, zero) + jnp.log(one + jnp.exp2(jnp.abs(zb) * jnp.asarray(-LOG2E, BF16)))
    if mask is not None:
        sp = jnp.where(mask, sp, zero)
    return _dot(sp, u_ref[...])


def _sb_kernel(*refs, tq, tk, hd, n_meta, scale):
    z_scr, cs_scr, run_scr, acc_scr = refs[-4:]
    if n_meta:
        q_ref, k_ref, v_ref, km_ref, vm_ref, u_ref, o_ref = refs[:-4]
    else:
        q_ref, k_ref, v_ref, u_ref, o_ref = refs[:-4]
    s_len = q_ref.shape[1]
    n_hp = q_ref.shape[2] // LANES
    psl = lambda p: slice(p * LANES, (p + 1) * LANES)
    n_q = s_len // tq
    n_diag = tq // tk
    has_meta = 1 if n_meta else 0
    lane = lax.broadcasted_iota(jnp.int32, (tq, LANES), 1)
    head0 = lane < hd
    row_in = lax.broadcasted_iota(jnp.int32, (tq, tk), 0)
    col_in = lax.broadcasted_iota(jnp.int32, (tq, tk), 1)
    META = "meta"

    def q_tile(i, n_real, static_tiles):
        r0 = pl.multiple_of(i * tq, tq)
        qh = []
        for p in range(n_hp):
            q = q_ref[0, pl.ds(r0, tq), psl(p)] * scale
            qh += [jnp.where(head0, q, jnp.zeros_like(q)), jnp.where(head0, jnp.zeros_like(q), q)]

        def offset(n):
            return pl.multiple_of(r0 + (n_diag - 1 - n) * tk, tk)

        def mask_of(n):
            if n is META:
                return col_in < n_meta
            if isinstance(n, int) and n < n_diag:
                return (col_in + (n_diag - 1 - n) * tk) < row_in
            return None

        def load(n, ref, mref):
            return mref[...] if n is META else ref[0, pl.ds(offset(n), tk), :]

        def step(par, t1, t2, t3):
            if t3 is not None:
                v, m = load(t3, v_ref, vm_ref if n_meta else None), mask_of(t3)
                for h in range(2 * n_hp):
                    g = run_scr[h] + cs_scr[h]
                    a = jnp.exp(z_scr[par, h] + g)
                    if m is not None:
                        a = jnp.where(m, a, 0.0)
                    run_scr[h] = jnp.broadcast_to(g[:, 0:1], g.shape)
                    acc_scr[h] += _dot(a.astype(BF16), v[:, psl(h // 2)])
            if t2 is not None:
                m = mask_of(t2)
                for h in range(2 * n_hp):
                    cs_scr[h] = _sb_suffix(z_scr[1 - par, h], m, u_ref)
            if t1 is not None:
                k = load(t1, k_ref, km_ref if n_meta else None)
                for h in range(2 * n_hp):
                    z_scr[par, h] = _dot_nt(qh[h], k[:, psl(h // 2)])

        run_scr[...] = jnp.zeros_like(run_scr)
        acc_scr[...] = jnp.zeros_like(acc_scr)

        if static_tiles:
            tiles = list(range(n_real)) + ([META] if has_meta else [])
            for s in range(len(tiles) + 2):
                pick = lambda j: tiles[j] if 0 <= j < len(tiles) else None
                step(s % 2, pick(s), pick(s - 1), pick(s - 2))
        else:
            n_pro = n_diag + 2
            assert n_pro % 2 == 0
            for s in range(n_pro):
                pick = lambda j: j if j >= 0 else None
                step(s % 2, pick(s), pick(s - 1), pick(s - 2))

            def body(j, carry):
                s = n_pro + 2 * j
                step(0, s, s - 1, s - 2)
                step(1, s + 1, s, s - 1)
                return carry

            lax.fori_loop(0, (n_real - n_pro) // 2, body, 0)
            last = n_real - 1
            step(0, META if has_meta else None, last, last - 1)
            step(1, None, META if has_meta else None, last)
            if has_meta:
                step(0, None, None, META)

        for p in range(n_hp):
            o_ref[0, pl.ds(r0, tq), psl(p)] = jnp.where(
                head0, acc_scr[2 * p], acc_scr[2 * p + 1]).astype(o_ref.dtype)

    q_tile(0, n_diag, True)
    if n_q > 1:
        assert n_diag >= 2, "the pipelined prologue needs two diagonal key tiles per query tile"

        def outer(i, carry):
            q_tile(i, n_diag * (i + 1), False)
            return carry

        lax.fori_loop(1, n_q, outer, 0)


def _suffix_matrix(t):
    r = jnp.arange(t)
    return -(r[:, None] >= r[None, :]).astype(BF16)


def stickbreak_attention(qkv, qkv_meta, d_model, hd):
    b, s_len, _ = qkv.shape
    n_pair = d_model // LANES
    tq = min(ATT_TQ, s_len)
    tk = min(ATT_TK, s_len)
    n_meta = 0 if qkv_meta is None else qkv_meta.shape[0]
    n_hp = ATT_PAIRS if n_pair % ATT_PAIRS == 0 else 1
    width = n_hp * LANES
    n_grp = n_pair // n_hp
    col = lambda which: (lambda bi, hp: (bi, 0, which * n_grp + hp))
    in_specs = [pl.BlockSpec((1, s_len, width), col(0)),
                pl.BlockSpec((1, s_len, width), col(1)),
                pl.BlockSpec((1, s_len, width), col(2))]
    args = [qkv, qkv, qkv]
    if n_meta:
        assert n_meta <= tk
        meta_pad = jnp.zeros((tk, qkv_meta.shape[1]), qkv_meta.dtype).at[:n_meta].set(qkv_meta)
        in_specs += [pl.BlockSpec((tk, width), lambda bi, hp: (0, n_grp + hp)),
                     pl.BlockSpec((tk, width), lambda bi, hp: (0, 2 * n_grp + hp))]
        args += [meta_pad, meta_pad]
    in_specs.append(pl.BlockSpec((tk, tk), lambda bi, hp: (0, 0)))
    args.append(_suffix_matrix(tk))
    kern = functools.partial(_sb_kernel, tq=tq, tk=tk, hd=hd, n_meta=n_meta, scale=hd ** -0.5)
    nh = 2 * n_hp
    return pl.pallas_call(
        kern,
        grid=(b, n_grp),
        in_specs=in_specs,
        out_specs=pl.BlockSpec((1, s_len, width), lambda bi, hp: (bi, 0, hp)),
        out_shape=jax.ShapeDtypeStruct((b, s_len, d_model), BF16),
        scratch_shapes=[pltpu.VMEM((2, nh, tq, tk), F32), pltpu.VMEM((nh, tq, tk), F32),
                        pltpu.VMEM((nh, tq, tk), F32), pltpu.VMEM((nh, tq, LANES), F32)],
        compiler_params=_cparams(2),
        name="stickbreak_attention",
    )(*args)


def _log_sigmoid(y):
    return jnp.minimum(y, 0.0) - jnp.log(1.0 + jnp.exp(-jnp.abs(y)))


def _tri_cumsum(tri, x):
    h1, h2, h3 = _split3(x)
    return _dot(tri, h1) + _dot(tri, h2) + _dot(tri, h3)


def _forget_cumsum_kernel(f_ref, fm_ref, b_ref, tri_ref, trim_ref, o_ref, om_ref, *, chunk):
    bias = b_ref[...]
    fm = _tri_cumsum(trim_ref[...], _log_sigmoid(fm_ref[...] + bias))
    for p, part in enumerate(_split3(fm)):
        om_ref[p] = part
    n_meta = fm.shape[0]
    carry = fm[n_meta - 1:n_meta, :]
    s_len = f_ref.shape[1]
    for c0 in range(0, s_len, chunk):
        fc = _tri_cumsum(tri_ref[...], _log_sigmoid(f_ref[0, c0:c0 + chunk, :] + bias)) + carry
        for p, part in enumerate(_split3(fc)):
            o_ref[p, 0, c0:c0 + chunk, :] = part
        carry = fc[chunk - 1:chunk, :]


def forget_cumsum(f_logit, f_logit_meta, bias):
    b, s_len, w = f_logit.shape
    n_meta = f_logit_meta.shape[0]
    chunk = min(CUM_CHUNK, s_len)
    tri = lambda t: (jnp.arange(t)[:, None] >= jnp.arange(t)[None, :]).astype(BF16)
    return pl.pallas_call(
        functools.partial(_forget_cumsum_kernel, chunk=chunk),
        grid=(b,),
        in_specs=[
            pl.BlockSpec((1, s_len, w), lambda bi: (bi, 0, 0)),
            pl.BlockSpec((n_meta, w), lambda bi: (0, 0)),
            pl.BlockSpec((1, w), lambda bi: (0, 0)),
            pl.BlockSpec((chunk, chunk), lambda bi: (0, 0)),
            pl.BlockSpec((n_meta, n_meta), lambda bi: (0, 0)),
        ],
        out_specs=[pl.BlockSpec((3, 1, s_len, w), lambda bi: (0, bi, 0, 0)),
                   pl.BlockSpec((3, n_meta, w), lambda bi: (0, 0, 0))],
        out_shape=[jax.ShapeDtypeStruct((3, b, s_len, w), BF16),
                   jax.ShapeDtypeStruct((3, n_meta, w), BF16)],
        compiler_params=_cparams(1),
        name="forget_cumsum",
    )(f_logit, f_logit_meta, bias, tri(chunk), tri(n_meta))


def _fox_kernel(qc_ref, kc_ref, vc_ref, f_ref, kmc_ref, vmc_ref, fm_ref, pq_ref, pk_ref, pv_ref,
                cq_ref, ck_ref, cv_ref, o_ref,
                q_ref, k_ref, v_ref, km_ref, vm_ref, z_scr, p_scr, m_scr, alpha_scr, acc_scr,
                *, tq, tk, hd, n_meta):
    s_len = qc_ref.shape[1]
    n_q = s_len // tq
    n_diag = tq // tk
    lane = lax.broadcasted_iota(jnp.int32, (tq, LANES), 1)
    head0 = lane < hd
    row_in = lax.broadcasted_iota(jnp.int32, (tq, tk), 0)
    col_in = lax.broadcasted_iota(jnp.int32, (tq, tk), 1)
    META = "meta"
    n_hp = qc_ref.shape[2] // LANES
    n_h = 2 * n_hp
    hsl = lambda h: slice(h * LANES, (h + 1) * LANES)
    psl = lambda p: slice(p * LANES, (p + 1) * LANES)
    xsl = lambda p: slice(2 * p * LANES, 2 * (p + 1) * LANES)

    def place(main, fparts, p_mat, c_ref):
        lhs = main if fparts is None else jnp.concatenate([main] + fparts, axis=1)
        return (_dot(lhs, p_mat) + c_ref[...]).astype(BF16)

    fpm = [fm_ref[a] for a in range(3)]
    for p in range(n_hp):
        for c0 in range(0, s_len, tq):
            rows = slice(c0, c0 + tq)
            fp = [f_ref[a, 0, rows, :] for a in range(3)]
            q_ref[rows, xsl(p)] = place(qc_ref[0, rows, psl(p)], fp, pq_ref[p], cq_ref)
            k_ref[rows, xsl(p)] = place(kc_ref[0, rows, psl(p)], fp, pk_ref[p], ck_ref)
            v_ref[rows, xsl(p)] = place(vc_ref[0, rows, psl(p)], None, pv_ref[0], cv_ref)
        km_ref[:, xsl(p)] = place(kmc_ref[:, psl(p)], fpm, pk_ref[p], ck_ref)
        vm_ref[:, xsl(p)] = place(vmc_ref[:, psl(p)], None, pv_ref[0], cv_ref)

    def q_tile(i, n_real, static_tiles):
        r0 = pl.multiple_of(i * tq, tq)
        qh = [q_ref[pl.ds(r0, tq), hsl(h)] for h in range(n_h)]

        def mask_of(t):
            if t is META:
                return col_in < n_meta
            return None if t[1] is None else (col_in + t[1] * tk) <= row_in

        def load(t, ref, mref, h):
            if t is META:
                return mref[:, hsl(h)]
            return ref[pl.ds(pl.multiple_of(t[0] * tk, tk), tk), hsl(h)]

        def step(par, t1, t2, t3):
            if t3 is not None:
                for h in range(n_h):
                    acc_scr[h] = acc_scr[h] * alpha_scr[h] + _dot(p_scr[h], load(t3, v_ref, vm_ref, h))
            if t2 is not None:
                m = mask_of(t2)
                for h in range(n_h):
                    z = z_scr[1 - par, h]
                    if m is not None:
                        z = jnp.where(m, z, -jnp.inf)
                    m_old = m_scr[h]
                    m_new = jnp.maximum(m_old, jnp.max(z, axis=-1, keepdims=True))
                    p_scr[h] = jnp.exp(z - m_new).astype(BF16)
                    alpha_scr[h] = jnp.exp(m_old - m_new)[:, :LANES]
                    m_scr[h] = m_new
            if t1 is not None:
                for h in range(n_h):
                    z_scr[par, h] = _dot_nt(qh[h], load(t1, k_ref, km_ref, h))

        m_scr[...] = jnp.full(m_scr.shape, -jnp.inf, F32)
        acc_scr[...] = jnp.zeros_like(acc_scr)

        if static_tiles:
            tiles = [META] + [(n, n - (n_real - n_diag) if n >= n_real - n_diag else None)
                              for n in range(n_real)]
            for s in range(len(tiles) + 2):
                pick = lambda j: tiles[j] if 0 <= j < len(tiles) else None
                step(s % 2, pick(s), pick(s - 1), pick(s - 2))
        else:
            pos = lambda p: META if (isinstance(p, int) and p == 0) else (p - 1, None)
            n_pro = 4
            for s in range(n_pro):
                pick = lambda j: pos(j) if j >= 0 else None
                step(s % 2, pick(s), pick(s - 1), pick(s - 2))

            def body(j, carry):
                s = n_pro + 2 * j
                step(0, pos(s), pos(s - 1), pos(s - 2))
                step(1, pos(s + 1), pos(s), pos(s - 1))
                return carry

            lax.fori_loop(0, (n_real - n_pro) // 2, body, 0)
            a, b = n_real - 2, n_real - 1
            step(0, (b, None), (a, 0), (a - 1, None))
            step(1, None, (b, 1), (a, None))
            step(0, None, None, (b, None))

        for p in range(n_hp):
            res0, res1 = acc_scr[2 * p], acc_scr[2 * p + 1]
            rot0 = pltpu.roll(res0, hd, 1)
            rot1 = pltpu.roll(res1, hd, 1)
            o_ref[0, pl.ds(r0, tq), psl(p)] = jnp.where(
                head0, res0 / rot0, rot1 / res1).astype(o_ref.dtype)

    q_tile(0, n_diag, True)
    if n_q > 1:
        assert n_diag == 2, "the pipelined sweep is written for two diagonal key tiles per query tile"

        def outer(i, carry):
            q_tile(i, n_diag * (i + 1), False)
            return carry

        lax.fori_loop(1, n_q, outer, 0)


def _fox_placement(n_pair, hd):
    pq = np.zeros((n_pair, 4 * LANES, 2 * LANES), np.float32)
    pk = np.zeros((n_pair, 4 * LANES, 2 * LANES), np.float32)
    pv = np.zeros((1, LANES, 2 * LANES), np.float32)
    cq = np.zeros((1, 2 * LANES), np.float32)
    ck = np.zeros((1, 2 * LANES), np.float32)
    cv = np.zeros((1, 2 * LANES), np.float32)
    for h in range(2):
        base = h * LANES
        for j in range(hd):
            pq[:, h * hd + j, base + j] = hd ** -0.5
            pk[:, h * hd + j, base + j] = 1.0
            pv[:, h * hd + j, base + j] = 1.0
        for p in range(3):
            for hp in range(n_pair):
                pq[hp, LANES + p * LANES + 2 * hp + h, base + hd + p] = 1.0
                pk[hp, LANES + p * LANES + 2 * hp + h, base + hd + 3 + p] = -1.0
        cq[0, base + hd + 3:base + hd + 6] = 1.0
        ck[0, base + hd:base + hd + 3] = 1.0
        cv[0, base + hd:base + LANES] = 1.0
    bf = lambda a: jnp.asarray(a, BF16)
    return bf(pq), bf(pk), bf(pv), jnp.asarray(cq), jnp.asarray(ck), jnp.asarray(cv)


def forgetting_attention(q, kv, kv_meta, fparts, fparts_meta, d_model, hd):
    b, s_len, _ = q.shape
    n_pair = d_model // LANES
    n_meta = kv_meta.shape[0]
    tq = min(ATT_TQ, s_len)
    tk = min(ATT_TK, s_len)
    assert n_meta <= tk and tk == LANES and 2 * n_pair <= LANES
    kvm = jnp.zeros((tk, kv_meta.shape[1]), kv_meta.dtype).at[:n_meta].set(kv_meta)
    fpm = jnp.zeros((3, tk, LANES), fparts_meta.dtype).at[:, :n_meta].set(fparts_meta)
    pq, pk, pv, cq, ck, cv = _fox_placement(n_pair, hd)
    n_hp = FOX_PAIRS if n_pair % FOX_PAIRS == 0 else 1
    width = n_hp * LANES
    n_grp = n_pair // n_hp
    nh = 2 * n_hp
    blk = lambda off: pl.BlockSpec((1, s_len, width), lambda bi, hp: (bi, 0, off + hp))
    mblk = lambda off: pl.BlockSpec((tk, width), lambda bi, hp: (0, off + hp))
    crow = pl.BlockSpec((1, 2 * LANES), lambda bi, hp: (0, 0))
    in_specs = [blk(0), blk(0), blk(n_grp),
                pl.BlockSpec((3, 1, s_len, LANES), lambda bi, hp: (0, bi, 0, 0)),
                mblk(0), mblk(n_grp),
                pl.BlockSpec((3, tk, LANES), lambda bi, hp: (0, 0, 0)),
                pl.BlockSpec((n_hp, 4 * LANES, 2 * LANES), lambda bi, hp: (hp, 0, 0)),
                pl.BlockSpec((n_hp, 4 * LANES, 2 * LANES), lambda bi, hp: (hp, 0, 0)),
                pl.BlockSpec((1, LANES, 2 * LANES), lambda bi, hp: (0, 0, 0)),
                crow, crow, crow]
    ext = lambda rows: pltpu.VMEM((rows, nh * LANES), BF16)
    return pl.pallas_call(
        functools.partial(_fox_kernel, tq=tq, tk=tk, hd=hd, n_meta=n_meta),
        grid=(b, n_grp),
        in_specs=in_specs,
        out_specs=pl.BlockSpec((1, s_len, width), lambda bi, hp: (bi, 0, hp)),
        out_shape=jax.ShapeDtypeStruct((b, s_len, d_model), BF16),
        scratch_shapes=[ext(s_len), ext(s_len), ext(s_len), ext(tk), ext(tk),
                        pltpu.VMEM((2, nh, tq, tk), F32), pltpu.VMEM((nh, tq, tk), BF16),
                        pltpu.VMEM((nh, tq, tk), F32), pltpu.VMEM((nh, tq, LANES), F32),
                        pltpu.VMEM((nh, tq, LANES), F32)],
        compiler_params=_cparams(2),
        name="forgetting_attention",
    )(q, kv, kv, fparts, kvm, kvm, fpm, pq, pk, pv, cq, ck, cv)


def _router_kernel(x_ref, g_ref, wr_ref, xn_ref, id_ref, g1_ref, g2_ref, *, n_experts):
    xn = _rms(x_ref[...], g_ref[...])
    xn_ref[...] = xn.astype(BF16)
    a1, a2, a3 = _split3(xn)
    b1, b2, b3 = _split3(wr_ref[...])
    logits = (_dot(a1, b1) + _dot(a1, b2) + _dot(a2, b1)
              + _dot(a2, b2) + _dot(a1, b3) + _dot(a3, b1))
    lane = lax.broadcasted_iota(jnp.int32, logits.shape, 1)
    lg = jnp.where(lane < n_experts, logits, -jnp.inf)
    m1 = jnp.max(lg, axis=-1, keepdims=True)
    i1 = jnp.min(jnp.where(lg == m1, lane, LANES), axis=-1, keepdims=True)
    lg2 = jnp.where(lane == i1, -jnp.inf, lg)
    m2 = jnp.max(lg2, axis=-1, keepdims=True)
    i2 = jnp.min(jnp.where(lg2 == m2, lane, LANES), axis=-1, keepdims=True)
    e2 = jnp.exp(m2 - m1)
    den = 1.0 + e2
    id_ref[...] = jnp.where(lane == 0, i1, jnp.where(lane == 1, i2, 0))
    g1_ref[...] = jnp.broadcast_to(1.0 / den, logits.shape)
    g2_ref[...] = jnp.broadcast_to(e2 / den, logits.shape)


def moe_router(x, g, w_router):
    rows, d = x.shape
    n_experts = w_router.shape[1]
    wr = jnp.zeros((d, LANES), F32).at[:, :n_experts].set(w_router)
    tm = _row_tile(rows)
    return pl.pallas_call(
        functools.partial(_router_kernel, n_experts=n_experts),
        grid=(rows // tm,),
        in_specs=[
            pl.BlockSpec((tm, d), lambda i: (i, 0)),
            pl.BlockSpec((1, d), lambda i: (0, 0)),
            pl.BlockSpec((d, LANES), lambda i: (0, 0)),
        ],
        out_specs=[pl.BlockSpec((tm, d), lambda i: (i, 0)),
                   pl.BlockSpec((tm, LANES), lambda i: (i, 0)),
                   pl.BlockSpec((tm, LANES), lambda i: (i, 0)),
                   pl.BlockSpec((tm, LANES), lambda i: (i, 0))],
        out_shape=[jax.ShapeDtypeStruct((rows, d), BF16),
                   jax.ShapeDtypeStruct((rows, LANES), jnp.int32),
                   jax.ShapeDtypeStruct((rows, LANES), F32),
                   jax.ShapeDtypeStruct((rows, LANES), F32)],
        compiler_params=_cparams(1),
        name="moe_router",
    )(x, g, wr)


def _moe_ffn_kernel(te_ref, nt_ref, x_ref, wgu_ref, wd_ref, o_ref, *, ff, f_chunk):
    i = pl.program_id(0)

    @pl.when(i < nt_ref[0])
    def _():
        acc = _swiglu_acc(x_ref[...], wgu_ref, wd_ref, ff, f_chunk, widx=(0,))
        o_ref[...] = acc.astype(o_ref.dtype)

    @pl.when(i >= nt_ref[0])
    def _():
        o_ref[...] = jnp.zeros_like(o_ref)


def moe_grouped_ffn(xs, tile_expert, n_tiles_used, wgu, wd):
    p_rows, d = xs.shape
    n_experts, ff, _ = wd.shape
    tm = MOE_TM
    grid_spec = pltpu.PrefetchScalarGridSpec(
        num_scalar_prefetch=2,
        grid=(p_rows // tm,),
        in_specs=[
            pl.BlockSpec((tm, d), lambda i, te, nt: (i, 0)),
            pl.BlockSpec((1, d, 2 * ff), lambda i, te, nt: (te[i], 0, 0)),
            pl.BlockSpec((1, ff, d), lambda i, te, nt: (te[i], 0, 0)),
        ],
        out_specs=pl.BlockSpec((tm, d), lambda i, te, nt: (i, 0)),
    )
    return pl.pallas_call(
        functools.partial(_moe_ffn_kernel, ff=ff, f_chunk=FF_CHUNK),
        grid_spec=grid_spec,
        out_shape=jax.ShapeDtypeStruct((p_rows, d), BF16),
        compiler_params=_cparams(1),
        name="moe_grouped_ffn",
    )(tile_expert, n_tiles_used, xs, wgu, wd)


def _combine_kernel(h_ref, y1_ref, y2_ref, g1_ref, g2_ref, o_ref):
    d = h_ref.shape[1]
    g1, g2 = g1_ref[...], g2_ref[...]
    for c0 in range(0, d, LANES):
        cs = slice(c0, c0 + LANES)
        o_ref[:, cs] = (h_ref[:, cs] + g1 * y1_ref[:, cs].astype(F32)
                        + g2 * y2_ref[:, cs].astype(F32))


def moe_block(h, g, w_router, wgu, wd):
    rows, d = h.shape
    n_experts = w_router.shape[1]
    tm = MOE_TM
    xn, ids, g1, g2 = moe_router(h, g, w_router)

    eid = ids[:, :TOP_K]
    sel = (eid[:, :, None] == jnp.arange(n_experts)[None, None, :]).any(axis=1)
    rank = jnp.cumsum(sel.astype(jnp.int32), axis=0) - 1
    cnt = rank[-1] + 1
    cnt_pad = ((cnt + tm - 1) // tm) * tm
    ends = jnp.cumsum(cnt_pad)
    offs = ends - cnt_pad
    pos = offs[eid] + jnp.take_along_axis(rank, eid, axis=1)
    p_rows = rows * TOP_K + n_experts * tm
    n_tiles = p_rows // tm
    tile_start = jnp.arange(n_tiles, dtype=jnp.int32) * tm
    tile_expert = jnp.minimum((tile_start[:, None] >= ends[None, :]).sum(axis=1), n_experts - 1)
    n_used = (ends[-1] // tm).astype(jnp.int32).reshape(1)

    tok = jnp.arange(rows, dtype=jnp.int32)[:, None]
    tok_c = jnp.sort((eid * rows + tok).reshape(-1)) % rows
    tok_c = jnp.concatenate([tok_c, jnp.zeros((p_rows - rows * TOP_K,), jnp.int32)])
    cstart = jnp.cumsum(cnt) - cnt
    slot = jnp.arange(p_rows, dtype=jnp.int32)
    tok_sorted = jnp.zeros((p_rows,), jnp.int32)
    for e in range(n_experts):
        inside = (slot >= offs[e]) & (slot < offs[e] + cnt[e])
        tok_sorted = jnp.where(inside, jnp.roll(tok_c, offs[e] - cstart[e]), tok_sorted)
    xs = xn.at[tok_sorted].get(mode="promise_in_bounds")
    ys = moe_grouped_ffn(xs, tile_expert.astype(jnp.int32), n_used, wgu, wd)
    y1 = ys.at[pos[:, 0]].get(mode="promise_in_bounds", unique_indices=True)
    y2 = ys.at[pos[:, 1]].get(mode="promise_in_bounds", unique_indices=True)
    tr = _row_tile(rows)
    spec = pl.BlockSpec((tr, d), lambda i: (i, 0))
    gspec = pl.BlockSpec((tr, LANES), lambda i: (i, 0))
    return pl.pallas_call(
        _combine_kernel,
        grid=(rows // tr,),
        in_specs=[spec, spec, spec, gspec, gspec],
        out_specs=spec,
        out_shape=jax.ShapeDtypeStruct((rows, d), F32),
        compiler_params=_cparams(1),
        name="moe_combine",
    )(h, y1, y2, g1, g2)


def kernel(x, meta_tokens, norm_attn_a, w_qkv_a, w_o_a, norm_kv, w_kvf, b_f, k_norm, norm_attn_b,
           w_q_b, q_norm_b, w_o_b, norm_ffn_dense, w_gu_dense, w_down_dense, norm_ffn_moe,
           w_router, w_gu_moe, w_down_moe):
    b, s_len, d = x.shape
    n_heads = b_f.shape[0]
    hd = d // n_heads
    n_meta = meta_tokens.shape[0]
    depth = norm_attn_a.shape[0] + norm_attn_b.shape[0]
    n_a = norm_attn_a.shape[0]
    assert 2 * hd == LANES and d % LANES == 0
    row = lambda v: v.reshape(1, -1).astype(F32)

    h = x.reshape(b * s_len, d)
    hm = meta_tokens.astype(x.dtype)
    for layer in range(depth):
        last = layer == depth - 1
        if layer < n_a:
            w_qkv = w_qkv_a[layer].astype(BF16)
            g = row(norm_attn_a[layer])
            qkv = norm_matmul(h, g, w_qkv, BF16).reshape(b, s_len, 3 * d)
            qkv_m = norm_matmul(hm, g, w_qkv, BF16)
            o = stickbreak_attention(qkv, qkv_m, d, hd).reshape(b * s_len, d)
            o_m = stickbreak_attention(qkv_m[None], None, d, hd)[0]
            w_o = w_o_a[layer].astype(BF16)
            h = matmul_residual(o, w_o, h)
            hm = matmul_residual(o_m, w_o, hm)
        else:
            if layer == n_a:
                g = row(norm_kv)
                w_kv = w_kvf[:, :2 * d].astype(BF16)
                w_f = jnp.zeros((d, LANES), F32).at[:, :n_heads].set(w_kvf[:, 2 * d:]).astype(BF16)
                kgain = row(jnp.tile(k_norm, n_heads))
                kv = norm_matmul(h, g, w_kv, BF16, head_gain=kgain, head_dim=hd)
                kv_m = norm_matmul(hm, g, w_kv, BF16, head_gain=kgain, head_dim=hd)
                fl = norm_matmul(h, g, w_f, F32).reshape(b, s_len, LANES)
                fl_m = norm_matmul(hm, g, w_f, F32)
                bias = jnp.zeros((1, LANES), F32).at[0, :n_heads].set(b_f.astype(F32))
                f_cum, f_cum_m = forget_cumsum(fl, fl_m, bias)
                kv_sh = kv.reshape(b, s_len, 2 * d)
            i = layer - n_a
            qgain = row(jnp.tile(q_norm_b[i], n_heads))
            q = norm_matmul(h, row(norm_attn_b[i]), w_q_b[i].astype(BF16), BF16,
                            head_gain=qgain, head_dim=hd)
            o = forgetting_attention(q.reshape(b, s_len, d), kv_sh, kv_m, f_cum, f_cum_m, d, hd)
            h = matmul_residual(o.reshape(b * s_len, d), w_o_b[i].astype(BF16), h)
            if not last:
                raise NotImplementedError("meta-row queries in forgetting layers before the last")
        j = layer // 2
        if layer % 2 == 0:
            wgu = w_gu_dense[j].astype(BF16)
            wd = w_down_dense[j].astype(BF16)
            g = row(norm_ffn_dense[j])
            h = dense_ffn(h, g, wgu, wd)
            if not last:
                hm = dense_ffn(hm, g, wgu, wd)
        else:
            h = moe_block(h, row(norm_ffn_moe[j]), w_router[j],
                          w_gu_moe[j].astype(BF16), w_down_moe[j].astype(BF16))
            if not last:
                raise NotImplementedError("meta rows through a MoE layer before the last")
    return h.reshape(b, s_len, d)
```

```python
import functools

import jax
import jax.numpy as jnp
import numpy as np
from jax import lax
from jax.experimental import pallas as pl
from jax.experimental.pallas import tpu as pltpu

F32 = jnp.float32
BF16 = jnp.bfloat16

EPS = 1e-6
LOG2E = 1.4426950408889634
TOP_K = 2
LANES = 128
VMEM_LIMIT = 56 * 1024 * 1024
ROW_TILE = 512
COL_CHUNK = 512
FF_CHUNK = 256
ATT_TQ = 256
ATT_TK = 128
ATT_PAIRS = 4
FOX_PAIRS = 2
MOE_TM = 512
CUM_CHUNK = 256
CAST_BLOCK_BYTES = 6 * 1024 * 1024
CAST_STREAMS = 4


def _cparams(n_axes):
    return pltpu.CompilerParams(dimension_semantics=("arbitrary",) * n_axes,
                                vmem_limit_bytes=VMEM_LIMIT)


def _split3(x):
    h1 = x.astype(BF16)
    r1 = x - h1.astype(F32)
    h2 = r1.astype(BF16)
    h3 = (r1 - h2.astype(F32)).astype(BF16)
    return h1, h2, h3


def _split2(x):
    h1 = x.astype(BF16)
    h2 = (x - h1.astype(F32)).astype(BF16)
    return h1, h2


def _dot(a, b):
    return jnp.dot(a, b, preferred_element_type=F32)


def _dot_nt(a, b):
    return lax.dot_general(a, b, (((1,), (1,)), ((), ())), preferred_element_type=F32)


def _rms(xf, g):
    ms = jnp.mean(xf * xf, axis=-1, keepdims=True)
    return xf * lax.rsqrt(ms + EPS) * g


def _row_tile(rows, pref=ROW_TILE):
    return pref if rows % pref == 0 else rows


def _col_chunk(*widths):
    c = COL_CHUNK
    while any(w % c for w in widths):
        c //= 2
    assert c >= LANES
    return c


def _cast_kernel(*refs):
    o_ref = refs[-1]
    sub = refs[0].shape[0]
    for k, x_ref in enumerate(refs[:-1]):
        o_ref[k * sub:(k + 1) * sub, :] = x_ref[...].astype(o_ref.dtype)


def to_bf16(w):
    shape = w.shape
    cols = shape[-1]
    w2 = w.reshape(-1, cols)
    rows = w2.shape[0]
    rb = rows
    for cand in (2048, 1024, 512, 256, 128, 64):
        if rows % cand == 0 and cand * cols * 4 <= CAST_BLOCK_BYTES:
            rb = cand
            break
    n_str = CAST_STREAMS if rb % (16 * CAST_STREAMS) == 0 else 1
    sub = rb // n_str
    in_specs = [pl.BlockSpec((sub, cols), functools.partial(lambda i, k: (i * n_str + k, 0), k=k))
                for k in range(n_str)]
    out = pl.pallas_call(
        _cast_kernel,
        grid=(rows // rb,),
        in_specs=in_specs,
        out_specs=pl.BlockSpec((rb, cols), lambda i: (i, 0)),
        out_shape=jax.ShapeDtypeStruct((rows, cols), BF16),
        compiler_params=_cparams(1),
        name="to_bf16",
    )(*([w2] * n_str))
    return out.reshape(shape)


def _norm_matmul_kernel(*refs, cfg):
    n_p = len(cfg)
    x_ref, ins, outs = refs[0], refs[1:1 + 4 * n_p], refs[1 + 4 * n_p:]
    xf = x_ref[...]
    xs = xf * lax.rsqrt(jnp.mean(xf * xf, axis=-1, keepdims=True) + EPS)
    for pi, (n_chunk, head_cols) in enumerate(cfg):
        g_ref, w_ref, hg_ref, gm_ref = ins[4 * pi:4 * pi + 4]
        o_ref = outs[pi]
        xn = (xs * g_ref[...]).astype(BF16)
        n = w_ref.shape[1]
        for c0 in range(0, n, n_chunk):
            acc = _dot(xn, w_ref[:, c0:c0 + n_chunk])
            if c0 < head_cols:
                s1, s2 = _split2(acc * acc)
                ms = _dot(s1, gm_ref[...]) + _dot(s2, gm_ref[...])
                acc = acc * lax.rsqrt(ms + EPS) * hg_ref[:, c0:c0 + n_chunk]
            o_ref[:, c0:c0 + n_chunk] = acc.astype(o_ref.dtype)


def norm_projections(x, projs):
    rows, d = x.shape
    tm = _row_tile(rows)
    const = lambda shape: pl.BlockSpec(shape, lambda i: (0,) * len(shape))
    cfg, args, in_specs, out_specs, out_shape = [], [x], [pl.BlockSpec((tm, d), lambda i: (i, 0))], [], []
    for g, w, out_dtype, head_gain, head_dim in projs:
        n = w.shape[1]
        if head_gain is None:
            n_chunk = _col_chunk(n)
            head_cols = 0
            head_gain = jnp.zeros((1, n_chunk), F32)
            head_dim = n_chunk
        else:
            head_cols = head_gain.shape[1]
            n_chunk = _col_chunk(n, head_cols)
            assert n_chunk % head_dim == 0
        grp = jnp.arange(n_chunk) // head_dim
        gm = ((grp[:, None] == grp[None, :]).astype(F32) / head_dim).astype(BF16)
        cfg.append((n_chunk, head_cols))
        args += [g, w, head_gain, gm]
        in_specs += [const((1, d)), const((d, n)), const(head_gain.shape), const((n_chunk, n_chunk))]
        out_specs.append(pl.BlockSpec((tm, n), lambda i: (i, 0)))
        out_shape.append(jax.ShapeDtypeStruct((rows, n), out_dtype))
    return pl.pallas_call(
        functools.partial(_norm_matmul_kernel, cfg=tuple(cfg)),
        grid=(rows // tm,),
        in_specs=in_specs,
        out_specs=out_specs,
        out_shape=out_shape,
        compiler_params=_cparams(1),
        name="norm_matmul",
    )(*args)


def norm_matmul(x, g, w, out_dtype, *, head_gain=None, head_dim=None):
    return norm_projections(x, [(g, w, out_dtype, head_gain, head_dim)])[0]


def _matmul_res_kernel(a_ref, w_ref, r_ref, o_ref, *, n_chunk):
    a = a_ref[...]
    n = w_ref.shape[1]
    for c0 in range(0, n, n_chunk):
        o_ref[:, c0:c0 + n_chunk] = r_ref[:, c0:c0 + n_chunk] + _dot(a, w_ref[:, c0:c0 + n_chunk])


def matmul_residual(a, w, res):
    rows, k = a.shape
    n = w.shape[1]
    tm = _row_tile(rows)
    n_chunk = _col_chunk(n)
    return pl.pallas_call(
        functools.partial(_matmul_res_kernel, n_chunk=n_chunk),
        grid=(rows // tm,),
        in_specs=[
            pl.BlockSpec((tm, k), lambda i: (i, 0)),
            pl.BlockSpec((k, n), lambda i: (0, 0)),
            pl.BlockSpec((tm, n), lambda i: (i, 0)),
        ],
        out_specs=pl.BlockSpec((tm, n), lambda i: (i, 0)),
        out_shape=jax.ShapeDtypeStruct((rows, n), F32),
        compiler_params=_cparams(1),
        name="matmul_residual",
    )(a, w, res)


def _swiglu_acc(xn, wgu_ref, wd_ref, ff, f_chunk, widx=()):
    acc = None
    for c0 in range(0, ff, f_chunk):
        gt = _dot(xn, wgu_ref[widx + (slice(None), slice(c0, c0 + f_chunk))])
        up = _dot(xn, wgu_ref[widx + (slice(None), slice(ff + c0, ff + c0 + f_chunk))])
        act = (gt * (1.0 / (1.0 + jnp.exp(-gt))) * up).astype(BF16)
        part = _dot(act, wd_ref[widx + (slice(c0, c0 + f_chunk), slice(None))])
        acc = part if acc is None else acc + part
    return acc


def _dense_ffn_kernel(x_ref, g_ref, wgu_ref, wd_ref, o_ref, *, ff, f_chunk):
    xf = x_ref[...]
    xn = _rms(xf, g_ref[...]).astype(BF16)
    o_ref[...] = xf + _swiglu_acc(xn, wgu_ref, wd_ref, ff, f_chunk)


def dense_ffn(x, g, wgu, wd):
    rows, d = x.shape
    ff = wd.shape[0]
    tm = _row_tile(rows)
    return pl.pallas_call(
        functools.partial(_dense_ffn_kernel, ff=ff, f_chunk=FF_CHUNK),
        grid=(rows // tm,),
        in_specs=[
            pl.BlockSpec((tm, d), lambda i: (i, 0)),
            pl.BlockSpec((1, d), lambda i: (0, 0)),
            pl.BlockSpec((d, 2 * ff), lambda i: (0, 0)),
            pl.BlockSpec((ff, d), lambda i: (0, 0)),
        ],
        out_specs=pl.BlockSpec((tm, d), lambda i: (i, 0)),
        out_shape=jax.ShapeDtypeStruct((rows, d), F32),
        compiler_params=_cparams(1),
        name="dense_ffn",
    )(x, g, wgu, wd)


def _sb_suffix(z, mask, u_ref):
    zb = z.astype(BF16)
    one, zero = jnp.asarray(1.0, BF16), jnp.asarray(0.0, BF16)
    sp = jnp.maximum(zb, zero) + jnp.log(one + jnp.exp2(jnp.abs(zb) * jnp.asarray(-LOG2E, BF16)))
    if mask is not None:
        sp = jnp.where(mask, sp, zero)
    return _dot(sp, u_ref[...])


def _sb_kernel(*refs, tq, tk, hd, n_meta, scale):
    z_scr, cs_scr, run_scr, acc_scr = refs[-4:]
    if n_meta:
        q_ref, k_ref, v_ref, km_ref, vm_ref, u_ref, o_ref = refs[:-4]
    else:
        q_ref, k_ref, v_ref, u_ref, o_ref = refs[:-4]
    s_len = q_ref.shape[1]
    n_hp = q_ref.shape[2] // LANES
    psl = lambda p: slice(p * LANES, (p + 1) * LANES)
    n_q = s_len // tq
    n_diag = tq // tk
    has_meta = 1 if n_meta else 0
    lane = lax.broadcasted_iota(jnp.int32, (tq, LANES), 1)
    head0 = lane < hd
    row_in = lax.broadcasted_iota(jnp.int32, (tq, tk), 0)
    col_in = lax.broadcasted_iota(jnp.int32, (tq, tk), 1)
    META = "meta"

    def q_tile(i, n_real, static_tiles):
        r0 = pl.multiple_of(i * tq, tq)
        qh = []
        for p in range(n_hp):
            q = q_ref[0, pl.ds(r0, tq), psl(p)] * scale
            qh += [jnp.where(head0, q, jnp.zeros_like(q)), jnp.where(head0, jnp.zeros_like(q), q)]

        def offset(n):
            return pl.multiple_of(r0 + (n_diag - 1 - n) * tk, tk)

        def mask_of(n):
            if n is META:
                return col_in < n_meta
            if isinstance(n, int) and n < n_diag:
                return (col_in + (n_diag - 1 - n) * tk) < row_in
            return None

        def load(n, ref, mref):
            return mref[...] if n is META else ref[0, pl.ds(offset(n), tk), :]

        def step(par, t1, t2, t3):
            if t3 is not None:
                v, m = load(t3, v_ref, vm_ref if n_meta else None), mask_of(t3)
                for h in range(2 * n_hp):
                    g = run_scr[h] + cs_scr[h]
                    a = jnp.exp(z_scr[par, h] + g)
                    if m is not None:
                        a = jnp.where(m, a, 0.0)
                    run_scr[h] = jnp.broadcast_to(g[:, 0:1], g.shape)
                    acc_scr[h] += _dot(a.astype(BF16), v[:, psl(h // 2)])
            if t2 is not None:
                m = mask_of(t2)
                for h in range(2 * n_hp):
                    cs_scr[h] = _sb_suffix(z_scr[1 - par, h], m, u_ref)
            if t1 is not None:
                k = load(t1, k_ref, km_ref if n_meta else None)
                for h in range(2 * n_hp):
                    z_scr[par, h] = _dot_nt(qh[h], k[:, psl(h // 2)])

        run_scr[...] = jnp.zeros_like(run_scr)
        acc_scr[...] = jnp.zeros_like(acc_scr)

        if static_tiles:
            tiles = list(range(n_real)) + ([META] if has_meta else [])
            for s in range(len(tiles) + 2):
                pick = lambda j: tiles[j] if 0 <= j < len(tiles) else None
                step(s % 2, pick(s), pick(s - 1), pick(s - 2))
        else:
            n_pro = n_diag + 2
            assert n_pro % 2 == 0
            for s in range(n_pro):
                pick = lambda j: j if j >= 0 else None
                step(s % 2, pick(s), pick(s - 1), pick(s - 2))

            def body(j, carry):
                s = n_pro + 2 * j
                step(0, s, s - 1, s - 2)
                step(1, s + 1, s, s - 1)
                return carry

            lax.fori_loop(0, (n_real - n_pro) // 2, body, 0)
            last = n_real - 1
            step(0, META if has_meta else None, last, last - 1)
            step(1, None, META if has_meta else None, last)
            if has_meta:
                step(0, None, None, META)

        for p in range(n_hp):
            o_ref[0, pl.ds(r0, tq), psl(p)] = jnp.where(
                head0, acc_scr[2 * p], acc_scr[2 * p + 1]).astype(o_ref.dtype)

    q_tile(0, n_diag, True)
    if n_q > 1:
        assert n_diag >= 2, "the pipelined prologue needs two diagonal key tiles per query tile"

        def outer(i, carry):
            q_tile(i, n_diag * (i + 1), False)
            return carry

        lax.fori_loop(1, n_q, outer, 0)


def _suffix_matrix(t):
    r = jnp.arange(t)
    return -(r[:, None] >= r[None, :]).astype(BF16)


def stickbreak_attention(qkv, qkv_meta, d_model, hd):
    b, s_len, _ = qkv.shape
    n_pair = d_model // LANES
    tq = min(ATT_TQ, s_len)
    tk = min(ATT_TK, s_len)
    n_meta = 0 if qkv_meta is None else qkv_meta.shape[0]
    n_hp = ATT_PAIRS if n_pair % ATT_PAIRS == 0 else 1
    width = n_hp * LANES
    n_grp = n_pair // n_hp
    col = lambda which: (lambda bi, hp: (bi, 0, which * n_grp + hp))
    in_specs = [pl.BlockSpec((1, s_len, width), col(0)),
                pl.BlockSpec((1, s_len, width), col(1)),
                pl.BlockSpec((1, s_len, width), col(2))]
    args = [qkv, qkv, qkv]
    if n_meta:
        assert n_meta <= tk
        meta_pad = jnp.zeros((tk, qkv_meta.shape[1]), qkv_meta.dtype).at[:n_meta].set(qkv_meta)
        in_specs += [pl.BlockSpec((tk, width), lambda bi, hp: (0, n_grp + hp)),
                     pl.BlockSpec((tk, width), lambda bi, hp: (0, 2 * n_grp + hp))]
        args += [meta_pad, meta_pad]
    in_specs.append(pl.BlockSpec((tk, tk), lambda bi, hp: (0, 0)))
    args.append(_suffix_matrix(tk))
    kern = functools.partial(_sb_kernel, tq=tq, tk=tk, hd=hd, n_meta=n_meta, scale=hd ** -0.5)
    nh = 2 * n_hp
    return pl.pallas_call(
        kern,
        grid=(b, n_grp),
        in_specs=in_specs,
        out_specs=pl.BlockSpec((1, s_len, width), lambda bi, hp: (bi, 0, hp)),
        out_shape=jax.ShapeDtypeStruct((b, s_len, d_model), BF16),
        scratch_shapes=[pltpu.VMEM((2, nh, tq, tk), F32), pltpu.VMEM((nh, tq, tk), F32),
                        pltpu.VMEM((nh, tq, tk), F32), pltpu.VMEM((nh, tq, LANES), F32)],
        compiler_params=_cparams(2),
        name="stickbreak_attention",
    )(*args)


def _log_sigmoid(y):
    return jnp.minimum(y, 0.0) - jnp.log(1.0 + jnp.exp(-jnp.abs(y)))


def _tri_cumsum(tri, x):
    h1, h2, h3 = _split3(x)
    return _dot(tri, h1) + _dot(tri, h2) + _dot(tri, h3)


def _forget_cumsum_kernel(f_ref, fm_ref, b_ref, tri_ref, trim_ref, o_ref, om_ref, *, chunk, n_heads):
    bias = b_ref[...]

    def packed(f):
        lane = lax.broadcasted_iota(jnp.int32, f.shape, 1)
        out = None
        for a, part in enumerate(_split3(f)):
            pa = jnp.where(lane < n_heads, part.astype(F32), 0.0)
            pa = pa if a == 0 else pltpu.roll(pa, a * n_heads, 1)
            out = pa if out is None else out + pa
        return out.astype(BF16)

    fm = _tri_cumsum(trim_ref[...], _log_sigmoid(fm_ref[...] + bias))
    om_ref[...] = packed(fm)
    n_meta = fm.shape[0]
    carry = fm[n_meta - 1:n_meta, :]
    s_len = f_ref.shape[1]
    for c0 in range(0, s_len, chunk):
        fc = _tri_cumsum(tri_ref[...], _log_sigmoid(f_ref[0, c0:c0 + chunk, :] + bias)) + carry
        o_ref[0, c0:c0 + chunk, :] = packed(fc)
        carry = fc[chunk - 1:chunk, :]


def forget_cumsum(f_logit, f_logit_meta, bias, n_heads):
    b, s_len, w = f_logit.shape
    n_meta = f_logit_meta.shape[0]
    assert 3 * n_heads <= w
    chunk = min(CUM_CHUNK, s_len)
    tri = lambda t: (jnp.arange(t)[:, None] >= jnp.arange(t)[None, :]).astype(BF16)
    return pl.pallas_call(
        functools.partial(_forget_cumsum_kernel, chunk=chunk, n_heads=n_heads),
        grid=(b,),
        in_specs=[
            pl.BlockSpec((1, s_len, w), lambda bi: (bi, 0, 0)),
            pl.BlockSpec((n_meta, w), lambda bi: (0, 0)),
            pl.BlockSpec((1, w), lambda bi: (0, 0)),
            pl.BlockSpec((chunk, chunk), lambda bi: (0, 0)),
            pl.BlockSpec((n_meta, n_meta), lambda bi: (0, 0)),
        ],
        out_specs=[pl.BlockSpec((1, s_len, w), lambda bi: (bi, 0, 0)),
                   pl.BlockSpec((n_meta, w), lambda bi: (0, 0))],
        out_shape=[jax.ShapeDtypeStruct((b, s_len, w), BF16),
                   jax.ShapeDtypeStruct((n_meta, w), BF16)],
        compiler_params=_cparams(1),
        name="forget_cumsum",
    )(f_logit, f_logit_meta, bias, tri(chunk), tri(n_meta))


def _fox_kernel(qc_ref, kc_ref, vc_ref, f_ref, kmc_ref, vmc_ref, fm_ref, pq_ref, pk_ref, pv_ref,
                cq_ref, ck_ref, cv_ref, o_ref,
                q_ref, k_ref, v_ref, km_ref, vm_ref, z_scr, p_scr, m_scr, alpha_scr, acc_scr,
                *, tq, tk, hd, n_meta):
    s_len = qc_ref.shape[1]
    n_q = s_len // tq
    n_diag = tq // tk
    lane = lax.broadcasted_iota(jnp.int32, (tq, LANES), 1)
    head0 = lane < hd
    row_in = lax.broadcasted_iota(jnp.int32, (tq, tk), 0)
    col_in = lax.broadcasted_iota(jnp.int32, (tq, tk), 1)
    META = "meta"
    n_hp = qc_ref.shape[2] // LANES
    n_h = 2 * n_hp
    hsl = lambda h: slice(h * LANES, (h + 1) * LANES)
    psl = lambda p: slice(p * LANES, (p + 1) * LANES)
    xsl = lambda p: slice(2 * p * LANES, 2 * (p + 1) * LANES)

    def place(main, fpacked, p_mat, c_ref):
        lhs = main if fpacked is None else jnp.concatenate([main, fpacked], axis=1)
        return (_dot(lhs, p_mat) + c_ref[...]).astype(BF16)

    fpm = fm_ref[...]
    for p in range(n_hp):
        for c0 in range(0, s_len, tq):
            rows = slice(c0, c0 + tq)
            fp = f_ref[0, rows, :]
            q_ref[rows, xsl(p)] = place(qc_ref[0, rows, psl(p)], fp, pq_ref[p], cq_ref)
            k_ref[rows, xsl(p)] = place(kc_ref[0, rows, psl(p)], fp, pk_ref[p], ck_ref)
            v_ref[rows, xsl(p)] = place(vc_ref[0, rows, psl(p)], None, pv_ref[0], cv_ref)
        km_ref[:, xsl(p)] = place(kmc_ref[:, psl(p)], fpm, pk_ref[p], ck_ref)
        vm_ref[:, xsl(p)] = place(vmc_ref[:, psl(p)], None, pv_ref[0], cv_ref)

    def q_tile(i, n_real, static_tiles):
        r0 = pl.multiple_of(i * tq, tq)
        qh = [q_ref[pl.ds(r0, tq), hsl(h)] for h in range(n_h)]

        def mask_of(t):
            if t is META:
                return col_in < n_meta
            return None if t[1] is None else (col_in + t[1] * tk) <= row_in

        def load(t, ref, mref, h):
            if t is META:
                return mref[:, hsl(h)]
            return ref[pl.ds(pl.multiple_of(t[0] * tk, tk), tk), hsl(h)]

        def step(par, t1, t2, t3):
            if t3 is not None:
                for h in range(n_h):
                    acc_scr[h] = acc_scr[h] * alpha_scr[h] + _dot(p_scr[h], load(t3, v_ref, vm_ref, h))
            if t2 is not None:
                m = mask_of(t2)
                for h in range(n_h):
                    z = z_scr[1 - par, h]
                    if m is not None:
                        z = jnp.where(m, z, -jnp.inf)
                    m_old = m_scr[h]
                    m_new = jnp.maximum(m_old, jnp.max(z, axis=-1, keepdims=True))
                    p_scr[h] = jnp.exp(z - m_new).astype(BF16)
                    alpha_scr[h] = jnp.exp(m_old - m_new)[:, :LANES]
                    m_scr[h] = m_new
            if t1 is not None:
                for h in range(n_h):
                    z_scr[par, h] = _dot_nt(qh[h], load(t1, k_ref, km_ref, h))

        m_scr[...] = jnp.full(m_scr.shape, -jnp.inf, F32)
        acc_scr[...] = jnp.zeros_like(acc_scr)

        if static_tiles:
            tiles = [META] + [(n, n - (n_real - n_diag) if n >= n_real - n_diag else None)
                              for n in range(n_real)]
            for s in range(len(tiles) + 2):
                pick = lambda j: tiles[j] if 0 <= j < len(tiles) else None
                step(s % 2, pick(s), pick(s - 1), pick(s - 2))
        else:
            pos = lambda p: META if (isinstance(p, int) and p == 0) else (p - 1, None)
            n_pro = 4
            for s in range(n_pro):
                pick = lambda j: pos(j) if j >= 0 else None
                step(s % 2, pick(s), pick(s - 1), pick(s - 2))

            def body(j, carry):
                s = n_pro + 2 * j
                step(0, pos(s), pos(s - 1), pos(s - 2))
                step(1, pos(s + 1), pos(s), pos(s - 1))
                return carry

            lax.fori_loop(0, (n_real - n_pro) // 2, body, 0)
            a, b = n_real - 2, n_real - 1
            step(0, (b, None), (a, 0), (a - 1, None))
            step(1, None, (b, 1), (a, None))
            step(0, None, None, (b, None))

        for p in range(n_hp):
            res0, res1 = acc_scr[2 * p], acc_scr[2 * p + 1]
            rot0 = pltpu.roll(res0, hd, 1)
            rot1 = pltpu.roll(res1, hd, 1)
            o_ref[0, pl.ds(r0, tq), psl(p)] = jnp.where(
                head0, res0 / rot0, rot1 / res1).astype(o_ref.dtype)

    q_tile(0, n_diag, True)
    if n_q > 1:
        assert n_diag == 2, "the pipelined sweep is written for two diagonal key tiles per query tile"

        def outer(i, carry):
            q_tile(i, n_diag * (i + 1), False)
            return carry

        lax.fori_loop(1, n_q, outer, 0)


def _fox_placement(n_pair, hd):
    n_heads = 2 * n_pair
    pq = np.zeros((n_pair, 2 * LANES, 2 * LANES), np.float32)
    pk = np.zeros((n_pair, 2 * LANES, 2 * LANES), np.float32)
    pv = np.zeros((1, LANES, 2 * LANES), np.float32)
    cq = np.zeros((1, 2 * LANES), np.float32)
    ck = np.zeros((1, 2 * LANES), np.float32)
    cv = np.zeros((1, 2 * LANES), np.float32)
    for h in range(2):
        base = h * LANES
        for j in range(hd):
            pq[:, h * hd + j, base + j] = hd ** -0.5
            pk[:, h * hd + j, base + j] = 1.0
            pv[:, h * hd + j, base + j] = 1.0
        for p in range(3):
            for hp in range(n_pair):
                pq[hp, LANES + p * n_heads + 2 * hp + h, base + hd + p] = 1.0
                pk[hp, LANES + p * n_heads + 2 * hp + h, base + hd + 3 + p] = -1.0
        cq[0, base + hd + 3:base + hd + 6] = 1.0
        ck[0, base + hd:base + hd + 3] = 1.0
        cv[0, base + hd:base + LANES] = 1.0
    bf = lambda a: jnp.asarray(a, BF16)
    return bf(pq), bf(pk), bf(pv), jnp.asarray(cq), jnp.asarray(ck), jnp.asarray(cv)


def forgetting_attention(q, kv, kv_meta, fparts, fparts_meta, d_model, hd):
    b, s_len, _ = q.shape
    n_pair = d_model // LANES
    n_meta = kv_meta.shape[0]
    tq = min(ATT_TQ, s_len)
    tk = min(ATT_TK, s_len)
    assert n_meta <= tk and tk == LANES and 3 * 2 * n_pair <= LANES
    kvm = jnp.zeros((tk, kv_meta.shape[1]), kv_meta.dtype).at[:n_meta].set(kv_meta)
    fpm = jnp.zeros((tk, LANES), fparts_meta.dtype).at[:n_meta].set(fparts_meta)
    pq, pk, pv, cq, ck, cv = _fox_placement(n_pair, hd)
    n_hp = FOX_PAIRS if n_pair % FOX_PAIRS == 0 else 1
    width = n_hp * LANES
    n_grp = n_pair // n_hp
    nh = 2 * n_hp
    blk = lambda off: pl.BlockSpec((1, s_len, width), lambda bi, hp: (bi, 0, off + hp))
    mblk = lambda off: pl.BlockSpec((tk, width), lambda bi, hp: (0, off + hp))
    crow = pl.BlockSpec((1, 2 * LANES), lambda bi, hp: (0, 0))
    in_specs = [blk(0), blk(0), blk(n_grp),
                pl.BlockSpec((1, s_len, LANES), lambda bi, hp: (bi, 0, 0)),
                mblk(0), mblk(n_grp),
                pl.BlockSpec((tk, LANES), lambda bi, hp: (0, 0)),
                pl.BlockSpec((n_hp, 2 * LANES, 2 * LANES), lambda bi, hp: (hp, 0, 0)),
                pl.BlockSpec((n_hp, 2 * LANES, 2 * LANES), lambda bi, hp: (hp, 0, 0)),
                pl.BlockSpec((1, LANES, 2 * LANES), lambda bi, hp: (0, 0, 0)),
                crow, crow, crow]
    ext = lambda rows: pltpu.VMEM((rows, nh * LANES), BF16)
    return pl.pallas_call(
        functools.partial(_fox_kernel, tq=tq, tk=tk, hd=hd, n_meta=n_meta),
        grid=(b, n_grp),
        in_specs=in_specs,
        out_specs=pl.BlockSpec((1, s_len, width), lambda bi, hp: (bi, 0, hp)),
        out_shape=jax.ShapeDtypeStruct((b, s_len, d_model), BF16),
        scratch_shapes=[ext(s_len), ext(s_len), ext(s_len), ext(tk), ext(tk),
                        pltpu.VMEM((2, nh, tq, tk), F32), pltpu.VMEM((nh, tq, tk), BF16),
                        pltpu.VMEM((nh, tq, tk), F32), pltpu.VMEM((nh, tq, LANES), F32),
                        pltpu.VMEM((nh, tq, LANES), F32)],
        compiler_params=_cparams(2),
        name="forgetting_attention",
    )(q, kv, kv, fparts, kvm, kvm, fpm, pq, pk, pv, cq, ck, cv)


def _router_kernel(x_ref, g_ref, wr_ref, xn_ref, id_ref, g1_ref, g2_ref, *, n_experts):
    xn = _rms(x_ref[...], g_ref[...])
    xn_ref[...] = xn.astype(BF16)
    a1, a2, a3 = _split3(xn)
    b1, b2, b3 = _split3(wr_ref[...])
    logits = (_dot(a1, b1) + _dot(a1, b2) + _dot(a2, b1)
              + _dot(a2, b2) + _dot(a1, b3) + _dot(a3, b1))
    lane = lax.broadcasted_iota(jnp.int32, logits.shape, 1)
    lg = jnp.where(lane < n_experts, logits, -jnp.inf)
    m1 = jnp.max(lg, axis=-1, keepdims=True)
    i1 = jnp.min(jnp.where(lg == m1, lane, LANES), axis=-1, keepdims=True)
    lg2 = jnp.where(lane == i1, -jnp.inf, lg)
    m2 = jnp.max(lg2, axis=-1, keepdims=True)
    i2 = jnp.min(jnp.where(lg2 == m2, lane, LANES), axis=-1, keepdims=True)
    e2 = jnp.exp(m2 - m1)
    den = 1.0 + e2
    id_ref[...] = jnp.where(lane == 0, i1, jnp.where(lane == 1, i2, 0))
    g1_ref[...] = jnp.broadcast_to(1.0 / den, logits.shape)
    g2_ref[...] = jnp.broadcast_to(e2 / den, logits.shape)


def moe_router(x, g, w_router):
    rows, d = x.shape
    n_experts = w_router.shape[1]
    wr = jnp.zeros((d, LANES), F32).at[:, :n_experts].set(w_router)
    tm = _row_tile(rows)
    return pl.pallas_call(
        functools.partial(_router_kernel, n_experts=n_experts),
        grid=(rows // tm,),
        in_specs=[
            pl.BlockSpec((tm, d), lambda i: (i, 0)),
            pl.BlockSpec((1, d), lambda i: (0, 0)),
            pl.BlockSpec((d, LANES), lambda i: (0, 0)),
        ],
        out_specs=[pl.BlockSpec((tm, d), lambda i: (i, 0)),
                   pl.BlockSpec((tm, LANES), lambda i: (i, 0)),
                   pl.BlockSpec((tm, LANES), lambda i: (i, 0)),
                   pl.BlockSpec((tm, LANES), lambda i: (i, 0))],
        out_shape=[jax.ShapeDtypeStruct((rows, d), BF16),
                   jax.ShapeDtypeStruct((rows, LANES), jnp.int32),
                   jax.ShapeDtypeStruct((rows, LANES), F32),
                   jax.ShapeDtypeStruct((rows, LANES), F32)],
        compiler_params=_cparams(1),
        name="moe_router",
    )(x, g, wr)


def _moe_ffn_kernel(te_ref, nt_ref, x_ref, wgu_ref, wd_ref, o_ref, *, ff, f_chunk):
    i = pl.program_id(0)

    @pl.when(i < nt_ref[0])
    def _():
        acc = _swiglu_acc(x_ref[...], wgu_ref, wd_ref, ff, f_chunk, widx=(0,))
        o_ref[...] = acc.astype(o_ref.dtype)

    @pl.when(i >= nt_ref[0])
    def _():
        o_ref[...] = jnp.zeros_like(o_ref)


def moe_grouped_ffn(xs, tile_expert, n_tiles_used, wgu, wd):
    p_rows, d = xs.shape
    n_experts, ff, _ = wd.shape
    tm = MOE_TM
    grid_spec = pltpu.PrefetchScalarGridSpec(
        num_scalar_prefetch=2,
        grid=(p_rows // tm,),
        in_specs=[
            pl.BlockSpec((tm, d), lambda i, te, nt: (i, 0)),
            pl.BlockSpec((1, d, 2 * ff), lambda i, te, nt: (te[i], 0, 0)),
            pl.BlockSpec((1, ff, d), lambda i, te, nt: (te[i], 0, 0)),
        ],
        out_specs=pl.BlockSpec((tm, d), lambda i, te, nt: (i, 0)),
    )
    return pl.pallas_call(
        functools.partial(_moe_ffn_kernel, ff=ff, f_chunk=FF_CHUNK),
        grid_spec=grid_spec,
        out_shape=jax.ShapeDtypeStruct((p_rows, d), BF16),
        compiler_params=_cparams(1),
        name="moe_grouped_ffn",
    )(tile_expert, n_tiles_used, xs, wgu, wd)


def _combine_kernel(h_ref, y1_ref, y2_ref, g1_ref, g2_ref, o_ref):
    d = h_ref.shape[1]
    g1, g2 = g1_ref[...], g2_ref[...]
    for c0 in range(0, d, LANES):
        cs = slice(c0, c0 + LANES)
        o_ref[:, cs] = (h_ref[:, cs] + g1 * y1_ref[:, cs].astype(F32)
                        + g2 * y2_ref[:, cs].astype(F32))


def moe_block(h, g, w_router, wgu, wd):
    rows, d = h.shape
    n_experts = w_router.shape[1]
    tm = MOE_TM
    xn, ids, g1, g2 = moe_router(h, g, w_router)

    eid = ids[:, :TOP_K]
    sel = (eid[:, :, None] == jnp.arange(n_experts)[None, None, :]).any(axis=1)
    rank = jnp.cumsum(sel.astype(jnp.int32), axis=0) - 1
    cnt = rank[-1] + 1
    cnt_pad = ((cnt + tm - 1) // tm) * tm
    ends = jnp.cumsum(cnt_pad)
    offs = ends - cnt_pad
    pos = offs[eid] + jnp.take_along_axis(rank, eid, axis=1)
    p_rows = rows * TOP_K + n_experts * tm
    n_tiles = p_rows // tm
    tile_start = jnp.arange(n_tiles, dtype=jnp.int32) * tm
    tile_expert = jnp.minimum((tile_start[:, None] >= ends[None, :]).sum(axis=1), n_experts - 1)
    n_used = (ends[-1] // tm).astype(jnp.int32).reshape(1)

    tok = jnp.arange(rows, dtype=jnp.int32)[:, None]
    tok_c = jnp.sort((eid * rows + tok).reshape(-1)) % rows
    tok_c = jnp.concatenate([tok_c, jnp.zeros((p_rows - rows * TOP_K,), jnp.int32)])
    cstart = jnp.cumsum(cnt) - cnt
    slot = jnp.arange(p_rows, dtype=jnp.int32)
    tok_sorted = jnp.zeros((p_rows,), jnp.int32)
    for e in range(n_experts):
        inside = (slot >= offs[e]) & (slot < offs[e] + cnt[e])
        tok_sorted = jnp.where(inside, jnp.roll(tok_c, offs[e] - cstart[e]), tok_sorted)
    xs = xn.at[tok_sorted].get(mode="promise_in_bounds")
    ys = moe_grouped_ffn(xs, tile_expert.astype(jnp.int32), n_used, wgu, wd)
    y1 = ys.at[pos[:, 0]].get(mode="promise_in_bounds", unique_indices=True)
    y2 = ys.at[pos[:, 1]].get(mode="promise_in_bounds", unique_indices=True)
    tr = _row_tile(rows)
    spec = pl.BlockSpec((tr, d), lambda i: (i, 0))
    gspec = pl.BlockSpec((tr, LANES), lambda i: (i, 0))
    return pl.pallas_call(
        _combine_kernel,
        grid=(rows // tr,),
        in_specs=[spec, spec, spec, gspec, gspec],
        out_specs=spec,
        out_shape=jax.ShapeDtypeStruct((rows, d), F32),
        compiler_params=_cparams(1),
        name="moe_combine",
    )(h, y1, y2, g1, g2)


def kernel(x, meta_tokens, norm_attn_a, w_qkv_a, w_o_a, norm_kv, w_kvf, b_f, k_norm, norm_attn_b,
           w_q_b, q_norm_b, w_o_b, norm_ffn_dense, w_gu_dense, w_down_dense, norm_ffn_moe,
           w_router, w_gu_moe, w_down_moe):
    b, s_len, d = x.shape
    n_heads = b_f.shape[0]
    hd = d // n_heads
    n_meta = meta_tokens.shape[0]
    depth = norm_attn_a.shape[0] + norm_attn_b.shape[0]
    n_a = norm_attn_a.shape[0]
    assert 2 * hd == LANES and d % LANES == 0
    row = lambda v: v.reshape(1, -1).astype(F32)

    h = x.reshape(b * s_len, d)
    hm = meta_tokens.astype(x.dtype)
    for layer in range(depth):
        last = layer == depth - 1
        if layer < n_a:
            w_qkv = to_bf16(w_qkv_a[layer])
            g = row(norm_attn_a[layer])
            qkv = norm_matmul(h, g, w_qkv, BF16).reshape(b, s_len, 3 * d)
            qkv_m = norm_matmul(hm, g, w_qkv, BF16)
            o = stickbreak_attention(qkv, qkv_m, d, hd).reshape(b * s_len, d)
            o_m = stickbreak_attention(qkv_m[None], None, d, hd)[0]
            w_o = to_bf16(w_o_a[layer])
            h = matmul_residual(o, w_o, h)
            hm = matmul_residual(o_m, w_o, hm)
        else:
            if layer == n_a:
                g = row(norm_kv)
                w_kv = to_bf16(w_kvf[:, :2 * d])
                w_f = jnp.zeros((d, LANES), F32).at[:, :n_heads].set(w_kvf[:, 2 * d:]).astype(BF16)
                kgain = row(jnp.tile(k_norm, n_heads))
                shared = [(g, w_kv, BF16, kgain, hd), (g, w_f, F32, None, None)]
            i = layer - n_a
            q_proj = (row(norm_attn_b[i]), to_bf16(w_q_b[i]), BF16,
                      row(jnp.tile(q_norm_b[i], n_heads)), hd)
            if layer == n_a:
                kv, fl, q = norm_projections(h, shared + [q_proj])
                kv_m, fl_m = norm_projections(hm, shared)
                bias = jnp.zeros((1, LANES), F32).at[0, :n_heads].set(b_f.astype(F32))
                f_cum, f_cum_m = forget_cumsum(fl.reshape(b, s_len, LANES), fl_m, bias, n_heads)
                kv_sh = kv.reshape(b, s_len, 2 * d)
            else:
                q, = norm_projections(h, [q_proj])
            o = forgetting_attention(q.reshape(b, s_len, d), kv_sh, kv_m, f_cum, f_cum_m, d, hd)
            h = matmul_residual(o.reshape(b * s_len, d), to_bf16(w_o_b[i]), h)
            if not last:
                raise NotImplementedError("meta-row queries in forgetting layers before the last")
        j = layer // 2
        if layer % 2 == 0:
            wgu = to_bf16(w_gu_dense[j])
            wd = to_bf16(w_down_dense[j])
            g = row(norm_ffn_dense[j])
            h = dense_ffn(h, g, wgu, wd)
            if not last:
                hm = dense_ffn(hm, g, wgu, wd)
        else:
            h = moe_block(h, row(norm_ffn_moe[j]), w_router[j],
                          to_bf16(w_gu_moe[j]), to_bf16(w_down_moe[j]))
            if not last:
                raise NotImplementedError("meta rows through a MoE layer before the last")
    return h.reshape(b, s_len, d)
```

```python
import functools

import jax
import jax.numpy as jnp
import numpy as np
from jax import lax
from jax.experimental import pallas as pl
from jax.experimental.pallas import tpu as pltpu

F32 = jnp.float32
BF16 = jnp.bfloat16

EPS = 1e-6
LOG2E = 1.4426950408889634
TOP_K = 2
LANES = 128
VMEM_LIMIT = 56 * 1024 * 1024
ROW_TILE = 512
COL_CHUNK = 512
FF_CHUNK = 256
ATT_TQ = 256
ATT_TK = 128
ATT_PAIRS = 4
FOX_PAIRS = 2
MOE_TM = 512
CUM_CHUNK = 256
CAST_BLOCK_BYTES = 6 * 1024 * 1024


def _cparams(n_axes):
    return pltpu.CompilerParams(dimension_semantics=("arbitrary",) * n_axes,
                                vmem_limit_bytes=VMEM_LIMIT)


def _split3(x):
    h1 = x.astype(BF16)
    r1 = x - h1.astype(F32)
    h2 = r1.astype(BF16)
    h3 = (r1 - h2.astype(F32)).astype(BF16)
    return h1, h2, h3


def _split2(x):
    h1 = x.astype(BF16)
    h2 = (x - h1.astype(F32)).astype(BF16)
    return h1, h2


def _dot(a, b):
    return jnp.dot(a, b, preferred_element_type=F32)


def _dot_nt(a, b):
    return lax.dot_general(a, b, (((1,), (1,)), ((), ())), preferred_element_type=F32)


def _rms(xf, g):
    ms = jnp.mean(xf * xf, axis=-1, keepdims=True)
    return xf * lax.rsqrt(ms + EPS) * g


def _row_tile(rows, pref=ROW_TILE):
    return pref if rows % pref == 0 else rows


def _col_chunk(*widths):
    c = COL_CHUNK
    while any(w % c for w in widths):
        c //= 2
    assert c >= LANES
    return c


def _cast_kernel(*refs):
    n = len(refs) // 2
    for x_ref, o_ref in zip(refs[:n], refs[n:]):
        o_ref[...] = x_ref[...].astype(o_ref.dtype)


def to_bf16(w, splits=1):
    shape = w.shape
    cols = shape[-1]
    assert cols % (splits * LANES) == 0
    sc = cols // splits
    w2 = w.reshape(-1, cols)
    rows = w2.shape[0]
    rb = rows
    for cand in (2048, 1024, 512, 256, 128, 64):
        if rows % cand == 0 and cand * cols * 4 <= CAST_BLOCK_BYTES:
            rb = cand
            break
    spec = lambda s: pl.BlockSpec((rb, sc), functools.partial(lambda i, s: (i, s), s=s))
    outs = pl.pallas_call(
        _cast_kernel,
        grid=(rows // rb,),
        in_specs=[spec(s) for s in range(splits)],
        out_specs=[pl.BlockSpec((rb, sc), lambda i: (i, 0)) for _ in range(splits)],
        out_shape=[jax.ShapeDtypeStruct((rows, sc), BF16) for _ in range(splits)],
        compiler_params=_cparams(1),
        name="to_bf16",
    )(*([w2] * splits))
    outs = [o.reshape(shape[:-1] + (sc,)) for o in outs]
    return outs[0] if splits == 1 else outs


def _norm_matmul_kernel(*refs, cfg):
    n_p = len(cfg)
    x_ref, ins, outs = refs[0], refs[1:1 + 4 * n_p], refs[1 + 4 * n_p:]
    xf = x_ref[...]
    xs = xf * lax.rsqrt(jnp.mean(xf * xf, axis=-1, keepdims=True) + EPS)
    for pi, (n_chunk, head_cols) in enumerate(cfg):
        g_ref, w_ref, hg_ref, gm_ref = ins[4 * pi:4 * pi + 4]
        o_ref = outs[pi]
        xn = (xs * g_ref[...]).astype(BF16)
        n = w_ref.shape[1]
        for c0 in range(0, n, n_chunk):
            acc = _dot(xn, w_ref[:, c0:c0 + n_chunk])
            if c0 < head_cols:
                s1, s2 = _split2(acc * acc)
                ms = _dot(s1, gm_ref[...]) + _dot(s2, gm_ref[...])
                acc = acc * lax.rsqrt(ms + EPS) * hg_ref[:, c0:c0 + n_chunk]
            o_ref[:, c0:c0 + n_chunk] = acc.astype(o_ref.dtype)


def norm_projections(x, projs):
    rows, d = x.shape
    tm = _row_tile(rows)
    const = lambda shape: pl.BlockSpec(shape, lambda i: (0,) * len(shape))
    cfg, args, in_specs, out_specs, out_shape = [], [x], [pl.BlockSpec((tm, d), lambda i: (i, 0))], [], []
    for g, w, out_dtype, head_gain, head_dim in projs:
        n = w.shape[1]
        if head_gain is None:
            n_chunk = _col_chunk(n)
            head_cols = 0
            head_gain = jnp.zeros((1, n_chunk), F32)
            head_dim = n_chunk
        else:
            head_cols = head_gain.shape[1]
            n_chunk = _col_chunk(n, head_cols)
            assert n_chunk % head_dim == 0
        grp = jnp.arange(n_chunk) // head_dim
        gm = ((grp[:, None] == grp[None, :]).astype(F32) / head_dim).astype(BF16)
        cfg.append((n_chunk, head_cols))
        args += [g, w, head_gain, gm]
        in_specs += [const((1, d)), const((d, n)), const(head_gain.shape), const((n_chunk, n_chunk))]
        out_specs.append(pl.BlockSpec((tm, n), lambda i: (i, 0)))
        out_shape.append(jax.ShapeDtypeStruct((rows, n), out_dtype))
    return pl.pallas_call(
        functools.partial(_norm_matmul_kernel, cfg=tuple(cfg)),
        grid=(rows // tm,),
        in_specs=in_specs,
        out_specs=out_specs,
        out_shape=out_shape,
        compiler_params=_cparams(1),
        name="norm_matmul",
    )(*args)


def norm_matmul(x, g, w, out_dtype, *, head_gain=None, head_dim=None):
    return norm_projections(x, [(g, w, out_dtype, head_gain, head_dim)])[0]


def _matmul_res_kernel(a_ref, w_ref, r_ref, o_ref, *, n_chunk):
    a = a_ref[...]
    n = w_ref.shape[1]
    for c0 in range(0, n, n_chunk):
        o_ref[:, c0:c0 + n_chunk] = r_ref[:, c0:c0 + n_chunk] + _dot(a, w_ref[:, c0:c0 + n_chunk])


def matmul_residual(a, w, res):
    rows, k = a.shape
    n = w.shape[1]
    tm = _row_tile(rows)
    n_chunk = _col_chunk(n)
    return pl.pallas_call(
        functools.partial(_matmul_res_kernel, n_chunk=n_chunk),
        grid=(rows // tm,),
        in_specs=[
            pl.BlockSpec((tm, k), lambda i: (i, 0)),
            pl.BlockSpec((k, n), lambda i: (0, 0)),
            pl.BlockSpec((tm, n), lambda i: (i, 0)),
        ],
        out_specs=pl.BlockSpec((tm, n), lambda i: (i, 0)),
        out_shape=jax.ShapeDtypeStruct((rows, n), F32),
        compiler_params=_cparams(1),
        name="matmul_residual",
    )(a, w, res)


def _swiglu_acc(xn, wg_ref, wu_ref, wda_ref, wdb_ref, f_chunk, widx=()):
    ff = wg_ref.shape[-1]
    acc_a = acc_b = None
    for c0 in range(0, ff, f_chunk):
        cols = widx + (slice(None), slice(c0, c0 + f_chunk))
        rows = widx + (slice(c0, c0 + f_chunk), slice(None))
        gt = _dot(xn, wg_ref[cols])
        up = _dot(xn, wu_ref[cols])
        act = (gt * (1.0 / (1.0 + jnp.exp(-gt))) * up).astype(BF16)
        pa, pb = _dot(act, wda_ref[rows]), _dot(act, wdb_ref[rows])
        acc_a = pa if acc_a is None else acc_a + pa
        acc_b = pb if acc_b is None else acc_b + pb
    return acc_a, acc_b


def _dense_ffn_kernel(x_ref, g_ref, wg_ref, wu_ref, wda_ref, wdb_ref, o_ref, *, f_chunk):
    xf = x_ref[...]
    xn = _rms(xf, g_ref[...]).astype(BF16)
    acc_a, acc_b = _swiglu_acc(xn, wg_ref, wu_ref, wda_ref, wdb_ref, f_chunk)
    half = acc_a.shape[1]
    o_ref[:, :half] = xf[:, :half] + acc_a
    o_ref[:, half:] = xf[:, half:] + acc_b


def dense_ffn(x, g, wg, wu, wda, wdb):
    rows, d = x.shape
    ff = wg.shape[1]
    tm = _row_tile(rows)
    const = lambda shape: pl.BlockSpec(shape, lambda i: (0, 0))
    return pl.pallas_call(
        functools.partial(_dense_ffn_kernel, f_chunk=FF_CHUNK),
        grid=(rows // tm,),
        in_specs=[
            pl.BlockSpec((tm, d), lambda i: (i, 0)),
            const((1, d)), const((d, ff)), const((d, ff)), const((ff, d // 2)), const((ff, d // 2)),
        ],
        out_specs=pl.BlockSpec((tm, d), lambda i: (i, 0)),
        out_shape=jax.ShapeDtypeStruct((rows, d), F32),
        compiler_params=_cparams(1),
        name="dense_ffn",
    )(x, g, wg, wu, wda, wdb)


def _sb_suffix(z, mask, u_ref):
    zb = z.astype(BF16)
    one, zero = jnp.asarray(1.0, BF16), jnp.asarray(0.0, BF16)
    sp = jnp.maximum(zb, zero) + jnp.log(one + jnp.exp2(jnp.abs(zb) * jnp.asarray(-LOG2E, BF16)))
    if mask is not None:
        sp = jnp.where(mask, sp, zero)
    return _dot(sp, u_ref[...])


def _sb_kernel(*refs, tq, tk, hd, n_meta, scale):
    z_scr, cs_scr, run_scr, acc_scr = refs[-4:]
    if n_meta:
        q_ref, k_ref, v_ref, km_ref, vm_ref, u_ref, o_ref = refs[:-4]
    else:
        q_ref, k_ref, v_ref, u_ref, o_ref = refs[:-4]
    s_len = q_ref.shape[1]
    n_hp = q_ref.shape[2] // LANES
    psl = lambda p: slice(p * LANES, (p + 1) * LANES)
    n_q = s_len // tq
    n_diag = tq // tk
    has_meta = 1 if n_meta else 0
    lane = lax.broadcasted_iota(jnp.int32, (tq, LANES), 1)
    head0 = lane < hd
    row_in = lax.broadcasted_iota(jnp.int32, (tq, tk), 0)
    col_in = lax.broadcasted_iota(jnp.int32, (tq, tk), 1)
    META = "meta"

    def q_tile(i, n_real, static_tiles):
        r0 = pl.multiple_of(i * tq, tq)
        qh = []
        for p in range(n_hp):
            q = q_ref[0, pl.ds(r0, tq), psl(p)] * scale
            qh += [jnp.where(head0, q, jnp.zeros_like(q)), jnp.where(head0, jnp.zeros_like(q), q)]

        def offset(n):
            return pl.multiple_of(r0 + (n_diag - 1 - n) * tk, tk)

        def mask_of(n):
            if n is META:
                return col_in < n_meta
            if isinstance(n, int) and n < n_diag:
                return (col_in + (n_diag - 1 - n) * tk) < row_in
            return None

        def load(n, ref, mref):
            return mref[...] if n is META else ref[0, pl.ds(offset(n), tk), :]

        def step(par, t1, t2, t3):
            if t3 is not None:
                v, m = load(t3, v_ref, vm_ref if n_meta else None), mask_of(t3)
                for h in range(2 * n_hp):
                    g = run_scr[h] + cs_scr[h]
                    a = jnp.exp(z_scr[par, h] + g)
                    if m is not None:
                        a = jnp.where(m, a, 0.0)
                    run_scr[h] = jnp.broadcast_to(g[:, 0:1], g.shape)
                    acc_scr[h] += _dot(a.astype(BF16), v[:, psl(h // 2)])
            if t2 is not None:
                m = mask_of(t2)
                for h in range(2 * n_hp):
                    cs_scr[h] = _sb_suffix(z_scr[1 - par, h], m, u_ref)
            if t1 is not None:
                k = load(t1, k_ref, km_ref if n_meta else None)
                for h in range(2 * n_hp):
                    z_scr[par, h] = _dot_nt(qh[h], k[:, psl(h // 2)])

        run_scr[...] = jnp.zeros_like(run_scr)
        acc_scr[...] = jnp.zeros_like(acc_scr)

        if static_tiles:
            tiles = list(range(n_real)) + ([META] if has_meta else [])
            for s in range(len(tiles) + 2):
                pick = lambda j: tiles[j] if 0 <= j < len(tiles) else None
                step(s % 2, pick(s), pick(s - 1), pick(s - 2))
        else:
            n_pro = n_diag + 2
            assert n_pro % 2 == 0
            for s in range(n_pro):
                pick = lambda j: j if j >= 0 else None
                step(s % 2, pick(s), pick(s - 1), pick(s - 2))

            def body(j, carry):
                s = n_pro + 2 * j
                step(0, s, s - 1, s - 2)
                step(1, s + 1, s, s - 1)
                return carry

            lax.fori_loop(0, (n_real - n_pro) // 2, body, 0)
            last = n_real - 1
            step(0, META if has_meta else None, last, last - 1)
            step(1, None, META if has_meta else None, last)
            if has_meta:
                step(0, None, None, META)

        for p in range(n_hp):
            o_ref[0, pl.ds(r0, tq), psl(p)] = jnp.where(
                head0, acc_scr[2 * p], acc_scr[2 * p + 1]).astype(o_ref.dtype)

    q_tile(0, n_diag, True)
    if n_q > 1:
        assert n_diag >= 2, "the pipelined prologue needs two diagonal key tiles per query tile"

        def outer(i, carry):
            q_tile(i, n_diag * (i + 1), False)
            return carry

        lax.fori_loop(1, n_q, outer, 0)


def _suffix_matrix(t):
    r = jnp.arange(t)
    return -(r[:, None] >= r[None, :]).astype(BF16)


def stickbreak_attention(qkv, qkv_meta, d_model, hd):
    b, s_len, _ = qkv.shape
    n_pair = d_model // LANES
    tq = min(ATT_TQ, s_len)
    tk = min(ATT_TK, s_len)
    n_meta = 0 if qkv_meta is None else qkv_meta.shape[0]
    n_hp = ATT_PAIRS if n_pair % ATT_PAIRS == 0 else 1
    width = n_hp * LANES
    n_grp = n_pair // n_hp
    col = lambda which: (lambda bi, hp: (bi, 0, which * n_grp + hp))
    in_specs = [pl.BlockSpec((1, s_len, width), col(0)),
                pl.BlockSpec((1, s_len, width), col(1)),
                pl.BlockSpec((1, s_len, width), col(2))]
    args = [qkv, qkv, qkv]
    if n_meta:
        assert n_meta <= tk
        meta_pad = jnp.zeros((tk, qkv_meta.shape[1]), qkv_meta.dtype).at[:n_meta].set(qkv_meta)
        in_specs += [pl.BlockSpec((tk, width), lambda bi, hp: (0, n_grp + hp)),
                     pl.BlockSpec((tk, width), lambda bi, hp: (0, 2 * n_grp + hp))]
        args += [meta_pad, meta_pad]
    in_specs.append(pl.BlockSpec((tk, tk), lambda bi, hp: (0, 0)))
    args.append(_suffix_matrix(tk))
    kern = functools.partial(_sb_kernel, tq=tq, tk=tk, hd=hd, n_meta=n_meta, scale=hd ** -0.5)
    nh = 2 * n_hp
    return pl.pallas_call(
        kern,
        grid=(b, n_grp),
        in_specs=in_specs,
        out_specs=pl.BlockSpec((1, s_len, width), lambda bi, hp: (bi, 0, hp)),
        out_shape=jax.ShapeDtypeStruct((b, s_len, d_model), BF16),
        scratch_shapes=[pltpu.VMEM((2, nh, tq, tk), F32), pltpu.VMEM((nh, tq, tk), F32),
                        pltpu.VMEM((nh, tq, tk), F32), pltpu.VMEM((nh, tq, LANES), F32)],
        compiler_params=_cparams(2),
        name="stickbreak_attention",
    )(*args)


def _log_sigmoid(y):
    return jnp.minimum(y, 0.0) - jnp.log(1.0 + jnp.exp(-jnp.abs(y)))


def _tri_cumsum(tri, x):
    h1, h2, h3 = _split3(x)
    return _dot(tri, h1) + _dot(tri, h2) + _dot(tri, h3)


def _forget_cumsum_kernel(f_ref, fm_ref, b_ref, tri_ref, trim_ref, o_ref, om_ref, *, chunk, n_heads):
    bias = b_ref[...]

    def packed(f):
        lane = lax.broadcasted_iota(jnp.int32, f.shape, 1)
        out = None
        for a, part in enumerate(_split3(f)):
            pa = jnp.where(lane < n_heads, part.astype(F32), 0.0)
            pa = pa if a == 0 else pltpu.roll(pa, a * n_heads, 1)
            out = pa if out is None else out + pa
        return out.astype(BF16)

    fm = _tri_cumsum(trim_ref[...], _log_sigmoid(fm_ref[...] + bias))
    om_ref[...] = packed(fm)
    n_meta = fm.shape[0]
    carry = fm[n_meta - 1:n_meta, :]
    s_len = f_ref.shape[1]
    for c0 in range(0, s_len, chunk):
        fc = _tri_cumsum(tri_ref[...], _log_sigmoid(f_ref[0, c0:c0 + chunk, :] + bias)) + carry
        o_ref[0, c0:c0 + chunk, :] = packed(fc)
        carry = fc[chunk - 1:chunk, :]


def forget_cumsum(f_logit, f_logit_meta, bias, n_heads):
    b, s_len, w = f_logit.shape
    n_meta = f_logit_meta.shape[0]
    assert 3 * n_heads <= w
    chunk = min(CUM_CHUNK, s_len)
    tri = lambda t: (jnp.arange(t)[:, None] >= jnp.arange(t)[None, :]).astype(BF16)
    return pl.pallas_call(
        functools.partial(_forget_cumsum_kernel, chunk=chunk, n_heads=n_heads),
        grid=(b,),
        in_specs=[
            pl.BlockSpec((1, s_len, w), lambda bi: (bi, 0, 0)),
            pl.BlockSpec((n_meta, w), lambda bi: (0, 0)),
            pl.BlockSpec((1, w), lambda bi: (0, 0)),
            pl.BlockSpec((chunk, chunk), lambda bi: (0, 0)),
            pl.BlockSpec((n_meta, n_meta), lambda bi: (0, 0)),
        ],
        out_specs=[pl.BlockSpec((1, s_len, w), lambda bi: (bi, 0, 0)),
                   pl.BlockSpec((n_meta, w), lambda bi: (0, 0))],
        out_shape=[jax.ShapeDtypeStruct((b, s_len, w), BF16),
                   jax.ShapeDtypeStruct((n_meta, w), BF16)],
        compiler_params=_cparams(1),
        name="forget_cumsum",
    )(f_logit, f_logit_meta, bias, tri(chunk), tri(n_meta))


def _fox_kernel(qc_ref, kc_ref, vc_ref, f_ref, kmc_ref, vmc_ref, fm_ref, pq_ref, pk_ref, pv_ref,
                cq_ref, ck_ref, cv_ref, o_ref,
                q_ref, k_ref, v_ref, km_ref, vm_ref, z_scr, p_scr, m_scr, alpha_scr, acc_scr,
                *, tq, tk, hd, n_meta):
    s_len = qc_ref.shape[1]
    n_q = s_len // tq
    n_diag = tq // tk
    lane = lax.broadcasted_iota(jnp.int32, (tq, LANES), 1)
    head0 = lane < hd
    row_in = lax.broadcasted_iota(jnp.int32, (tq, tk), 0)
    col_in = lax.broadcasted_iota(jnp.int32, (tq, tk), 1)
    META = "meta"
    n_hp = qc_ref.shape[2] // LANES
    n_h = 2 * n_hp
    hsl = lambda h: slice(h * LANES, (h + 1) * LANES)
    psl = lambda p: slice(p * LANES, (p + 1) * LANES)
    xsl = lambda p: slice(2 * p * LANES, 2 * (p + 1) * LANES)

    def place(main, fpacked, p_mat, c_ref):
        lhs = main if fpacked is None else jnp.concatenate([main, fpacked], axis=1)
        return (_dot(lhs, p_mat) + c_ref[...]).astype(BF16)

    fpm = fm_ref[...]
    for p in range(n_hp):
        for c0 in range(0, s_len, tq):
            rows = slice(c0, c0 + tq)
            fp = f_ref[0, rows, :]
            q_ref[rows, xsl(p)] = place(qc_ref[0, rows, psl(p)], fp, pq_ref[p], cq_ref)
            k_ref[rows, xsl(p)] = place(kc_ref[0, rows, psl(p)], fp, pk_ref[p], ck_ref)
            v_ref[rows, xsl(p)] = place(vc_ref[0, rows, psl(p)], None, pv_ref[0], cv_ref)
        km_ref[:, xsl(p)] = place(kmc_ref[:, psl(p)], fpm, pk_ref[p], ck_ref)
        vm_ref[:, xsl(p)] = place(vmc_ref[:, psl(p)], None, pv_ref[0], cv_ref)

    def q_tile(i, n_real, static_tiles):
        r0 = pl.multiple_of(i * tq, tq)
        qh = [q_ref[pl.ds(r0, tq), hsl(h)] for h in range(n_h)]

        def mask_of(t):
            if t is META:
                return col_in < n_meta
            return None if t[1] is None else (col_in + t[1] * tk) <= row_in

        def load(t, ref, mref, h):
            if t is META:
                return mref[:, hsl(h)]
            return ref[pl.ds(pl.multiple_of(t[0] * tk, tk), tk), hsl(h)]

        def step(par, t1, t2, t3):
            if t3 is not None:
                for h in range(n_h):
                    acc_scr[h] = acc_scr[h] * alpha_scr[h] + _dot(p_scr[h], load(t3, v_ref, vm_ref, h))
            if t2 is not None:
                m = mask_of(t2)
                for h in range(n_h):
                    z = z_scr[1 - par, h]
                    if m is not None:
                        z = jnp.where(m, z, -jnp.inf)
                    m_old = m_scr[h]
                    m_new = jnp.maximum(m_old, jnp.max(z, axis=-1, keepdims=True))
                    p_scr[h] = jnp.exp(z - m_new).astype(BF16)
                    alpha_scr[h] = jnp.exp(m_old - m_new)[:, :LANES]
                    m_scr[h] = m_new
            if t1 is not None:
                for h in range(n_h):
                    z_scr[par, h] = _dot_nt(qh[h], load(t1, k_ref, km_ref, h))

        m_scr[...] = jnp.full(m_scr.shape, -jnp.inf, F32)
        acc_scr[...] = jnp.zeros_like(acc_scr)

        if static_tiles:
            tiles = [META] + [(n, n - (n_real - n_diag) if n >= n_real - n_diag else None)
                              for n in range(n_real)]
            for s in range(len(tiles) + 2):
                pick = lambda j: tiles[j] if 0 <= j < len(tiles) else None
                step(s % 2, pick(s), pick(s - 1), pick(s - 2))
        else:
            pos = lambda p: META if (isinstance(p, int) and p == 0) else (p - 1, None)
            n_pro = 4
            for s in range(n_pro):
                pick = lambda j: pos(j) if j >= 0 else None
                step(s % 2, pick(s), pick(s - 1), pick(s - 2))

            def body(j, carry):
                s = n_pro + 2 * j
                step(0, pos(s), pos(s - 1), pos(s - 2))
                step(1, pos(s + 1), pos(s), pos(s - 1))
                return carry

            lax.fori_loop(0, (n_real - n_pro) // 2, body, 0)
            a, b = n_real - 2, n_real - 1
            step(0, (b, None), (a, 0), (a - 1, None))
            step(1, None, (b, 1), (a, None))
            step(0, None, None, (b, None))

        for p in range(n_hp):
            res0, res1 = acc_scr[2 * p], acc_scr[2 * p + 1]
            rot0 = pltpu.roll(res0, hd, 1)
            rot1 = pltpu.roll(res1, hd, 1)
            o_ref[0, pl.ds(r0, tq), psl(p)] = jnp.where(
                head0, res0 / rot0, rot1 / res1).astype(o_ref.dtype)

    q_tile(0, n_diag, True)
    if n_q > 1:
        assert n_diag == 2, "the pipelined sweep is written for two diagonal key tiles per query tile"

        def outer(i, carry):
            q_tile(i, n_diag * (i + 1), False)
            return carry

        lax.fori_loop(1, n_q, outer, 0)


def _fox_placement(n_pair, hd):
    n_heads = 2 * n_pair
    pq = np.zeros((n_pair, 2 * LANES, 2 * LANES), np.float32)
    pk = np.zeros((n_pair, 2 * LANES, 2 * LANES), np.float32)
    pv = np.zeros((1, LANES, 2 * LANES), np.float32)
    cq = np.zeros((1, 2 * LANES), np.float32)
    ck = np.zeros((1, 2 * LANES), np.float32)
    cv = np.zeros((1, 2 * LANES), np.float32)
    for h in range(2):
        base = h * LANES
        for j in range(hd):
            pq[:, h * hd + j, base + j] = hd ** -0.5
            pk[:, h * hd + j, base + j] = 1.0
            pv[:, h * hd + j, base + j] = 1.0
        for p in range(3):
            for hp in range(n_pair):
                pq[hp, LANES + p * n_heads + 2 * hp + h, base + hd + p] = 1.0
                pk[hp, LANES + p * n_heads + 2 * hp + h, base + hd + 3 + p] = -1.0
        cq[0, base + hd + 3:base + hd + 6] = 1.0
        ck[0, base + hd:base + hd + 3] = 1.0
        cv[0, base + hd:base + LANES] = 1.0
    bf = lambda a: jnp.asarray(a, BF16)
    return bf(pq), bf(pk), bf(pv), jnp.asarray(cq), jnp.asarray(ck), jnp.asarray(cv)


def forgetting_attention(q, kv, kv_meta, fparts, fparts_meta, d_model, hd):
    b, s_len, _ = q.shape
    n_pair = d_model // LANES
    n_meta = kv_meta.shape[0]
    tq = min(ATT_TQ, s_len)
    tk = min(ATT_TK, s_len)
    assert n_meta <= tk and tk == LANES and 3 * 2 * n_pair <= LANES
    kvm = jnp.zeros((tk, kv_meta.shape[1]), kv_meta.dtype).at[:n_meta].set(kv_meta)
    fpm = jnp.zeros((tk, LANES), fparts_meta.dtype).at[:n_meta].set(fparts_meta)
    pq, pk, pv, cq, ck, cv = _fox_placement(n_pair, hd)
    n_hp = FOX_PAIRS if n_pair % FOX_PAIRS == 0 else 1
    width = n_hp * LANES
    n_grp = n_pair // n_hp
    nh = 2 * n_hp
    blk = lambda off: pl.BlockSpec((1, s_len, width), lambda bi, hp: (bi, 0, off + hp))
    mblk = lambda off: pl.BlockSpec((tk, width), lambda bi, hp: (0, off + hp))
    crow = pl.BlockSpec((1, 2 * LANES), lambda bi, hp: (0, 0))
    in_specs = [blk(0), blk(0), blk(n_grp),
                pl.BlockSpec((1, s_len, LANES), lambda bi, hp: (bi, 0, 0)),
                mblk(0), mblk(n_grp),
                pl.BlockSpec((tk, LANES), lambda bi, hp: (0, 0)),
                pl.BlockSpec((n_hp, 2 * LANES, 2 * LANES), lambda bi, hp: (hp, 0, 0)),
                pl.BlockSpec((n_hp, 2 * LANES, 2 * LANES), lambda bi, hp: (hp, 0, 0)),
                pl.BlockSpec((1, LANES, 2 * LANES), lambda bi, hp: (0, 0, 0)),
                crow, crow, crow]
    ext = lambda rows: pltpu.VMEM((rows, nh * LANES), BF16)
    return pl.pallas_call(
        functools.partial(_fox_kernel, tq=tq, tk=tk, hd=hd, n_meta=n_meta),
        grid=(b, n_grp),
        in_specs=in_specs,
        out_specs=pl.BlockSpec((1, s_len, width), lambda bi, hp: (bi, 0, hp)),
        out_shape=jax.ShapeDtypeStruct((b, s_len, d_model), BF16),
        scratch_shapes=[ext(s_len), ext(s_len), ext(s_len), ext(tk), ext(tk),
                        pltpu.VMEM((2, nh, tq, tk), F32), pltpu.VMEM((nh, tq, tk), BF16),
                        pltpu.VMEM((nh, tq, tk), F32), pltpu.VMEM((nh, tq, LANES), F32),
                        pltpu.VMEM((nh, tq, LANES), F32)],
        compiler_params=_cparams(2),
        name="forgetting_attention",
    )(q, kv, kv, fparts, kvm, kvm, fpm, pq, pk, pv, cq, ck, cv)


def _router_kernel(x_ref, g_ref, wr_ref, xn_ref, id_ref, g1_ref, g2_ref, *, n_experts):
    xn = _rms(x_ref[...], g_ref[...])
    xn_ref[...] = xn.astype(BF16)
    a1, a2, a3 = _split3(xn)
    b1, b2, b3 = _split3(wr_ref[...])
    logits = (_dot(a1, b1) + _dot(a1, b2) + _dot(a2, b1)
              + _dot(a2, b2) + _dot(a1, b3) + _dot(a3, b1))
    lane = lax.broadcasted_iota(jnp.int32, logits.shape, 1)
    lg = jnp.where(lane < n_experts, logits, -jnp.inf)
    m1 = jnp.max(lg, axis=-1, keepdims=True)
    i1 = jnp.min(jnp.where(lg == m1, lane, LANES), axis=-1, keepdims=True)
    lg2 = jnp.where(lane == i1, -jnp.inf, lg)
    m2 = jnp.max(lg2, axis=-1, keepdims=True)
    i2 = jnp.min(jnp.where(lg2 == m2, lane, LANES), axis=-1, keepdims=True)
    e2 = jnp.exp(m2 - m1)
    den = 1.0 + e2
    id_ref[...] = jnp.where(lane == 0, i1, jnp.where(lane == 1, i2, 0))
    g1_ref[...] = jnp.broadcast_to(1.0 / den, logits.shape)
    g2_ref[...] = jnp.broadcast_to(e2 / den, logits.shape)


def moe_router(x, g, w_router):
    rows, d = x.shape
    n_experts = w_router.shape[1]
    wr = jnp.zeros((d, LANES), F32).at[:, :n_experts].set(w_router)
    tm = _row_tile(rows)
    return pl.pallas_call(
        functools.partial(_router_kernel, n_experts=n_experts),
        grid=(rows // tm,),
        in_specs=[
            pl.BlockSpec((tm, d), lambda i: (i, 0)),
            pl.BlockSpec((1, d), lambda i: (0, 0)),
            pl.BlockSpec((d, LANES), lambda i: (0, 0)),
        ],
        out_specs=[pl.BlockSpec((tm, d), lambda i: (i, 0)),
                   pl.BlockSpec((tm, LANES), lambda i: (i, 0)),
                   pl.BlockSpec((tm, LANES), lambda i: (i, 0)),
                   pl.BlockSpec((tm, LANES), lambda i: (i, 0))],
        out_shape=[jax.ShapeDtypeStruct((rows, d), BF16),
                   jax.ShapeDtypeStruct((rows, LANES), jnp.int32),
                   jax.ShapeDtypeStruct((rows, LANES), F32),
                   jax.ShapeDtypeStruct((rows, LANES), F32)],
        compiler_params=_cparams(1),
        name="moe_router",
    )(x, g, wr)


def _moe_ffn_kernel(te_ref, nt_ref, x_ref, wg_ref, wu_ref, wda_ref, wdb_ref, o_ref, *, f_chunk):
    i = pl.program_id(0)

    @pl.when(i < nt_ref[0])
    def _():
        acc_a, acc_b = _swiglu_acc(x_ref[...], wg_ref, wu_ref, wda_ref, wdb_ref, f_chunk, widx=(0,))
        half = acc_a.shape[1]
        o_ref[:, :half] = acc_a.astype(o_ref.dtype)
        o_ref[:, half:] = acc_b.astype(o_ref.dtype)

    @pl.when(i >= nt_ref[0])
    def _():
        o_ref[...] = jnp.zeros_like(o_ref)


def moe_grouped_ffn(xs, tile_expert, n_tiles_used, wg, wu, wda, wdb):
    p_rows, d = xs.shape
    n_experts, _, ff = wg.shape
    tm = MOE_TM
    expert = lambda shape: pl.BlockSpec(shape, lambda i, te, nt: (te[i], 0, 0))
    grid_spec = pltpu.PrefetchScalarGridSpec(
        num_scalar_prefetch=2,
        grid=(p_rows // tm,),
        in_specs=[
            pl.BlockSpec((tm, d), lambda i, te, nt: (i, 0)),
            expert((1, d, ff)), expert((1, d, ff)), expert((1, ff, d // 2)), expert((1, ff, d // 2)),
        ],
        out_specs=pl.BlockSpec((tm, d), lambda i, te, nt: (i, 0)),
    )
    return pl.pallas_call(
        functools.partial(_moe_ffn_kernel, f_chunk=FF_CHUNK),
        grid_spec=grid_spec,
        out_shape=jax.ShapeDtypeStruct((p_rows, d), BF16),
        compiler_params=_cparams(1),
        name="moe_grouped_ffn",
    )(tile_expert, n_tiles_used, xs, wg, wu, wda, wdb)


def _combine_kernel(h_ref, y1_ref, y2_ref, g1_ref, g2_ref, o_ref):
    d = h_ref.shape[1]
    g1, g2 = g1_ref[...], g2_ref[...]
    for c0 in range(0, d, LANES):
        cs = slice(c0, c0 + LANES)
        o_ref[:, cs] = (h_ref[:, cs] + g1 * y1_ref[:, cs].astype(F32)
                        + g2 * y2_ref[:, cs].astype(F32))


def moe_block(h, g, w_router, ffn_weights):
    rows, d = h.shape
    n_experts = w_router.shape[1]
    tm = MOE_TM
    xn, ids, g1, g2 = moe_router(h, g, w_router)

    eid = ids[:, :TOP_K]
    sel = (eid[:, :, None] == jnp.arange(n_experts)[None, None, :]).any(axis=1)
    rank = jnp.cumsum(sel.astype(jnp.int32), axis=0) - 1
    cnt = rank[-1] + 1
    cnt_pad = ((cnt + tm - 1) // tm) * tm
    ends = jnp.cumsum(cnt_pad)
    offs = ends - cnt_pad
    pos = offs[eid] + jnp.take_along_axis(rank, eid, axis=1)
    p_rows = rows * TOP_K + n_experts * tm
    n_tiles = p_rows // tm
    tile_start = jnp.arange(n_tiles, dtype=jnp.int32) * tm
    tile_expert = jnp.minimum((tile_start[:, None] >= ends[None, :]).sum(axis=1), n_experts - 1)
    n_used = (ends[-1] // tm).astype(jnp.int32).reshape(1)

    tok = jnp.arange(rows, dtype=jnp.int32)[:, None]
    tok_c = jnp.sort((eid * rows + tok).reshape(-1)) % rows
    tok_c = jnp.concatenate([tok_c, jnp.zeros((p_rows - rows * TOP_K,), jnp.int32)])
    cstart = jnp.cumsum(cnt) - cnt
    slot = jnp.arange(p_rows, dtype=jnp.int32)
    tok_sorted = jnp.zeros((p_rows,), jnp.int32)
    for e in range(n_experts):
        inside = (slot >= offs[e]) & (slot < offs[e] + cnt[e])
        tok_sorted = jnp.where(inside, jnp.roll(tok_c, offs[e] - cstart[e]), tok_sorted)
    xs = xn.at[tok_sorted].get(mode="promise_in_bounds")
    ys = moe_grouped_ffn(xs, tile_expert.astype(jnp.int32), n_used, *ffn_weights)
    y1 = ys.at[pos[:, 0]].get(mode="promise_in_bounds", unique_indices=True)
    y2 = ys.at[pos[:, 1]].get(mode="promise_in_bounds", unique_indices=True)
    tr = _row_tile(rows)
    spec = pl.BlockSpec((tr, d), lambda i: (i, 0))
    gspec = pl.BlockSpec((tr, LANES), lambda i: (i, 0))
    return pl.pallas_call(
        _combine_kernel,
        grid=(rows // tr,),
        in_specs=[spec, spec, spec, gspec, gspec],
        out_specs=spec,
        out_shape=jax.ShapeDtypeStruct((rows, d), F32),
        compiler_params=_cparams(1),
        name="moe_combine",
    )(h, y1, y2, g1, g2)


def kernel(x, meta_tokens, norm_attn_a, w_qkv_a, w_o_a, norm_kv, w_kvf, b_f, k_norm, norm_attn_b,
           w_q_b, q_norm_b, w_o_b, norm_ffn_dense, w_gu_dense, w_down_dense, norm_ffn_moe,
           w_router, w_gu_moe, w_down_moe):
    b, s_len, d = x.shape
    n_heads = b_f.shape[0]
    hd = d // n_heads
    n_meta = meta_tokens.shape[0]
    depth = norm_attn_a.shape[0] + norm_attn_b.shape[0]
    n_a = norm_attn_a.shape[0]
    assert 2 * hd == LANES and d % LANES == 0
    row = lambda v: v.reshape(1, -1).astype(F32)

    h = x.reshape(b * s_len, d)
    hm = meta_tokens.astype(x.dtype)
    for layer in range(depth):
        last = layer == depth - 1
        if layer < n_a:
            w_qkv = to_bf16(w_qkv_a[layer])
            g = row(norm_attn_a[layer])
            qkv = norm_matmul(h, g, w_qkv, BF16).reshape(b, s_len, 3 * d)
            qkv_m = norm_matmul(hm, g, w_qkv, BF16)
            o = stickbreak_attention(qkv, qkv_m, d, hd).reshape(b * s_len, d)
            o_m = stickbreak_attention(qkv_m[None], None, d, hd)[0]
            w_o = to_bf16(w_o_a[layer])
            h = matmul_residual(o, w_o, h)
            hm = matmul_residual(o_m, w_o, hm)
        else:
            if layer == n_a:
                g = row(norm_kv)
                w_kv = to_bf16(w_kvf[:, :2 * d])
                w_f = jnp.zeros((d, LANES), F32).at[:, :n_heads].set(w_kvf[:, 2 * d:]).astype(BF16)
                kgain = row(jnp.tile(k_norm, n_heads))
                shared = [(g, w_kv, BF16, kgain, hd), (g, w_f, F32, None, None)]
            i = layer - n_a
            q_proj = (row(norm_attn_b[i]), to_bf16(w_q_b[i]), BF16,
                      row(jnp.tile(q_norm_b[i], n_heads)), hd)
            if layer == n_a:
                kv, fl, q = norm_projections(h, shared + [q_proj])
                kv_m, fl_m = norm_projections(hm, shared)
                bias = jnp.zeros((1, LANES), F32).at[0, :n_heads].set(b_f.astype(F32))
                f_cum, f_cum_m = forget_cumsum(fl.reshape(b, s_len, LANES), fl_m, bias, n_heads)
                kv_sh = kv.reshape(b, s_len, 2 * d)
            else:
                q, = norm_projections(h, [q_proj])
            o = forgetting_attention(q.reshape(b, s_len, d), kv_sh, kv_m, f_cum, f_cum_m, d, hd)
            h = matmul_residual(o.reshape(b * s_len, d), to_bf16(w_o_b[i]), h)
            if not last:
                raise NotImplementedError("meta-row queries in forgetting layers before the last")
        j = layer // 2
        if layer % 2 == 0:
            ffn_w = to_bf16(w_gu_dense[j], splits=2) + to_bf16(w_down_dense[j], splits=2)
            g = row(norm_ffn_dense[j])
            h = dense_ffn(h, g, *ffn_w)
            if not last:
                hm = dense_ffn(hm, g, *ffn_w)
        else:
            ffn_w = to_bf16(w_gu_moe[j], splits=2) + to_bf16(w_down_moe[j], splits=2)
            h = moe_block(h, row(norm_ffn_moe[j]), w_router[j], ffn_w)
            if not last:
                raise NotImplementedError("meta rows through a MoE layer before the last")
    return h.reshape(b, s_len, d)
```

```python
import functools

import jax
import jax.numpy as jnp
import numpy as np
from jax import lax
from jax.experimental import pallas as pl
from jax.experimental.pallas import tpu as pltpu

F32 = jnp.float32
BF16 = jnp.bfloat16

EPS = 1e-6
LOG2E = 1.4426950408889634
TOP_K = 2
LANES = 128
MXU_WIDTH = 256
VMEM_LIMIT = 56 * 1024 * 1024
ROW_TILE = 512
COL_CHUNK = 512
FF_CHUNK = 256
ATT_TQ = 256
ATT_TK = 128
ATT_PAIRS = 4
FOX_PAIRS = 4
MOE_TM = 512
CUM_CHUNK = 256
CAST_BLOCK_BYTES = 6 * 1024 * 1024


def _cparams(n_axes):
    return pltpu.CompilerParams(dimension_semantics=("arbitrary",) * n_axes,
                                vmem_limit_bytes=VMEM_LIMIT)


def _split3(x):
    h1 = x.astype(BF16)
    r1 = x - h1.astype(F32)
    h2 = r1.astype(BF16)
    h3 = (r1 - h2.astype(F32)).astype(BF16)
    return h1, h2, h3


def _split2(x):
    h1 = x.astype(BF16)
    h2 = (x - h1.astype(F32)).astype(BF16)
    return h1, h2


def _dot(a, b):
    return jnp.dot(a, b, preferred_element_type=F32)


def _dot_nt(a, b):
    return lax.dot_general(a, b, (((1,), (1,)), ((), ())), preferred_element_type=F32)


def _rms(xf, g):
    ms = jnp.mean(xf * xf, axis=-1, keepdims=True)
    return xf * lax.rsqrt(ms + EPS) * g


def _row_tile(rows, pref=ROW_TILE):
    return pref if rows % pref == 0 else rows


def _col_chunk(*widths):
    c = COL_CHUNK
    while any(w % c for w in widths):
        c //= 2
    assert c >= LANES
    return c


def _cast_kernel(*refs):
    n = len(refs) // 2
    for x_ref, o_ref in zip(refs[:n], refs[n:]):
        o_ref[...] = x_ref[...].astype(o_ref.dtype)


def to_bf16(w, splits=1):
    shape = w.shape
    cols = shape[-1]
    assert cols % (splits * LANES) == 0
    sc = cols // splits
    w2 = w.reshape(-1, cols)
    rows = w2.shape[0]
    rb = rows
    for cand in (2048, 1024, 512, 256, 128, 64):
        if rows % cand == 0 and cand * cols * 4 <= CAST_BLOCK_BYTES:
            rb = cand
            break
    spec = lambda s: pl.BlockSpec((rb, sc), functools.partial(lambda i, s: (i, s), s=s))
    outs = pl.pallas_call(
        _cast_kernel,
        grid=(rows // rb,),
        in_specs=[spec(s) for s in range(splits)],
        out_specs=[pl.BlockSpec((rb, sc), lambda i: (i, 0)) for _ in range(splits)],
        out_shape=[jax.ShapeDtypeStruct((rows, sc), BF16) for _ in range(splits)],
        compiler_params=_cparams(1),
        name="to_bf16",
    )(*([w2] * splits))
    outs = [o.reshape(shape[:-1] + (sc,)) for o in outs]
    return outs[0] if splits == 1 else outs


def _norm_matmul_kernel(*refs, cfg):
    n_p = len(cfg)
    x_ref, ins, outs = refs[0], refs[1:1 + 4 * n_p], refs[1 + 4 * n_p:]
    xf = x_ref[...]
    xs = xf * lax.rsqrt(jnp.mean(xf * xf, axis=-1, keepdims=True) + EPS)
    for pi, (n_chunk, head_cols) in enumerate(cfg):
        g_ref, w_ref, hg_ref, gm_ref = ins[4 * pi:4 * pi + 4]
        o_ref = outs[pi]
        xn = (xs * g_ref[...]).astype(BF16)
        n = w_ref.shape[1]
        for c0 in range(0, n, n_chunk):
            acc = _dot(xn, w_ref[:, c0:c0 + n_chunk])
            if c0 < head_cols:
                sq = (acc * acc).astype(BF16)
                gw = gm_ref.shape[0]
                ms = jnp.concatenate([_dot(sq[:, j:j + gw], gm_ref[...])
                                      for j in range(0, n_chunk, gw)], axis=1)
                acc = acc * lax.rsqrt(ms + EPS) * hg_ref[:, c0:c0 + n_chunk]
            o_ref[:, c0:c0 + n_chunk] = acc.astype(o_ref.dtype)


def norm_projections(x, projs):
    rows, d = x.shape
    tm = _row_tile(rows)
    const = lambda shape: pl.BlockSpec(shape, lambda i: (0,) * len(shape))
    cfg, args, in_specs, out_specs, out_shape = [], [x], [pl.BlockSpec((tm, d), lambda i: (i, 0))], [], []
    for g, w, out_dtype, head_gain, head_dim in projs:
        n = w.shape[1]
        if head_gain is None:
            n_chunk = _col_chunk(n)
            head_cols = 0
            head_gain = jnp.zeros((1, n_chunk), F32)
            head_dim = min(n_chunk, MXU_WIDTH)
        else:
            head_cols = head_gain.shape[1]
            n_chunk = _col_chunk(n, head_cols)
            assert n_chunk % head_dim == 0
        gw = min(n_chunk, MXU_WIDTH)
        assert gw % head_dim == 0 and n_chunk % gw == 0
        grp = jnp.arange(gw) // head_dim
        gm = ((grp[:, None] == grp[None, :]).astype(F32) / head_dim).astype(BF16)
        cfg.append((n_chunk, head_cols))
        args += [g, w, head_gain, gm]
        in_specs += [const((1, d)), const((d, n)), const(head_gain.shape), const((gw, gw))]
        out_specs.append(pl.BlockSpec((tm, n), lambda i: (i, 0)))
        out_shape.append(jax.ShapeDtypeStruct((rows, n), out_dtype))
    return pl.pallas_call(
        functools.partial(_norm_matmul_kernel, cfg=tuple(cfg)),
        grid=(rows // tm,),
        in_specs=in_specs,
        out_specs=out_specs,
        out_shape=out_shape,
        compiler_params=_cparams(1),
        name="norm_matmul",
    )(*args)


def norm_matmul(x, g, w, out_dtype, *, head_gain=None, head_dim=None):
    return norm_projections(x, [(g, w, out_dtype, head_gain, head_dim)])[0]


def _matmul_res_kernel(a_ref, w_ref, r_ref, o_ref, *, n_chunk):
    a = a_ref[...]
    n = w_ref.shape[1]
    for c0 in range(0, n, n_chunk):
        o_ref[:, c0:c0 + n_chunk] = r_ref[:, c0:c0 + n_chunk] + _dot(a, w_ref[:, c0:c0 + n_chunk])


def matmul_residual(a, w, res):
    rows, k = a.shape
    n = w.shape[1]
    tm = _row_tile(rows)
    n_chunk = _col_chunk(n)
    return pl.pallas_call(
        functools.partial(_matmul_res_kernel, n_chunk=n_chunk),
        grid=(rows // tm,),
        in_specs=[
            pl.BlockSpec((tm, k), lambda i: (i, 0)),
            pl.BlockSpec((k, n), lambda i: (0, 0)),
            pl.BlockSpec((tm, n), lambda i: (i, 0)),
        ],
        out_specs=pl.BlockSpec((tm, n), lambda i: (i, 0)),
        out_shape=jax.ShapeDtypeStruct((rows, n), F32),
        compiler_params=_cparams(1),
        name="matmul_residual",
    )(a, w, res)


def _swiglu_acc(xn, wg_ref, wu_ref, wda_ref, wdb_ref, f_chunk, widx=()):
    ff = wg_ref.shape[-1]
    acc_a = acc_b = None
    for c0 in range(0, ff, f_chunk):
        cols = widx + (slice(None), slice(c0, c0 + f_chunk))
        rows = widx + (slice(c0, c0 + f_chunk), slice(None))
        gt = _dot(xn, wg_ref[cols])
        up = _dot(xn, wu_ref[cols])
        act = (gt * (1.0 / (1.0 + jnp.exp(-gt))) * up).astype(BF16)
        pa, pb = _dot(act, wda_ref[rows]), _dot(act, wdb_ref[rows])
        acc_a = pa if acc_a is None else acc_a + pa
        acc_b = pb if acc_b is None else acc_b + pb
    return acc_a, acc_b


def _dense_ffn_kernel(x_ref, g_ref, wg_ref, wu_ref, wda_ref, wdb_ref, o_ref, *, f_chunk):
    xf = x_ref[...]
    xn = _rms(xf, g_ref[...]).astype(BF16)
    acc_a, acc_b = _swiglu_acc(xn, wg_ref, wu_ref, wda_ref, wdb_ref, f_chunk)
    half = acc_a.shape[1]
    o_ref[:, :half] = xf[:, :half] + acc_a
    o_ref[:, half:] = xf[:, half:] + acc_b


def dense_ffn(x, g, wg, wu, wda, wdb):
    rows, d = x.shape
    ff = wg.shape[1]
    tm = _row_tile(rows)
    const = lambda shape: pl.BlockSpec(shape, lambda i: (0, 0))
    return pl.pallas_call(
        functools.partial(_dense_ffn_kernel, f_chunk=FF_CHUNK),
        grid=(rows // tm,),
        in_specs=[
            pl.BlockSpec((tm, d), lambda i: (i, 0)),
            const((1, d)), const((d, ff)), const((d, ff)), const((ff, d // 2)), const((ff, d // 2)),
        ],
        out_specs=pl.BlockSpec((tm, d), lambda i: (i, 0)),
        out_shape=jax.ShapeDtypeStruct((rows, d), F32),
        compiler_params=_cparams(1),
        name="dense_ffn",
    )(x, g, wg, wu, wda, wdb)


def _sb_suffix(z, mask, u_ref):
    zb = z.astype(BF16)
    one, zero = jnp.asarray(1.0, BF16), jnp.asarray(0.0, BF16)
    sp = jnp.maximum(zb, zero) + jnp.log(one + jnp.exp2(jnp.abs(zb) * jnp.asarray(-LOG2E, BF16)))
    if mask is not None:
        sp = jnp.where(mask, sp, zero)
    return _dot(sp, u_ref[...])


def _sb_kernel(*refs, tq, tk, hd, n_meta, scale):
    z_scr, cs_scr, run_scr, acc_scr = refs[-4:]
    if n_meta:
        q_ref, k_ref, v_ref, km_ref, vm_ref, u_ref, o_ref = refs[:-4]
    else:
        q_ref, k_ref, v_ref, u_ref, o_ref = refs[:-4]
    s_len = q_ref.shape[1]
    n_hp = q_ref.shape[2] // LANES
    psl = lambda p: slice(p * LANES, (p + 1) * LANES)
    n_q = s_len // tq
    n_diag = tq // tk
    has_meta = 1 if n_meta else 0
    lane = lax.broadcasted_iota(jnp.int32, (tq, LANES), 1)
    head0 = lane < hd
    row_in = lax.broadcasted_iota(jnp.int32, (tq, tk), 0)
    col_in = lax.broadcasted_iota(jnp.int32, (tq, tk), 1)
    META = "meta"

    def q_tile(i, n_real, static_tiles):
        r0 = pl.multiple_of(i * tq, tq)
        qh = []
        for p in range(n_hp):
            q = q_ref[0, pl.ds(r0, tq), psl(p)] * scale
            qh += [jnp.where(head0, q, jnp.zeros_like(q)), jnp.where(head0, jnp.zeros_like(q), q)]

        def offset(n):
            return pl.multiple_of(r0 + (n_diag - 1 - n) * tk, tk)

        def mask_of(n):
            if n is META:
                return col_in < n_meta
            if isinstance(n, int) and n < n_diag:
                return (col_in + (n_diag - 1 - n) * tk) < row_in
            return None

        def load(n, ref, mref):
            return mref[...] if n is META else ref[0, pl.ds(offset(n), tk), :]

        def step(par, t1, t2, t3):
            if t3 is not None:
                v, m = load(t3, v_ref, vm_ref if n_meta else None), mask_of(t3)
                for h in range(2 * n_hp):
                    g = run_scr[h] + cs_scr[h]
                    a = jnp.exp(z_scr[par, h] + g)
                    if m is not None:
                        a = jnp.where(m, a, 0.0)
                    run_scr[h] = jnp.broadcast_to(g[:, 0:1], g.shape)
                    acc_scr[h] += _dot(a.astype(BF16), v[:, psl(h // 2)])
            if t2 is not None:
                m = mask_of(t2)
                for h in range(2 * n_hp):
                    cs_scr[h] = _sb_suffix(z_scr[1 - par, h], m, u_ref)
            if t1 is not None:
                k = load(t1, k_ref, km_ref if n_meta else None)
                for h in range(2 * n_hp):
                    z_scr[par, h] = _dot_nt(qh[h], k[:, psl(h // 2)])

        run_scr[...] = jnp.zeros_like(run_scr)
        acc_scr[...] = jnp.zeros_like(acc_scr)

        if static_tiles:
            tiles = list(range(n_real)) + ([META] if has_meta else [])
            for s in range(len(tiles) + 2):
                pick = lambda j: tiles[j] if 0 <= j < len(tiles) else None
                step(s % 2, pick(s), pick(s - 1), pick(s - 2))
        else:
            n_pro = n_diag + 2
            assert n_pro % 2 == 0
            for s in range(n_pro):
                pick = lambda j: j if j >= 0 else None
                step(s % 2, pick(s), pick(s - 1), pick(s - 2))

            def body(j, carry):
                s = n_pro + 2 * j
                step(0, s, s - 1, s - 2)
                step(1, s + 1, s, s - 1)
                return carry

            lax.fori_loop(0, (n_real - n_pro) // 2, body, 0)
            last = n_real - 1
            step(0, META if has_meta else None, last, last - 1)
            step(1, None, META if has_meta else None, last)
            if has_meta:
                step(0, None, None, META)

        for p in range(n_hp):
            o_ref[0, pl.ds(r0, tq), psl(p)] = jnp.where(
                head0, acc_scr[2 * p], acc_scr[2 * p + 1]).astype(o_ref.dtype)

    q_tile(0, n_diag, True)
    if n_q > 1:
        assert n_diag >= 2, "the pipelined prologue needs two diagonal key tiles per query tile"

        def outer(i, carry):
            q_tile(i, n_diag * (i + 1), False)
            return carry

        lax.fori_loop(1, n_q, outer, 0)


def _suffix_matrix(t):
    r = jnp.arange(t)
    return -(r[:, None] >= r[None, :]).astype(BF16)


def stickbreak_attention(qkv, qkv_meta, d_model, hd):
    b, s_len, _ = qkv.shape
    n_pair = d_model // LANES
    tq = min(ATT_TQ, s_len)
    tk = min(ATT_TK, s_len)
    n_meta = 0 if qkv_meta is None else qkv_meta.shape[0]
    n_hp = ATT_PAIRS if n_pair % ATT_PAIRS == 0 else 1
    width = n_hp * LANES
    n_grp = n_pair // n_hp
    col = lambda which: (lambda bi, hp: (bi, 0, which * n_grp + hp))
    in_specs = [pl.BlockSpec((1, s_len, width), col(0)),
                pl.BlockSpec((1, s_len, width), col(1)),
                pl.BlockSpec((1, s_len, width), col(2))]
    args = [qkv, qkv, qkv]
    if n_meta:
        assert n_meta <= tk
        meta_pad = jnp.zeros((tk, qkv_meta.shape[1]), qkv_meta.dtype).at[:n_meta].set(qkv_meta)
        in_specs += [pl.BlockSpec((tk, width), lambda bi, hp: (0, n_grp + hp)),
                     pl.BlockSpec((tk, width), lambda bi, hp: (0, 2 * n_grp + hp))]
        args += [meta_pad, meta_pad]
    in_specs.append(pl.BlockSpec((tk, tk), lambda bi, hp: (0, 0)))
    args.append(_suffix_matrix(tk))
    kern = functools.partial(_sb_kernel, tq=tq, tk=tk, hd=hd, n_meta=n_meta, scale=hd ** -0.5)
    nh = 2 * n_hp
    return pl.pallas_call(
        kern,
        grid=(b, n_grp),
        in_specs=in_specs,
        out_specs=pl.BlockSpec((1, s_len, width), lambda bi, hp: (bi, 0, hp)),
        out_shape=jax.ShapeDtypeStruct((b, s_len, d_model), BF16),
        scratch_shapes=[pltpu.VMEM((2, nh, tq, tk), F32), pltpu.VMEM((nh, tq, tk), F32),
                        pltpu.VMEM((nh, tq, tk), F32), pltpu.VMEM((nh, tq, LANES), F32)],
        compiler_params=_cparams(2),
        name="stickbreak_attention",
    )(*args)


def _log_sigmoid(y):
    return jnp.minimum(y, 0.0) - jnp.log(1.0 + jnp.exp(-jnp.abs(y)))


def _tri_cumsum(tri, x):
    h1, h2, h3 = _split3(x)
    return _dot(tri, h1) + _dot(tri, h2) + _dot(tri, h3)


def _forget_cumsum_kernel(f_ref, fm_ref, b_ref, tri_ref, trim_ref, o_ref, om_ref, *, chunk, n_heads):
    bias = b_ref[...]

    def packed(f):
        lane = lax.broadcasted_iota(jnp.int32, f.shape, 1)
        out = None
        for a, part in enumerate(_split3(f)):
            pa = jnp.where(lane < n_heads, part.astype(F32), 0.0)
            pa = pa if a == 0 else pltpu.roll(pa, a * n_heads, 1)
            out = pa if out is None else out + pa
        return out.astype(BF16)

    fm = _tri_cumsum(trim_ref[...], _log_sigmoid(fm_ref[...] + bias))
    om_ref[...] = packed(fm)
    n_meta = fm.shape[0]
    carry = fm[n_meta - 1:n_meta, :]
    s_len = f_ref.shape[1]
    for c0 in range(0, s_len, chunk):
        fc = _tri_cumsum(tri_ref[...], _log_sigmoid(f_ref[0, c0:c0 + chunk, :] + bias)) + carry
        o_ref[0, c0:c0 + chunk, :] = packed(fc)
        carry = fc[chunk - 1:chunk, :]


def forget_cumsum(f_logit, f_logit_meta, bias, n_heads):
    b, s_len, w = f_logit.shape
    n_meta = f_logit_meta.shape[0]
    assert 3 * n_heads <= w
    chunk = min(CUM_CHUNK, s_len)
    tri = lambda t: (jnp.arange(t)[:, None] >= jnp.arange(t)[None, :]).astype(BF16)
    return pl.pallas_call(
        functools.partial(_forget_cumsum_kernel, chunk=chunk, n_heads=n_heads),
        grid=(b,),
        in_specs=[
            pl.BlockSpec((1, s_len, w), lambda bi: (bi, 0, 0)),
            pl.BlockSpec((n_meta, w), lambda bi: (0, 0)),
            pl.BlockSpec((1, w), lambda bi: (0, 0)),
            pl.BlockSpec((chunk, chunk), lambda bi: (0, 0)),
            pl.BlockSpec((n_meta, n_meta), lambda bi: (0, 0)),
        ],
        out_specs=[pl.BlockSpec((1, s_len, w), lambda bi: (bi, 0, 0)),
                   pl.BlockSpec((n_meta, w), lambda bi: (0, 0))],
        out_shape=[jax.ShapeDtypeStruct((b, s_len, w), BF16),
                   jax.ShapeDtypeStruct((n_meta, w), BF16)],
        compiler_params=_cparams(1),
        name="forget_cumsum",
    )(f_logit, f_logit_meta, bias, tri(chunk), tri(n_meta))


def _fox_kernel(qc_ref, kc_ref, vc_ref, f_ref, kmc_ref, vmc_ref, fm_ref, pq_ref, pk_ref, pv_ref,
                cq_ref, ck_ref, cv_ref, o_ref,
                q_ref, k_ref, v_ref, km_ref, vm_ref, z_scr, p_scr, m_scr, alpha_scr, acc_scr,
                *, tq, tk, hd, n_meta):
    s_len = qc_ref.shape[1]
    n_q = s_len // tq
    n_diag = tq // tk
    lane = lax.broadcasted_iota(jnp.int32, (tq, LANES), 1)
    head0 = lane < hd
    row_in = lax.broadcasted_iota(jnp.int32, (tq, tk), 0)
    col_in = lax.broadcasted_iota(jnp.int32, (tq, tk), 1)
    META = "meta"
    n_hp = qc_ref.shape[2] // LANES
    n_h = 2 * n_hp
    hsl = lambda h: slice(h * LANES, (h + 1) * LANES)
    psl = lambda p: slice(p * LANES, (p + 1) * LANES)
    xsl = lambda p: slice(2 * p * LANES, 2 * (p + 1) * LANES)

    def place(main, fpacked, p_mat, c_ref):
        lhs = main if fpacked is None else jnp.concatenate([main, fpacked], axis=1)
        return (_dot(lhs, p_mat) + c_ref[...]).astype(BF16)

    fpm = fm_ref[...]
    for p in range(n_hp):
        for c0 in range(0, s_len, tq):
            rows = slice(c0, c0 + tq)
            fp = f_ref[0, rows, :]
            q_ref[rows, xsl(p)] = place(qc_ref[0, rows, psl(p)], fp, pq_ref[p], cq_ref)
            k_ref[rows, xsl(p)] = place(kc_ref[0, rows, psl(p)], fp, pk_ref[p], ck_ref)
            v_ref[rows, xsl(p)] = place(vc_ref[0, rows, psl(p)], None, pv_ref[0], cv_ref)
        km_ref[:, xsl(p)] = place(kmc_ref[:, psl(p)], fpm, pk_ref[p], ck_ref)
        vm_ref[:, xsl(p)] = place(vmc_ref[:, psl(p)], None, pv_ref[0], cv_ref)

    def q_tile(i, n_real, static_tiles):
        r0 = pl.multiple_of(i * tq, tq)
        qh = [q_ref[pl.ds(r0, tq), hsl(h)] for h in range(n_h)]

        def mask_of(t):
            if t is META:
                return col_in < n_meta
            return None if t[1] is None else (col_in + t[1] * tk) <= row_in

        def load(t, ref, mref, h):
            if t is META:
                return mref[:, hsl(h)]
            return ref[pl.ds(pl.multiple_of(t[0] * tk, tk), tk), hsl(h)]

        def step(par, t1, t2, t3):
            if t3 is not None:
                for h in range(n_h):
                    acc_scr[h] = acc_scr[h] * alpha_scr[h] + _dot(p_scr[h], load(t3, v_ref, vm_ref, h))
            if t2 is not None:
                m = mask_of(t2)
                for h in range(n_h):
                    z = z_scr[1 - par, h]
                    if m is not None:
                        z = jnp.where(m, z, -jnp.inf)
                    m_old = m_scr[h]
                    m_new = jnp.maximum(m_old, jnp.max(z, axis=-1, keepdims=True))
                    p_scr[h] = jnp.exp(z - m_new).astype(BF16)
                    alpha_scr[h] = jnp.exp(m_old - m_new)[:, :LANES]
                    m_scr[h] = m_new
            if t1 is not None:
                for h in range(n_h):
                    z_scr[par, h] = _dot_nt(qh[h], load(t1, k_ref, km_ref, h))

        m_scr[...] = jnp.full(m_scr.shape, -jnp.inf, F32)
        acc_scr[...] = jnp.zeros_like(acc_scr)

        if static_tiles:
            tiles = [META] + [(n, n - (n_real - n_diag) if n >= n_real - n_diag else None)
                              for n in range(n_real)]
            for s in range(len(tiles) + 2):
                pick = lambda j: tiles[j] if 0 <= j < len(tiles) else None
                step(s % 2, pick(s), pick(s - 1), pick(s - 2))
        else:
            pos = lambda p: META if (isinstance(p, int) and p == 0) else (p - 1, None)
            n_pro = 4
            for s in range(n_pro):
                pick = lambda j: pos(j) if j >= 0 else None
                step(s % 2, pick(s), pick(s - 1), pick(s - 2))

            def body(j, carry):
                s = n_pro + 2 * j
                step(0, pos(s), pos(s - 1), pos(s - 2))
                step(1, pos(s + 1), pos(s), pos(s - 1))
                return carry

            lax.fori_loop(0, (n_real - n_pro) // 2, body, 0)
            a, b = n_real - 2, n_real - 1
            step(0, (b, None), (a, 0), (a - 1, None))
            step(1, None, (b, 1), (a, None))
            step(0, None, None, (b, None))

        for p in range(n_hp):
            res0, res1 = acc_scr[2 * p], acc_scr[2 * p + 1]
            rot0 = pltpu.roll(res0, hd, 1)
            rot1 = pltpu.roll(res1, hd, 1)
            o_ref[0, pl.ds(r0, tq), psl(p)] = jnp.where(
                head0, res0 / rot0, rot1 / res1).astype(o_ref.dtype)

    q_tile(0, n_diag, True)
    if n_q > 1:
        assert n_diag == 2, "the pipelined sweep is written for two diagonal key tiles per query tile"

        def outer(i, carry):
            q_tile(i, n_diag * (i + 1), False)
            return carry

        lax.fori_loop(1, n_q, outer, 0)


def _fox_placement(n_pair, hd):
    n_heads = 2 * n_pair
    pq = np.zeros((n_pair, 2 * LANES, 2 * LANES), np.float32)
    pk = np.zeros((n_pair, 2 * LANES, 2 * LANES), np.float32)
    pv = np.zeros((1, LANES, 2 * LANES), np.float32)
    cq = np.zeros((1, 2 * LANES), np.float32)
    ck = np.zeros((1, 2 * LANES), np.float32)
    cv = np.zeros((1, 2 * LANES), np.float32)
    for h in range(2):
        base = h * LANES
        for j in range(hd):
            pq[:, h * hd + j, base + j] = hd ** -0.5
            pk[:, h * hd + j, base + j] = 1.0
            pv[:, h * hd + j, base + j] = 1.0
        for p in range(3):
            for hp in range(n_pair):
                pq[hp, LANES + p * n_heads + 2 * hp + h, base + hd + p] = 1.0
                pk[hp, LANES + p * n_heads + 2 * hp + h, base + hd + 3 + p] = -1.0
        cq[0, base + hd + 3:base + hd + 6] = 1.0
        ck[0, base + hd:base + hd + 3] = 1.0
        cv[0, base + hd:base + LANES] = 1.0
    bf = lambda a: jnp.asarray(a, BF16)
    return bf(pq), bf(pk), bf(pv), jnp.asarray(cq), jnp.asarray(ck), jnp.asarray(cv)


def forgetting_attention(q, kv, kv_meta, fparts, fparts_meta, d_model, hd):
    b, s_len, _ = q.shape
    n_pair = d_model // LANES
    n_meta = kv_meta.shape[0]
    tq = min(ATT_TQ, s_len)
    tk = min(ATT_TK, s_len)
    assert n_meta <= tk and tk == LANES and 3 * 2 * n_pair <= LANES
    kvm = jnp.zeros((tk, kv_meta.shape[1]), kv_meta.dtype).at[:n_meta].set(kv_meta)
    fpm = jnp.zeros((tk, LANES), fparts_meta.dtype).at[:n_meta].set(fparts_meta)
    pq, pk, pv, cq, ck, cv = _fox_placement(n_pair, hd)
    n_hp = FOX_PAIRS if n_pair % FOX_PAIRS == 0 else 1
    width = n_hp * LANES
    n_grp = n_pair // n_hp
    nh = 2 * n_hp
    blk = lambda off: pl.BlockSpec((1, s_len, width), lambda bi, hp: (bi, 0, off + hp))
    mblk = lambda off: pl.BlockSpec((tk, width), lambda bi, hp: (0, off + hp))
    crow = pl.BlockSpec((1, 2 * LANES), lambda bi, hp: (0, 0))
    in_specs = [blk(0), blk(0), blk(n_grp),
                pl.BlockSpec((1, s_len, LANES), lambda bi, hp: (bi, 0, 0)),
                mblk(0), mblk(n_grp),
                pl.BlockSpec((tk, LANES), lambda bi, hp: (0, 0)),
                pl.BlockSpec((n_hp, 2 * LANES, 2 * LANES), lambda bi, hp: (hp, 0, 0)),
                pl.BlockSpec((n_hp, 2 * LANES, 2 * LANES), lambda bi, hp: (hp, 0, 0)),
                pl.BlockSpec((1, LANES, 2 * LANES), lambda bi, hp: (0, 0, 0)),
                crow, crow, crow]
    ext = lambda rows: pltpu.VMEM((rows, nh * LANES), BF16)
    return pl.pallas_call(
        functools.partial(_fox_kernel, tq=tq, tk=tk, hd=hd, n_meta=n_meta),
        grid=(b, n_grp),
        in_specs=in_specs,
        out_specs=pl.BlockSpec((1, s_len, width), lambda bi, hp: (bi, 0, hp)),
        out_shape=jax.ShapeDtypeStruct((b, s_len, d_model), BF16),
        scratch_shapes=[ext(s_len), ext(s_len), ext(s_len), ext(tk), ext(tk),
                        pltpu.VMEM((2, nh, tq, tk), F32), pltpu.VMEM((nh, tq, tk), BF16),
                        pltpu.VMEM((nh, tq, tk), F32), pltpu.VMEM((nh, tq, LANES), F32),
                        pltpu.VMEM((nh, tq, LANES), F32)],
        compiler_params=_cparams(2),
        name="forgetting_attention",
    )(q, kv, kv, fparts, kvm, kvm, fpm, pq, pk, pv, cq, ck, cv)


def _router_kernel(a_ref, wo_ref, r_ref, g_ref, wr_ref, h_ref, xn_ref, id_ref, g1_ref, g2_ref,
                   *, n_experts, n_chunk):
    a = a_ref[...]
    for c0 in range(0, wo_ref.shape[1], n_chunk):
        h_ref[:, c0:c0 + n_chunk] = r_ref[:, c0:c0 + n_chunk] + _dot(a, wo_ref[:, c0:c0 + n_chunk])
    xn = _rms(h_ref[...], g_ref[...])
    xn_ref[...] = xn.astype(BF16)
    a1, a2 = _split2(xn)
    b1, b2 = _split2(wr_ref[...])
    logits = _dot(a1, b1) + _dot(a1, b2) + _dot(a2, b1)
    lane = lax.broadcasted_iota(jnp.int32, logits.shape, 1)
    lg = jnp.where(lane < n_experts, logits, -jnp.inf)
    m1 = jnp.max(lg, axis=-1, keepdims=True)
    i1 = jnp.min(jnp.where(lg == m1, lane, LANES), axis=-1, keepdims=True)
    lg2 = jnp.where(lane == i1, -jnp.inf, lg)
    m2 = jnp.max(lg2, axis=-1, keepdims=True)
    i2 = jnp.min(jnp.where(lg2 == m2, lane, LANES), axis=-1, keepdims=True)
    e2 = jnp.exp(m2 - m1)
    den = 1.0 + e2
    id_ref[...] = jnp.where(lane == 0, i1, jnp.where(lane == 1, i2, 0))
    g1_ref[...] = jnp.broadcast_to(1.0 / den, logits.shape)
    g2_ref[...] = jnp.broadcast_to(e2 / den, logits.shape)


def moe_router(a, w_o, res, g, w_router):
    rows, d = res.shape
    k = a.shape[1]
    n_experts = w_router.shape[1]
    wr = jnp.zeros((d, LANES), F32).at[:, :n_experts].set(w_router)
    tm = _row_tile(rows)
    row_blk = lambda w: pl.BlockSpec((tm, w), lambda i: (i, 0))
    return pl.pallas_call(
        functools.partial(_router_kernel, n_experts=n_experts, n_chunk=_col_chunk(d)),
        grid=(rows // tm,),
        in_specs=[
            row_blk(k),
            pl.BlockSpec((k, d), lambda i: (0, 0)),
            row_blk(d),
            pl.BlockSpec((1, d), lambda i: (0, 0)),
            pl.BlockSpec((d, LANES), lambda i: (0, 0)),
        ],
        out_specs=[row_blk(d), row_blk(d), row_blk(LANES), row_blk(LANES), row_blk(LANES)],
        out_shape=[jax.ShapeDtypeStruct((rows, d), F32),
                   jax.ShapeDtypeStruct((rows, d), BF16),
                   jax.ShapeDtypeStruct((rows, LANES), jnp.int32),
                   jax.ShapeDtypeStruct((rows, LANES), F32),
                   jax.ShapeDtypeStruct((rows, LANES), F32)],
        compiler_params=_cparams(1),
        name="moe_router",
    )(a, w_o, res, g, wr)


def _moe_ffn_kernel(te_ref, nt_ref, x_ref, wg_ref, wu_ref, wda_ref, wdb_ref, o_ref, *, f_chunk):
    i = pl.program_id(0)

    @pl.when(i < nt_ref[0])
    def _():
        acc_a, acc_b = _swiglu_acc(x_ref[...], wg_ref, wu_ref, wda_ref, wdb_ref, f_chunk, widx=(0,))
        half = acc_a.shape[1]
        o_ref[:, :half] = acc_a.astype(o_ref.dtype)
        o_ref[:, half:] = acc_b.astype(o_ref.dtype)

    @pl.when(i >= nt_ref[0])
    def _():
        o_ref[...] = jnp.zeros_like(o_ref)


def moe_grouped_ffn(xs, tile_expert, n_tiles_used, wg, wu, wda, wdb):
    p_rows, d = xs.shape
    n_experts, _, ff = wg.shape
    tm = MOE_TM
    expert = lambda shape: pl.BlockSpec(shape, lambda i, te, nt: (te[i], 0, 0))
    grid_spec = pltpu.PrefetchScalarGridSpec(
        num_scalar_prefetch=2,
        grid=(p_rows // tm,),
        in_specs=[
            pl.BlockSpec((tm, d), lambda i, te, nt: (i, 0)),
            expert((1, d, ff)), expert((1, d, ff)), expert((1, ff, d // 2)), expert((1, ff, d // 2)),
        ],
        out_specs=pl.BlockSpec((tm, d), lambda i, te, nt: (i, 0)),
    )
    return pl.pallas_call(
        functools.partial(_moe_ffn_kernel, f_chunk=FF_CHUNK),
        grid_spec=grid_spec,
        out_shape=jax.ShapeDtypeStruct((p_rows, d), BF16),
        compiler_params=_cparams(1),
        name="moe_grouped_ffn",
    )(tile_expert, n_tiles_used, xs, wg, wu, wda, wdb)


def _combine_kernel(h_ref, y1_ref, y2_ref, g1_ref, g2_ref, o_ref):
    d = h_ref.shape[1]
    g1, g2 = g1_ref[...], g2_ref[...]
    for c0 in range(0, d, LANES):
        cs = slice(c0, c0 + LANES)
        o_ref[:, cs] = (h_ref[:, cs] + g1 * y1_ref[:, cs].astype(F32)
                        + g2 * y2_ref[:, cs].astype(F32))


def moe_block(a, w_o, res, g, w_router, ffn_weights):
    rows, d = res.shape
    n_experts = w_router.shape[1]
    tm = MOE_TM
    h, xn, ids, g1, g2 = moe_router(a, w_o, res, g, w_router)

    eid = ids[:, :TOP_K]
    sel = (eid[:, :, None] == jnp.arange(n_experts)[None, None, :]).any(axis=1)
    rank = jnp.cumsum(sel.astype(jnp.int32), axis=0) - 1
    cnt = rank[-1] + 1
    cnt_pad = ((cnt + tm - 1) // tm) * tm
    ends = jnp.cumsum(cnt_pad)
    offs = ends - cnt_pad
    pos = offs[eid] + jnp.take_along_axis(rank, eid, axis=1)
    p_rows = rows * TOP_K + n_experts * tm
    n_tiles = p_rows // tm
    tile_start = jnp.arange(n_tiles, dtype=jnp.int32) * tm
    tile_expert = jnp.minimum((tile_start[:, None] >= ends[None, :]).sum(axis=1), n_experts - 1)
    n_used = (ends[-1] // tm).astype(jnp.int32).reshape(1)

    tok = jnp.arange(rows, dtype=jnp.int32)[:, None]
    tok_c = jnp.sort((eid * rows + tok).reshape(-1)) % rows
    tok_c = jnp.concatenate([tok_c, jnp.zeros((p_rows - rows * TOP_K,), jnp.int32)])
    cstart = jnp.cumsum(cnt) - cnt
    slot = jnp.arange(p_rows, dtype=jnp.int32)
    tok_sorted = jnp.zeros((p_rows,), jnp.int32)
    for e in range(n_experts):
        inside = (slot >= offs[e]) & (slot < offs[e] + cnt[e])
        tok_sorted = jnp.where(inside, jnp.roll(tok_c, offs[e] - cstart[e]), tok_sorted)
    xs = xn.at[tok_sorted].get(mode="promise_in_bounds")
    ys = moe_grouped_ffn(xs, tile_expert.astype(jnp.int32), n_used, *ffn_weights)
    y1 = ys.at[pos[:, 0]].get(mode="promise_in_bounds", unique_indices=True)
    y2 = ys.at[pos[:, 1]].get(mode="promise_in_bounds", unique_indices=True)
    tr = _row_tile(rows)
    spec = pl.BlockSpec((tr, d), lambda i: (i, 0))
    gspec = pl.BlockSpec((tr, LANES), lambda i: (i, 0))
    return pl.pallas_call(
        _combine_kernel,
        grid=(rows // tr,),
        in_specs=[spec, spec, spec, gspec, gspec],
        out_specs=spec,
        out_shape=jax.ShapeDtypeStruct((rows, d), F32),
        compiler_params=_cparams(1),
        name="moe_combine",
    )(h, y1, y2, g1, g2)


def kernel(x, meta_tokens, norm_attn_a, w_qkv_a, w_o_a, norm_kv, w_kvf, b_f, k_norm, norm_attn_b,
           w_q_b, q_norm_b, w_o_b, norm_ffn_dense, w_gu_dense, w_down_dense, norm_ffn_moe,
           w_router, w_gu_moe, w_down_moe):
    b, s_len, d = x.shape
    n_heads = b_f.shape[0]
    hd = d // n_heads
    n_meta = meta_tokens.shape[0]
    depth = norm_attn_a.shape[0] + norm_attn_b.shape[0]
    n_a = norm_attn_a.shape[0]
    assert 2 * hd == LANES and d % LANES == 0
    row = lambda v: v.reshape(1, -1).astype(F32)

    h = x.reshape(b * s_len, d)
    hm = meta_tokens.astype(x.dtype)
    for layer in range(depth):
        last = layer == depth - 1
        if layer < n_a:
            w_qkv = to_bf16(w_qkv_a[layer])
            g = row(norm_attn_a[layer])
            qkv = norm_matmul(h, g, w_qkv, BF16).reshape(b, s_len, 3 * d)
            qkv_m = norm_matmul(hm, g, w_qkv, BF16)
            o = stickbreak_attention(qkv, qkv_m, d, hd).reshape(b * s_len, d)
            o_m = stickbreak_attention(qkv_m[None], None, d, hd)[0]
            w_o = to_bf16(w_o_a[layer])
            hm = matmul_residual(o_m, w_o, hm)
        else:
            if layer == n_a:
                g = row(norm_kv)
                w_kv = to_bf16(w_kvf[:, :2 * d])
                w_f = jnp.zeros((d, LANES), F32).at[:, :n_heads].set(w_kvf[:, 2 * d:]).astype(BF16)
                kgain = row(jnp.tile(k_norm, n_heads))
                shared = [(g, w_kv, BF16, kgain, hd), (g, w_f, F32, None, None)]
            i = layer - n_a
            q_proj = (row(norm_attn_b[i]), to_bf16(w_q_b[i]), BF16,
                      row(jnp.tile(q_norm_b[i], n_heads)), hd)
            if layer == n_a:
                kv, fl, q = norm_projections(h, shared + [q_proj])
                kv_m, fl_m = norm_projections(hm, shared)
                bias = jnp.zeros((1, LANES), F32).at[0, :n_heads].set(b_f.astype(F32))
                f_cum, f_cum_m = forget_cumsum(fl.reshape(b, s_len, LANES), fl_m, bias, n_heads)
                kv_sh = kv.reshape(b, s_len, 2 * d)
            else:
                q, = norm_projections(h, [q_proj])
            o = forgetting_attention(q.reshape(b, s_len, d), kv_sh, kv_m, f_cum, f_cum_m, d, hd)
            o = o.reshape(b * s_len, d)
            w_o = to_bf16(w_o_b[i])
            if not last:
                raise NotImplementedError("meta-row queries in forgetting layers before the last")
        j = layer // 2
        if layer % 2 == 0:
            ffn_w = to_bf16(w_gu_dense[j], splits=2) + to_bf16(w_down_dense[j], splits=2)
            g = row(norm_ffn_dense[j])
            h = dense_ffn(matmul_residual(o, w_o, h), g, *ffn_w)
            if not last:
                hm = dense_ffn(hm, g, *ffn_w)
        else:
            ffn_w = to_bf16(w_gu_moe[j], splits=2) + to_bf16(w_down_moe[j], splits=2)
            h = moe_block(o, w_o, h, row(norm_ffn_moe[j]), w_router[j], ffn_w)
            if not last:
                raise NotImplementedError("meta rows through a MoE layer before the last")
    return h.reshape(b, s_len, d)
```

```python
import functools

import jax
import jax.numpy as jnp
import numpy as np
from jax import lax
from jax.experimental import pallas as pl
from jax.experimental.pallas import tpu as pltpu

F32 = jnp.float32
BF16 = jnp.bfloat16

EPS = 1e-6
LOG2E = 1.4426950408889634
TOP_K = 2
LANES = 128
MXU_WIDTH = 256
VMEM_LIMIT = 56 * 1024 * 1024
ROW_TILE = 512
COL_CHUNK = 512
FF_CHUNK = 256
ATT_TQ = 256
ATT_TK = 128
ATT_PAIRS = 4
FOX_PAIRS = 4
MOE_TM = 512
CUM_CHUNK = 256
CAST_BLOCK_BYTES = 6 * 1024 * 1024


def _cparams(n_axes):
    return pltpu.CompilerParams(dimension_semantics=("arbitrary",) * n_axes,
                                vmem_limit_bytes=VMEM_LIMIT)


def _split3(x):
    h1 = x.astype(BF16)
    r1 = x - h1.astype(F32)
    h2 = r1.astype(BF16)
    h3 = (r1 - h2.astype(F32)).astype(BF16)
    return h1, h2, h3


def _split2(x):
    h1 = x.astype(BF16)
    h2 = (x - h1.astype(F32)).astype(BF16)
    return h1, h2


def _dot(a, b):
    return jnp.dot(a, b, preferred_element_type=F32)


def _dot_nt(a, b):
    return lax.dot_general(a, b, (((1,), (1,)), ((), ())), preferred_element_type=F32)


def _rms(xf, g):
    ms = jnp.mean(xf * xf, axis=-1, keepdims=True)
    return xf * lax.rsqrt(ms + EPS) * g


def _row_tile(rows, pref=ROW_TILE):
    return pref if rows % pref == 0 else rows


def _col_chunk(*widths):
    c = COL_CHUNK
    while any(w % c for w in widths):
        c //= 2
    assert c >= LANES
    return c


def _cast_kernel(*refs):
    n = len(refs) // 2
    for x_ref, o_ref in zip(refs[:n], refs[n:]):
        o_ref[...] = x_ref[...].astype(o_ref.dtype)


def to_bf16(w, splits=1):
    shape = w.shape
    cols = shape[-1]
    assert cols % (splits * LANES) == 0
    sc = cols // splits
    w2 = w.reshape(-1, cols)
    rows = w2.shape[0]
    rb = rows
    for cand in (2048, 1024, 512, 256, 128, 64):
        if rows % cand == 0 and cand * cols * 4 <= CAST_BLOCK_BYTES:
            rb = cand
            break
    spec = lambda s: pl.BlockSpec((rb, sc), functools.partial(lambda i, s: (i, s), s=s))
    outs = pl.pallas_call(
        _cast_kernel,
        grid=(rows // rb,),
        in_specs=[spec(s) for s in range(splits)],
        out_specs=[pl.BlockSpec((rb, sc), lambda i: (i, 0)) for _ in range(splits)],
        out_shape=[jax.ShapeDtypeStruct((rows, sc), BF16) for _ in range(splits)],
        compiler_params=_cparams(1),
        name="to_bf16",
    )(*([w2] * splits))
    outs = [o.reshape(shape[:-1] + (sc,)) for o in outs]
    return outs[0] if splits == 1 else outs


def _norm_matmul_kernel(*refs, cfg):
    n_p = len(cfg)
    x_ref, ins, outs = refs[0], refs[1:1 + 4 * n_p], refs[1 + 4 * n_p:]
    xf = x_ref[...]
    xs = xf * lax.rsqrt(jnp.mean(xf * xf, axis=-1, keepdims=True) + EPS)
    for pi, (n_chunk, head_cols) in enumerate(cfg):
        g_ref, w_ref, hg_ref, gm_ref = ins[4 * pi:4 * pi + 4]
        o_ref = outs[pi]
        xn = (xs * g_ref[...]).astype(BF16)
        n = w_ref.shape[1]
        for c0 in range(0, n, n_chunk):
            acc = _dot(xn, w_ref[:, c0:c0 + n_chunk])
            if c0 < head_cols:
                sq = (acc * acc).astype(BF16)
                gw = gm_ref.shape[0]
                ms = jnp.concatenate([_dot(sq[:, j:j + gw], gm_ref[...])
                                      for j in range(0, n_chunk, gw)], axis=1)
                acc = acc * lax.rsqrt(ms + EPS) * hg_ref[:, c0:c0 + n_chunk]
            o_ref[:, c0:c0 + n_chunk] = acc.astype(o_ref.dtype)


def norm_projections(x, projs):
    rows, d = x.shape
    tm = _row_tile(rows)
    const = lambda shape: pl.BlockSpec(shape, lambda i: (0,) * len(shape))
    cfg, args, in_specs, out_specs, out_shape = [], [x], [pl.BlockSpec((tm, d), lambda i: (i, 0))], [], []
    for g, w, out_dtype, head_gain, head_dim in projs:
        n = w.shape[1]
        if head_gain is None:
            n_chunk = _col_chunk(n)
            head_cols = 0
            head_gain = jnp.zeros((1, n_chunk), F32)
            head_dim = min(n_chunk, MXU_WIDTH)
        else:
            head_cols = head_gain.shape[1]
            n_chunk = _col_chunk(n, head_cols)
            assert n_chunk % head_dim == 0
        gw = min(n_chunk, MXU_WIDTH)
        assert gw % head_dim == 0 and n_chunk % gw == 0
        grp = jnp.arange(gw) // head_dim
        gm = ((grp[:, None] == grp[None, :]).astype(F32) / head_dim).astype(BF16)
        cfg.append((n_chunk, head_cols))
        args += [g, w, head_gain, gm]
        in_specs += [const((1, d)), const((d, n)), const(head_gain.shape), const((gw, gw))]
        out_specs.append(pl.BlockSpec((tm, n), lambda i: (i, 0)))
        out_shape.append(jax.ShapeDtypeStruct((rows, n), out_dtype))
    return pl.pallas_call(
        functools.partial(_norm_matmul_kernel, cfg=tuple(cfg)),
        grid=(rows // tm,),
        in_specs=in_specs,
        out_specs=out_specs,
        out_shape=out_shape,
        compiler_params=_cparams(1),
        name="norm_matmul",
    )(*args)


def norm_matmul(x, g, w, out_dtype, *, head_gain=None, head_dim=None):
    return norm_projections(x, [(g, w, out_dtype, head_gain, head_dim)])[0]


def _matmul_res_kernel(a_ref, w_ref, r_ref, o_ref, *, n_chunk):
    a = a_ref[...]
    n = w_ref.shape[1]
    for c0 in range(0, n, n_chunk):
        o_ref[:, c0:c0 + n_chunk] = r_ref[:, c0:c0 + n_chunk] + _dot(a, w_ref[:, c0:c0 + n_chunk])


def matmul_residual(a, w, res):
    rows, k = a.shape
    n = w.shape[1]
    tm = _row_tile(rows)
    n_chunk = _col_chunk(n)
    return pl.pallas_call(
        functools.partial(_matmul_res_kernel, n_chunk=n_chunk),
        grid=(rows // tm,),
        in_specs=[
            pl.BlockSpec((tm, k), lambda i: (i, 0)),
            pl.BlockSpec((k, n), lambda i: (0, 0)),
            pl.BlockSpec((tm, n), lambda i: (i, 0)),
        ],
        out_specs=pl.BlockSpec((tm, n), lambda i: (i, 0)),
        out_shape=jax.ShapeDtypeStruct((rows, n), F32),
        compiler_params=_cparams(1),
        name="matmul_residual",
    )(a, w, res)


def _swiglu_acc(xn, wg_ref, wu_ref, wda_ref, wdb_ref, f_chunk, widx=()):
    ff = wg_ref.shape[-1]
    acc_a = acc_b = None
    for c0 in range(0, ff, f_chunk):
        cols = widx + (slice(None), slice(c0, c0 + f_chunk))
        rows = widx + (slice(c0, c0 + f_chunk), slice(None))
        gt = _dot(xn, wg_ref[cols])
        up = _dot(xn, wu_ref[cols])
        act = (gt * (1.0 / (1.0 + jnp.exp(-gt))) * up).astype(BF16)
        pa, pb = _dot(act, wda_ref[rows]), _dot(act, wdb_ref[rows])
        acc_a = pa if acc_a is None else acc_a + pa
        acc_b = pb if acc_b is None else acc_b + pb
    return acc_a, acc_b


def _dense_ffn_kernel(x_ref, g_ref, wg_ref, wu_ref, wda_ref, wdb_ref, o_ref, *, f_chunk):
    xf = x_ref[...]
    xn = _rms(xf, g_ref[...]).astype(BF16)
    acc_a, acc_b = _swiglu_acc(xn, wg_ref, wu_ref, wda_ref, wdb_ref, f_chunk)
    half = acc_a.shape[1]
    o_ref[:, :half] = xf[:, :half] + acc_a
    o_ref[:, half:] = xf[:, half:] + acc_b


def dense_ffn(x, g, wg, wu, wda, wdb):
    rows, d = x.shape
    ff = wg.shape[1]
    tm = _row_tile(rows)
    const = lambda shape: pl.BlockSpec(shape, lambda i: (0, 0))
    return pl.pallas_call(
        functools.partial(_dense_ffn_kernel, f_chunk=FF_CHUNK),
        grid=(rows // tm,),
        in_specs=[
            pl.BlockSpec((tm, d), lambda i: (i, 0)),
            const((1, d)), const((d, ff)), const((d, ff)), const((ff, d // 2)), const((ff, d // 2)),
        ],
        out_specs=pl.BlockSpec((tm, d), lambda i: (i, 0)),
        out_shape=jax.ShapeDtypeStruct((rows, d), F32),
        compiler_params=_cparams(1),
        name="dense_ffn",
    )(x, g, wg, wu, wda, wdb)


def _sb_suffix(z, mask, u_ref):
    zb = z.astype(BF16)
    one, zero = jnp.asarray(1.0, BF16), jnp.asarray(0.0, BF16)
    sp = jnp.maximum(zb, zero) + jnp.log(one + jnp.exp2(jnp.abs(zb) * jnp.asarray(-LOG2E, BF16)))
    if mask is not None:
        sp = jnp.where(mask, sp, zero)
    return _dot(sp, u_ref[...])


def _sb_kernel(*refs, tq, tk, hd, n_meta, scale):
    z_scr, cs_scr, run_scr, acc_scr = refs[-4:]
    if n_meta:
        q_ref, k_ref, v_ref, km_ref, vm_ref, u_ref, o_ref = refs[:-4]
    else:
        q_ref, k_ref, v_ref, u_ref, o_ref = refs[:-4]
    s_len = q_ref.shape[1]
    n_hp = q_ref.shape[2] // LANES
    psl = lambda p: slice(p * LANES, (p + 1) * LANES)
    n_q = s_len // tq
    n_diag = tq // tk
    has_meta = 1 if n_meta else 0
    lane = lax.broadcasted_iota(jnp.int32, (tq, LANES), 1)
    head0 = lane < hd
    row_in = lax.broadcasted_iota(jnp.int32, (tq, tk), 0)
    col_in = lax.broadcasted_iota(jnp.int32, (tq, tk), 1)
    MKEYS = "meta"

    def q_tile(i, n_real, static_tiles):
        r0 = pl.multiple_of(i * tq, tq)
        qh = []
        for p in range(n_hp):
            q = q_ref[0, pl.ds(r0, tq), psl(p)] * scale
            qh += [jnp.where(head0, q, jnp.zeros_like(q)), jnp.where(head0, jnp.zeros_like(q), q)]

        def offset(n):
            return pl.multiple_of(r0 + (n_diag - 1 - n) * tk, tk)

        def mask_of(n):
            if n is MKEYS:
                return col_in < n_meta
            if isinstance(n, int) and n < n_diag:
                return (col_in + (n_diag - 1 - n) * tk) < row_in
            return None

        def load(n, ref, mref):
            return mref[...] if n is MKEYS else ref[0, pl.ds(offset(n), tk), :]

        def step(par, t1, t2, t3):
            if t3 is not None:
                v, m = load(t3, v_ref, vm_ref if n_meta else None), mask_of(t3)
                for h in range(2 * n_hp):
                    g = run_scr[h] + cs_scr[h]
                    a = jnp.exp(z_scr[par, h] + g)
                    if m is not None:
                        a = jnp.where(m, a, 0.0)
                    run_scr[h] = jnp.broadcast_to(g[:, 0:1], g.shape)
                    acc_scr[h] += _dot(a.astype(BF16), v[:, psl(h // 2)])
            if t2 is not None:
                m = mask_of(t2)
                for h in range(2 * n_hp):
                    cs_scr[h] = _sb_suffix(z_scr[1 - par, h], m, u_ref)
            if t1 is not None:
                k = load(t1, k_ref, km_ref if n_meta else None)
                for h in range(2 * n_hp):
                    z_scr[par, h] = _dot_nt(qh[h], k[:, psl(h // 2)])

        run_scr[...] = jnp.zeros_like(run_scr)
        acc_scr[...] = jnp.zeros_like(acc_scr)

        if static_tiles:
            tiles = list(range(n_real)) + ([MKEYS] if has_meta else [])
            for s in range(len(tiles) + 2):
                pick = lambda j: tiles[j] if 0 <= j < len(tiles) else None
                step(s % 2, pick(s), pick(s - 1), pick(s - 2))
        else:
            n_pro = n_diag + 2
            assert n_pro % 2 == 0
            for s in range(n_pro):
                pick = lambda j: j if j >= 0 else None
                step(s % 2, pick(s), pick(s - 1), pick(s - 2))

            def body(j, carry):
                s = n_pro + 2 * j
                step(0, s, s - 1, s - 2)
                step(1, s + 1, s, s - 1)
                return carry

            lax.fori_loop(0, (n_real - n_pro) // 2, body, 0)
            last = n_real - 1
            step(0, MKEYS if has_meta else None, last, last - 1)
            step(1, None, MKEYS if has_meta else None, last)
            if has_meta:
                step(0, None, None, MKEYS)

        for p in range(n_hp):
            o_ref[0, pl.ds(r0, tq), psl(p)] = jnp.where(
                head0, acc_scr[2 * p], acc_scr[2 * p + 1]).astype(o_ref.dtype)

    q_tile(0, n_diag, True)
    if n_q > 1:
        assert n_diag >= 2, "the pipelined prologue needs two diagonal key tiles per query tile"

        def outer(i, carry):
            q_tile(i, n_diag * (i + 1), False)
            return carry

        lax.fori_loop(1, n_q, outer, 0)


def _suffix_matrix(t):
    r = jnp.arange(t)
    return -(r[:, None] >= r[None, :]).astype(BF16)


def stickbreak_attention(qkv, qkv_meta, d_model, hd):
    b, s_len, _ = qkv.shape
    n_pair = d_model // LANES
    tq = min(ATT_TQ, s_len)
    tk = min(ATT_TK, s_len)
    n_meta = 0 if qkv_meta is None else qkv_meta.shape[0]
    n_hp = ATT_PAIRS if n_pair % ATT_PAIRS == 0 else 1
    width = n_hp * LANES
    n_grp = n_pair // n_hp
    col = lambda which: (lambda bi, hp: (bi, 0, which * n_grp + hp))
    in_specs = [pl.BlockSpec((1, s_len, width), col(0)),
                pl.BlockSpec((1, s_len, width), col(1)),
                pl.BlockSpec((1, s_len, width), col(2))]
    args = [qkv, qkv, qkv]
    if n_meta:
        assert n_meta <= tk
        meta_pad = jnp.zeros((tk, qkv_meta.shape[1]), qkv_meta.dtype).at[:n_meta].set(qkv_meta)
        in_specs += [pl.BlockSpec((tk, width), lambda bi, hp: (0, n_grp + hp)),
                     pl.BlockSpec((tk, width), lambda bi, hp: (0, 2 * n_grp + hp))]
        args += [meta_pad, meta_pad]
    in_specs.append(pl.BlockSpec((tk, tk), lambda bi, hp: (0, 0)))
    args.append(_suffix_matrix(tk))
    kern = functools.partial(_sb_kernel, tq=tq, tk=tk, hd=hd, n_meta=n_meta, scale=hd ** -0.5)
    nh = 2 * n_hp
    return pl.pallas_call(
        kern,
        grid=(b, n_grp),
        in_specs=in_specs,
        out_specs=pl.BlockSpec((1, s_len, width), lambda bi, hp: (bi, 0, hp)),
        out_shape=jax.ShapeDtypeStruct((b, s_len, d_model), BF16),
        scratch_shapes=[pltpu.VMEM((2, nh, tq, tk), F32), pltpu.VMEM((nh, tq, tk), F32),
                        pltpu.VMEM((nh, tq, tk), F32), pltpu.VMEM((nh, tq, LANES), F32)],
        compiler_params=_cparams(2),
        name="stickbreak_attention",
    )(*args)


def _log_sigmoid(y):
    return jnp.minimum(y, 0.0) - jnp.log(1.0 + jnp.exp(-jnp.abs(y)))


def _tri_cumsum(tri, x):
    h1, h2, h3 = _split3(x)
    return _dot(tri, h1) + _dot(tri, h2) + _dot(tri, h3)


def _forget_cumsum_kernel(f_ref, fm_ref, b_ref, tri_ref, trim_ref, o_ref, om_ref, *, chunk, n_heads):
    bias = b_ref[...]

    def packed(f):
        lane = lax.broadcasted_iota(jnp.int32, f.shape, 1)
        out = None
        for a, part in enumerate(_split3(f)):
            pa = jnp.where(lane < n_heads, part.astype(F32), 0.0)
            pa = pa if a == 0 else pltpu.roll(pa, a * n_heads, 1)
            out = pa if out is None else out + pa
        return out.astype(BF16)

    fm = _tri_cumsum(trim_ref[...], _log_sigmoid(fm_ref[...] + bias))
    om_ref[...] = packed(fm)
    n_meta = fm.shape[0]
    carry = fm[n_meta - 1:n_meta, :]
    s_len = f_ref.shape[1]
    for c0 in range(0, s_len, chunk):
        fc = _tri_cumsum(tri_ref[...], _log_sigmoid(f_ref[0, c0:c0 + chunk, :] + bias)) + carry
        o_ref[0, c0:c0 + chunk, :] = packed(fc)
        carry = fc[chunk - 1:chunk, :]


def forget_cumsum(f_logit, f_logit_meta, bias, n_heads):
    b, s_len, w = f_logit.shape
    n_meta = f_logit_meta.shape[0]
    assert 3 * n_heads <= w
    chunk = min(CUM_CHUNK, s_len)
    tri = lambda t: (jnp.arange(t)[:, None] >= jnp.arange(t)[None, :]).astype(BF16)
    return pl.pallas_call(
        functools.partial(_forget_cumsum_kernel, chunk=chunk, n_heads=n_heads),
        grid=(b,),
        in_specs=[
            pl.BlockSpec((1, s_len, w), lambda bi: (bi, 0, 0)),
            pl.BlockSpec((n_meta, w), lambda bi: (0, 0)),
            pl.BlockSpec((1, w), lambda bi: (0, 0)),
            pl.BlockSpec((chunk, chunk), lambda bi: (0, 0)),
            pl.BlockSpec((n_meta, n_meta), lambda bi: (0, 0)),
        ],
        out_specs=[pl.BlockSpec((1, s_len, w), lambda bi: (bi, 0, 0)),
                   pl.BlockSpec((n_meta, w), lambda bi: (0, 0))],
        out_shape=[jax.ShapeDtypeStruct((b, s_len, w), BF16),
                   jax.ShapeDtypeStruct((n_meta, w), BF16)],
        compiler_params=_cparams(1),
        name="forget_cumsum",
    )(f_logit, f_logit_meta, bias, tri(chunk), tri(n_meta))


def _fox_kernel(qc_ref, kc_ref, vc_ref, f_ref, kmc_ref, vmc_ref, fm_ref, pq_ref, pk_ref, pv_ref,
                cq_ref, ck_ref, cv_ref, o_ref,
                q_ref, k_ref, v_ref, km_ref, vm_ref, z_scr, p_scr, m_scr, alpha_scr, acc_scr,
                *, tq, tk, hd, n_meta):
    s_len = qc_ref.shape[1]
    n_q = s_len // tq
    n_diag = tq // tk
    lane = lax.broadcasted_iota(jnp.int32, (tq, LANES), 1)
    head0 = lane < hd
    row_in = lax.broadcasted_iota(jnp.int32, (tq, tk), 0)
    col_in = lax.broadcasted_iota(jnp.int32, (tq, tk), 1)
    MKEYS = "meta"
    n_hp = qc_ref.shape[2] // LANES
    n_h = 2 * n_hp
    hsl = lambda h: slice(h * LANES, (h + 1) * LANES)
    psl = lambda p: slice(p * LANES, (p + 1) * LANES)
    xsl = lambda p: slice(2 * p * LANES, 2 * (p + 1) * LANES)

    def place(main, fpacked, p_mat, c_ref):
        lhs = main if fpacked is None else jnp.concatenate([main, fpacked], axis=1)
        return (_dot(lhs, p_mat) + c_ref[...]).astype(BF16)

    fpm = fm_ref[...]
    for p in range(n_hp):
        for c0 in range(0, s_len, tq):
            rows = slice(c0, c0 + tq)
            fp = f_ref[0, rows, :]
            q_ref[rows, xsl(p)] = place(qc_ref[0, rows, psl(p)], fp, pq_ref[p], cq_ref)
            k_ref[rows, xsl(p)] = place(kc_ref[0, rows, psl(p)], fp, pk_ref[p], ck_ref)
            v_ref[rows, xsl(p)] = place(vc_ref[0, rows, psl(p)], None, pv_ref[0], cv_ref)
        km_ref[:, xsl(p)] = place(kmc_ref[:, psl(p)], fpm, pk_ref[p], ck_ref)
        vm_ref[:, xsl(p)] = place(vmc_ref[:, psl(p)], None, pv_ref[0], cv_ref)

    def q_tile(i, n_real, static_tiles):
        r0 = pl.multiple_of(i * tq, tq)
        qh = [q_ref[pl.ds(r0, tq), hsl(h)] for h in range(n_h)]

        def mask_of(t):
            if t is MKEYS:
                return col_in < n_meta
            return None if t[1] is None else (col_in + t[1] * tk) <= row_in

        def load(t, ref, mref, h):
            if t is MKEYS:
                return mref[:, hsl(h)]
            return ref[pl.ds(pl.multiple_of(t[0] * tk, tk), tk), hsl(h)]

        def step(par, t1, t2, t3):
            if t3 is not None:
                for h in range(n_h):
                    acc_scr[h] = acc_scr[h] * alpha_scr[h] + _dot(p_scr[h], load(t3, v_ref, vm_ref, h))
            if t2 is not None:
                m = mask_of(t2)
                for h in range(n_h):
                    z = z_scr[1 - par, h]
                    if m is not None:
                        z = jnp.where(m, z, -jnp.inf)
                    m_old = m_scr[h]
                    m_new = jnp.maximum(m_old, jnp.max(z, axis=-1, keepdims=True))
                    p_scr[h] = jnp.exp(z - m_new).astype(BF16)
                    alpha_scr[h] = jnp.exp(m_old - m_new)[:, :LANES]
                    m_scr[h] = m_new
            if t1 is not None:
                for h in range(n_h):
                    z_scr[par, h] = _dot_nt(qh[h], load(t1, k_ref, km_ref, h))

        m_scr[...] = jnp.full(m_scr.shape, -jnp.inf, F32)
        acc_scr[...] = jnp.zeros_like(acc_scr)

        if static_tiles:
            tiles = [MKEYS] + [(n, n - (n_real - n_diag) if n >= n_real - n_diag else None)
                              for n in range(n_real)]
            for s in range(len(tiles) + 2):
                pick = lambda j: tiles[j] if 0 <= j < len(tiles) else None
                step(s % 2, pick(s), pick(s - 1), pick(s - 2))
        else:
            pos = lambda p: MKEYS if (isinstance(p, int) and p == 0) else (p - 1, None)
            n_pro = 4
            for s in range(n_pro):
                pick = lambda j: pos(j) if j >= 0 else None
                step(s % 2, pick(s), pick(s - 1), pick(s - 2))

            def body(j, carry):
                s = n_pro + 2 * j
                step(0, pos(s), pos(s - 1), pos(s - 2))
                step(1, pos(s + 1), pos(s), pos(s - 1))
                return carry

            lax.fori_loop(0, (n_real - n_pro) // 2, body, 0)
            a, b = n_real - 2, n_real - 1
            step(0, (b, None), (a, 0), (a - 1, None))
            step(1, None, (b, 1), (a, None))
            step(0, None, None, (b, None))

        for p in range(n_hp):
            res0, res1 = acc_scr[2 * p], acc_scr[2 * p + 1]
            rot0 = pltpu.roll(res0, hd, 1)
            rot1 = pltpu.roll(res1, hd, 1)
            o_ref[0, pl.ds(r0, tq), psl(p)] = jnp.where(
                head0, res0 / rot0, rot1 / res1).astype(o_ref.dtype)

    q_tile(0, n_diag, True)
    if n_q > 1:
        assert n_diag == 2, "the pipelined sweep is written for two diagonal key tiles per query tile"

        def outer(i, carry):
            q_tile(i, n_diag * (i + 1), False)
            return carry

        lax.fori_loop(1, n_q, outer, 0)


def _fox_placement(n_pair, hd):
    n_heads = 2 * n_pair
    pq = np.zeros((n_pair, 2 * LANES, 2 * LANES), np.float32)
    pk = np.zeros((n_pair, 2 * LANES, 2 * LANES), np.float32)
    pv = np.zeros((1, LANES, 2 * LANES), np.float32)
    cq = np.zeros((1, 2 * LANES), np.float32)
    ck = np.zeros((1, 2 * LANES), np.float32)
    cv = np.zeros((1, 2 * LANES), np.float32)
    for h in range(2):
        base = h * LANES
        for j in range(hd):
            pq[:, h * hd + j, base + j] = hd ** -0.5
            pk[:, h * hd + j, base + j] = 1.0
            pv[:, h * hd + j, base + j] = 1.0
        for p in range(3):
            for hp in range(n_pair):
                pq[hp, LANES + p * n_heads + 2 * hp + h, base + hd + p] = 1.0
                pk[hp, LANES + p * n_heads + 2 * hp + h, base + hd + 3 + p] = -1.0
        cq[0, base + hd + 3:base + hd + 6] = 1.0
        ck[0, base + hd:base + hd + 3] = 1.0
        cv[0, base + hd:base + LANES] = 1.0
    bf = lambda a: jnp.asarray(a, BF16)
    return bf(pq), bf(pk), bf(pv), jnp.asarray(cq), jnp.asarray(ck), jnp.asarray(cv)


def forgetting_attention(q, kv, kv_meta, fparts, fparts_meta, d_model, hd):
    b, s_len, _ = q.shape
    n_pair = d_model // LANES
    n_meta = kv_meta.shape[0]
    tq = min(ATT_TQ, s_len)
    tk = min(ATT_TK, s_len)
    assert n_meta <= tk and tk == LANES and 3 * 2 * n_pair <= LANES
    kvm = jnp.zeros((tk, kv_meta.shape[1]), kv_meta.dtype).at[:n_meta].set(kv_meta)
    fpm = jnp.zeros((tk, LANES), fparts_meta.dtype).at[:n_meta].set(fparts_meta)
    pq, pk, pv, cq, ck, cv = _fox_placement(n_pair, hd)
    n_hp = FOX_PAIRS if n_pair % FOX_PAIRS == 0 else 1
    width = n_hp * LANES
    n_grp = n_pair // n_hp
    nh = 2 * n_hp
    blk = lambda off: pl.BlockSpec((1, s_len, width), lambda bi, hp: (bi, 0, off + hp))
    mblk = lambda off: pl.BlockSpec((tk, width), lambda bi, hp: (0, off + hp))
    crow = pl.BlockSpec((1, 2 * LANES), lambda bi, hp: (0, 0))
    in_specs = [blk(0), blk(0), blk(n_grp),
                pl.BlockSpec((1, s_len, LANES), lambda bi, hp: (bi, 0, 0)),
                mblk(0), mblk(n_grp),
                pl.BlockSpec((tk, LANES), lambda bi, hp: (0, 0)),
                pl.BlockSpec((n_hp, 2 * LANES, 2 * LANES), lambda bi, hp: (hp, 0, 0)),
                pl.BlockSpec((n_hp, 2 * LANES, 2 * LANES), lambda bi, hp: (hp, 0, 0)),
                pl.BlockSpec((1, LANES, 2 * LANES), lambda bi, hp: (0, 0, 0)),
                crow, crow, crow]
    ext = lambda rows: pltpu.VMEM((rows, nh * LANES), BF16)
    return pl.pallas_call(
        functools.partial(_fox_kernel, tq=tq, tk=tk, hd=hd, n_meta=n_meta),
        grid=(b, n_grp),
        in_specs=in_specs,
        out_specs=pl.BlockSpec((1, s_len, width), lambda bi, hp: (bi, 0, hp)),
        out_shape=jax.ShapeDtypeStruct((b, s_len, d_model), BF16),
        scratch_shapes=[ext(s_len), ext(s_len), ext(s_len), ext(tk), ext(tk),
                        pltpu.VMEM((2, nh, tq, tk), F32), pltpu.VMEM((nh, tq, tk), BF16),
                        pltpu.VMEM((nh, tq, tk), F32), pltpu.VMEM((nh, tq, LANES), F32),
                        pltpu.VMEM((nh, tq, LANES), F32)],
        compiler_params=_cparams(2),
        name="forgetting_attention",
    )(q, kv, kv, fparts, kvm, kvm, fpm, pq, pk, pv, cq, ck, cv)


def _router_kernel(a_ref, wo_ref, r_ref, g_ref, wr_ref, h_ref, xn_ref, id_ref, g1_ref, g2_ref,
                   *, n_experts, n_chunk):
    a = a_ref[...]
    for c0 in range(0, wo_ref.shape[1], n_chunk):
        h_ref[:, c0:c0 + n_chunk] = r_ref[:, c0:c0 + n_chunk] + _dot(a, wo_ref[:, c0:c0 + n_chunk])
    xn = _rms(h_ref[...], g_ref[...])
    xn_ref[...] = xn.astype(BF16)
    a1, a2 = _split2(xn)
    b1, b2 = _split2(wr_ref[...])
    logits = _dot(a1, b1) + _dot(a1, b2) + _dot(a2, b1)
    lane = lax.broadcasted_iota(jnp.int32, logits.shape, 1)
    lg = jnp.where(lane < n_experts, logits, -jnp.inf)
    m1 = jnp.max(lg, axis=-1, keepdims=True)
    i1 = jnp.min(jnp.where(lg == m1, lane, LANES), axis=-1, keepdims=True)
    lg2 = jnp.where(lane == i1, -jnp.inf, lg)
    m2 = jnp.max(lg2, axis=-1, keepdims=True)
    i2 = jnp.min(jnp.where(lg2 == m2, lane, LANES), axis=-1, keepdims=True)
    e2 = jnp.exp(m2 - m1)
    den = 1.0 + e2
    id_ref[...] = jnp.where(lane == 0, i1, jnp.where(lane == 1, i2, 0))
    g1_ref[...] = jnp.broadcast_to(1.0 / den, logits.shape)
    g2_ref[...] = jnp.broadcast_to(e2 / den, logits.shape)


def moe_router(a, w_o, res, g, w_router):
    rows, d = res.shape
    k = a.shape[1]
    n_experts = w_router.shape[1]
    wr = jnp.zeros((d, LANES), F32).at[:, :n_experts].set(w_router)
    tm = _row_tile(rows)
    row_blk = lambda w: pl.BlockSpec((tm, w), lambda i: (i, 0))
    return pl.pallas_call(
        functools.partial(_router_kernel, n_experts=n_experts, n_chunk=_col_chunk(d)),
        grid=(rows // tm,),
        in_specs=[
            row_blk(k),
            pl.BlockSpec((k, d), lambda i: (0, 0)),
            row_blk(d),
            pl.BlockSpec((1, d), lambda i: (0, 0)),
            pl.BlockSpec((d, LANES), lambda i: (0, 0)),
        ],
        out_specs=[row_blk(d), row_blk(d), row_blk(LANES), row_blk(LANES), row_blk(LANES)],
        out_shape=[jax.ShapeDtypeStruct((rows, d), F32),
                   jax.ShapeDtypeStruct((rows, d), BF16),
                   jax.ShapeDtypeStruct((rows, LANES), jnp.int32),
                   jax.ShapeDtypeStruct((rows, LANES), F32),
                   jax.ShapeDtypeStruct((rows, LANES), F32)],
        compiler_params=_cparams(1),
        name="moe_router",
    )(a, w_o, res, g, wr)


def _moe_ffn_kernel(te_ref, nt_ref, x_ref, wg_ref, wu_ref, wda_ref, wdb_ref, o_ref, *, f_chunk):
    i = pl.program_id(0)

    @pl.when(i < nt_ref[0])
    def _():
        acc_a, acc_b = _swiglu_acc(x_ref[...], wg_ref, wu_ref, wda_ref, wdb_ref, f_chunk, widx=(0,))
        half = acc_a.shape[1]
        o_ref[:, :half] = acc_a.astype(o_ref.dtype)
        o_ref[:, half:] = acc_b.astype(o_ref.dtype)

    @pl.when(i >= nt_ref[0])
    def _():
        o_ref[...] = jnp.zeros_like(o_ref)


def moe_grouped_ffn(xs, tile_expert, n_tiles_used, wg, wu, wda, wdb):
    p_rows, d = xs.shape
    n_experts, _, ff = wg.shape
    tm = MOE_TM
    expert = lambda shape: pl.BlockSpec(shape, lambda i, te, nt: (te[i], 0, 0))
    grid_spec = pltpu.PrefetchScalarGridSpec(
        num_scalar_prefetch=2,
        grid=(p_rows // tm,),
        in_specs=[
            pl.BlockSpec((tm, d), lambda i, te, nt: (i, 0)),
            expert((1, d, ff)), expert((1, d, ff)), expert((1, ff, d // 2)), expert((1, ff, d // 2)),
        ],
        out_specs=pl.BlockSpec((tm, d), lambda i, te, nt: (i, 0)),
    )
    return pl.pallas_call(
        functools.partial(_moe_ffn_kernel, f_chunk=FF_CHUNK),
        grid_spec=grid_spec,
        out_shape=jax.ShapeDtypeStruct((p_rows, d), BF16),
        compiler_params=_cparams(1),
        name="moe_grouped_ffn",
    )(tile_expert, n_tiles_used, xs, wg, wu, wda, wdb)


def _combine_kernel(h_ref, y1_ref, y2_ref, g1_ref, g2_ref, o_ref):
    d = h_ref.shape[1]
    g1, g2 = g1_ref[...], g2_ref[...]
    for c0 in range(0, d, LANES):
        cs = slice(c0, c0 + LANES)
        o_ref[:, cs] = (h_ref[:, cs] + g1 * y1_ref[:, cs].astype(F32)
                        + g2 * y2_ref[:, cs].astype(F32))


def moe_block(a, w_o, res, g, w_router, ffn_weights):
    rows, d = res.shape
    n_experts = w_router.shape[1]
    tm = MOE_TM
    h, xn, ids, g1, g2 = moe_router(a, w_o, res, g, w_router)

    eid = ids[:, :TOP_K]
    sel = (eid[:, :, None] == jnp.arange(n_experts)[None, None, :]).any(axis=1)
    rank = jnp.cumsum(sel.astype(jnp.int32), axis=0) - 1
    cnt = rank[-1] + 1
    cnt_pad = ((cnt + tm - 1) // tm) * tm
    ends = jnp.cumsum(cnt_pad)
    offs = ends - cnt_pad
    pos = offs[eid] + jnp.take_along_axis(rank, eid, axis=1)
    p_rows = rows * TOP_K + n_experts * tm
    n_tiles = p_rows // tm
    tile_start = jnp.arange(n_tiles, dtype=jnp.int32) * tm
    tile_expert = jnp.minimum((tile_start[:, None] >= ends[None, :]).sum(axis=1), n_experts - 1)
    n_used = (ends[-1] // tm).astype(jnp.int32).reshape(1)

    tok = jnp.arange(rows, dtype=jnp.int32)[:, None]
    tok_c = jnp.sort((eid * rows + tok).reshape(-1)) % rows
    tok_c = jnp.concatenate([tok_c, jnp.zeros((p_rows - rows * TOP_K,), jnp.int32)])
    cstart = jnp.cumsum(cnt) - cnt
    slot = jnp.arange(p_rows, dtype=jnp.int32)
    tok_sorted = jnp.zeros((p_rows,), jnp.int32)
    for e in range(n_experts):
        inside = (slot >= offs[e]) & (slot < offs[e] + cnt[e])
        tok_sorted = jnp.where(inside, jnp.roll(tok_c, offs[e] - cstart[e]), tok_sorted)
    xs = xn.at[tok_sorted].get(mode="promise_in_bounds")
    ys = moe_grouped_ffn(xs, tile_expert.astype(jnp.int32), n_used, *ffn_weights)
    y1 = ys.at[pos[:, 0]].get(mode="promise_in_bounds", unique_indices=True)
    y2 = ys.at[pos[:, 1]].get(mode="promise_in_bounds", unique_indices=True)
    tr = _row_tile(rows)
    spec = pl.BlockSpec((tr, d), lambda i: (i, 0))
    gspec = pl.BlockSpec((tr, LANES), lambda i: (i, 0))
    return pl.pallas_call(
        _combine_kernel,
        grid=(rows // tr,),
        in_specs=[spec, spec, spec, gspec, gspec],
        out_specs=spec,
        out_shape=jax.ShapeDtypeStruct((rows, d), F32),
        compiler_params=_cparams(1),
        name="moe_combine",
    )(h, y1, y2, g1, g2)


def kernel(x, meta_tokens, norm_attn_a, w_qkv_a, w_o_a, norm_kv, w_kvf, b_f, k_norm, norm_attn_b,
           w_q_b, q_norm_b, w_o_b, norm_ffn_dense, w_gu_dense, w_down_dense, norm_ffn_moe,
           w_router, w_gu_moe, w_down_moe):
    b, s_len, d = x.shape
    n_heads = b_f.shape[0]
    hd = d // n_heads
    n_meta = meta_tokens.shape[0]
    depth = norm_attn_a.shape[0] + norm_attn_b.shape[0]
    n_a = norm_attn_a.shape[0]
    assert 2 * hd == LANES and d % LANES == 0
    row = lambda v: v.reshape(1, -1).astype(F32)

    h = x.reshape(b * s_len, d)
    hm = meta_tokens.astype(x.dtype)
    for layer in range(depth):
        last = layer == depth - 1
        if layer < n_a:
            w_qkv = to_bf16(w_qkv_a[layer])
            g = row(norm_attn_a[layer])
            qkv = norm_matmul(h, g, w_qkv, BF16).reshape(b, s_len, 3 * d)
            qkv_m = norm_matmul(hm, g, w_qkv, BF16)
            o = stickbreak_attention(qkv, qkv_m, d, hd).reshape(b * s_len, d)
            o_m = stickbreak_attention(qkv_m[None], None, d, hd)[0]
            w_o = to_bf16(w_o_a[layer])
            hm = matmul_residual(o_m, w_o, hm)
        else:
            if layer == n_a:
                g = row(norm_kv)
                w_kv = to_bf16(w_kvf[:, :2 * d])
                w_f = jnp.zeros((d, LANES), F32).at[:, :n_heads].set(w_kvf[:, 2 * d:]).astype(BF16)
                kgain = row(jnp.tile(k_norm, n_heads))
                shared = [(g, w_kv, BF16, kgain, hd), (g, w_f, F32, None, None)]
            i = layer - n_a
            q_proj = (row(norm_attn_b[i]), to_bf16(w_q_b[i]), BF16,
                      row(jnp.tile(q_norm_b[i], n_heads)), hd)
            if layer == n_a:
                kv, fl, q = norm_projections(h, shared + [q_proj])
                kv_m, fl_m = norm_projections(hm, shared)
                bias = jnp.zeros((1, LANES), F32).at[0, :n_heads].set(b_f.astype(F32))
                f_cum, f_cum_m = forget_cumsum(fl.reshape(b, s_len, LANES), fl_m, bias, n_heads)
                kv_sh = kv.reshape(b, s_len, 2 * d)
            else:
                q, = norm_projections(h, [q_proj])
            o = forgetting_attention(q.reshape(b, s_len, d), kv_sh, kv_m, f_cum, f_cum_m, d, hd)
            o = o.reshape(b * s_len, d)
            w_o = to_bf16(w_o_b[i])
            if not last:
                raise NotImplementedError("meta-row queries in forgetting layers before the last")
        j = layer // 2
        if layer % 2 == 0:
            ffn_w = to_bf16(w_gu_dense[j], splits=2) + to_bf16(w_down_dense[j], splits=2)
            g = row(norm_ffn_dense[j])
            h = dense_ffn(matmul_residual(o, w_o, h), g, *ffn_w)
            if not last:
                hm = dense_ffn(hm, g, *ffn_w)
        else:
            ffn_w = to_bf16(w_gu_moe[j], splits=2) + to_bf16(w_down_moe[j], splits=2)
            h = moe_block(o, w_o, h, row(norm_ffn_moe[j]), w_router[j], ffn_w)
            if not last:
                raise NotImplementedError("meta rows through a MoE layer before the last")
    return h.reshape(b, s_len, d)
```

```python
import functools

import jax
import jax.numpy as jnp
import numpy as np
from jax import lax
from jax.experimental import pallas as pl
from jax.experimental.pallas import tpu as pltpu

F32 = jnp.float32
BF16 = jnp.bfloat16

EPS = 1e-6
LOG2E = 1.4426950408889634
TOP_K = 2
LANES = 128
MXU_WIDTH = 256
VMEM_LIMIT = 56 * 1024 * 1024
ROW_TILE = 512
COL_CHUNK = 512
FF_CHUNK = 256
ATT_TQ = 256
SB_TQ = 512
ATT_TK = 128
ATT_PAIRS = 4
FOX_PAIRS = 4
MOE_TM = 512
CUM_CHUNK = 256
CAST_BLOCK_BYTES = 6 * 1024 * 1024


def _cparams(n_axes):
    return pltpu.CompilerParams(dimension_semantics=("arbitrary",) * n_axes,
                                vmem_limit_bytes=VMEM_LIMIT)


def _split3(x):
    h1 = x.astype(BF16)
    r1 = x - h1.astype(F32)
    h2 = r1.astype(BF16)
    h3 = (r1 - h2.astype(F32)).astype(BF16)
    return h1, h2, h3


def _split2(x):
    h1 = x.astype(BF16)
    h2 = (x - h1.astype(F32)).astype(BF16)
    return h1, h2


def _dot(a, b):
    return jnp.dot(a, b, preferred_element_type=F32)


def _dot_nt(a, b):
    return lax.dot_general(a, b, (((1,), (1,)), ((), ())), preferred_element_type=F32)


def _rms(xf, g):
    ms = jnp.mean(xf * xf, axis=-1, keepdims=True)
    return xf * lax.rsqrt(ms + EPS) * g


def _row_tile(rows, pref=ROW_TILE):
    return pref if rows % pref == 0 else rows


def _col_chunk(*widths):
    c = COL_CHUNK
    while any(w % c for w in widths):
        c //= 2
    assert c >= LANES
    return c


def _cast_kernel(*refs):
    n = len(refs) // 2
    for x_ref, o_ref in zip(refs[:n], refs[n:]):
        o_ref[...] = x_ref[...].astype(o_ref.dtype)


def to_bf16(w, splits=1):
    shape = w.shape
    cols = shape[-1]
    assert cols % (splits * LANES) == 0
    sc = cols // splits
    w2 = w.reshape(-1, cols)
    rows = w2.shape[0]
    rb = rows
    for cand in (2048, 1024, 512, 256, 128, 64):
        if rows % cand == 0 and cand * cols * 4 <= CAST_BLOCK_BYTES:
            rb = cand
            break
    spec = lambda s: pl.BlockSpec((rb, sc), functools.partial(lambda i, s: (i, s), s=s))
    outs = pl.pallas_call(
        _cast_kernel,
        grid=(rows // rb,),
        in_specs=[spec(s) for s in range(splits)],
        out_specs=[pl.BlockSpec((rb, sc), lambda i: (i, 0)) for _ in range(splits)],
        out_shape=[jax.ShapeDtypeStruct((rows, sc), BF16) for _ in range(splits)],
        compiler_params=_cparams(1),
        name="to_bf16",
    )(*([w2] * splits))
    outs = [o.reshape(shape[:-1] + (sc,)) for o in outs]
    return outs[0] if splits == 1 else outs


def _norm_matmul_kernel(*refs, cfg):
    n_p = len(cfg)
    x_ref, ins, outs = refs[0], refs[1:1 + 4 * n_p], refs[1 + 4 * n_p:]
    xf = x_ref[...]
    xs = xf * lax.rsqrt(jnp.mean(xf * xf, axis=-1, keepdims=True) + EPS)
    for pi, (n_chunk, head_cols) in enumerate(cfg):
        g_ref, w_ref, hg_ref, gm_ref = ins[4 * pi:4 * pi + 4]
        o_ref = outs[pi]
        xn = (xs * g_ref[...]).astype(BF16)
        n = w_ref.shape[1]
        for c0 in range(0, n, n_chunk):
            acc = _dot(xn, w_ref[:, c0:c0 + n_chunk])
            if c0 < head_cols:
                sq = (acc * acc).astype(BF16)
                gw = gm_ref.shape[0]
                ms = jnp.concatenate([_dot(sq[:, j:j + gw], gm_ref[...])
                                      for j in range(0, n_chunk, gw)], axis=1)
                acc = acc * lax.rsqrt(ms + EPS) * hg_ref[:, c0:c0 + n_chunk]
            o_ref[:, c0:c0 + n_chunk] = acc.astype(o_ref.dtype)


def norm_projections(x, projs):
    rows, d = x.shape
    tm = _row_tile(rows)
    const = lambda shape: pl.BlockSpec(shape, lambda i: (0,) * len(shape))
    cfg, args, in_specs, out_specs, out_shape = [], [x], [pl.BlockSpec((tm, d), lambda i: (i, 0))], [], []
    for g, w, out_dtype, head_gain, head_dim in projs:
        n = w.shape[1]
        if head_gain is None:
            n_chunk = _col_chunk(n)
            head_cols = 0
            head_gain = jnp.zeros((1, n_chunk), F32)
            head_dim = min(n_chunk, MXU_WIDTH)
        else:
            head_cols = head_gain.shape[1]
            n_chunk = _col_chunk(n, head_cols)
            assert n_chunk % head_dim == 0
        gw = min(n_chunk, MXU_WIDTH)
        assert gw % head_dim == 0 and n_chunk % gw == 0
        grp = jnp.arange(gw) // head_dim
        gm = ((grp[:, None] == grp[None, :]).astype(F32) / head_dim).astype(BF16)
        cfg.append((n_chunk, head_cols))
        args += [g, w, head_gain, gm]
        in_specs += [const((1, d)), const((d, n)), const(head_gain.shape), const((gw, gw))]
        out_specs.append(pl.BlockSpec((tm, n), lambda i: (i, 0)))
        out_shape.append(jax.ShapeDtypeStruct((rows, n), out_dtype))
    return pl.pallas_call(
        functools.partial(_norm_matmul_kernel, cfg=tuple(cfg)),
        grid=(rows // tm,),
        in_specs=in_specs,
        out_specs=out_specs,
        out_shape=out_shape,
        compiler_params=_cparams(1),
        name="norm_matmul",
    )(*args)


def norm_matmul(x, g, w, out_dtype, *, head_gain=None, head_dim=None):
    return norm_projections(x, [(g, w, out_dtype, head_gain, head_dim)])[0]


def _matmul_res_kernel(a_ref, w_ref, r_ref, o_ref, *, n_chunk):
    a = a_ref[...]
    n = w_ref.shape[1]
    for c0 in range(0, n, n_chunk):
        o_ref[:, c0:c0 + n_chunk] = r_ref[:, c0:c0 + n_chunk] + _dot(a, w_ref[:, c0:c0 + n_chunk])


def matmul_residual(a, w, res):
    rows, k = a.shape
    n = w.shape[1]
    tm = _row_tile(rows)
    n_chunk = _col_chunk(n)
    return pl.pallas_call(
        functools.partial(_matmul_res_kernel, n_chunk=n_chunk),
        grid=(rows // tm,),
        in_specs=[
            pl.BlockSpec((tm, k), lambda i: (i, 0)),
            pl.BlockSpec((k, n), lambda i: (0, 0)),
            pl.BlockSpec((tm, n), lambda i: (i, 0)),
        ],
        out_specs=pl.BlockSpec((tm, n), lambda i: (i, 0)),
        out_shape=jax.ShapeDtypeStruct((rows, n), F32),
        compiler_params=_cparams(1),
        name="matmul_residual",
    )(a, w, res)


def _swiglu_acc(xn, wg_ref, wu_ref, wda_ref, wdb_ref, f_chunk, widx=()):
    ff = wg_ref.shape[-1]
    acc_a = acc_b = None
    for c0 in range(0, ff, f_chunk):
        cols = widx + (slice(None), slice(c0, c0 + f_chunk))
        rows = widx + (slice(c0, c0 + f_chunk), slice(None))
        gt = _dot(xn, wg_ref[cols])
        up = _dot(xn, wu_ref[cols])
        act = (gt * (1.0 / (1.0 + jnp.exp(-gt))) * up).astype(BF16)
        pa, pb = _dot(act, wda_ref[rows]), _dot(act, wdb_ref[rows])
        acc_a = pa if acc_a is None else acc_a + pa
        acc_b = pb if acc_b is None else acc_b + pb
    return acc_a, acc_b


def _dense_ffn_kernel(x_ref, g_ref, wg_ref, wu_ref, wda_ref, wdb_ref, o_ref, *, f_chunk):
    xf = x_ref[...]
    xn = _rms(xf, g_ref[...]).astype(BF16)
    acc_a, acc_b = _swiglu_acc(xn, wg_ref, wu_ref, wda_ref, wdb_ref, f_chunk)
    half = acc_a.shape[1]
    o_ref[:, :half] = xf[:, :half] + acc_a
    o_ref[:, half:] = xf[:, half:] + acc_b


def dense_ffn(x, g, wg, wu, wda, wdb):
    rows, d = x.shape
    ff = wg.shape[1]
    tm = _row_tile(rows)
    const = lambda shape: pl.BlockSpec(shape, lambda i: (0, 0))
    return pl.pallas_call(
        functools.partial(_dense_ffn_kernel, f_chunk=FF_CHUNK),
        grid=(rows // tm,),
        in_specs=[
            pl.BlockSpec((tm, d), lambda i: (i, 0)),
            const((1, d)), const((d, ff)), const((d, ff)), const((ff, d // 2)), const((ff, d // 2)),
        ],
        out_specs=pl.BlockSpec((tm, d), lambda i: (i, 0)),
        out_shape=jax.ShapeDtypeStruct((rows, d), F32),
        compiler_params=_cparams(1),
        name="dense_ffn",
    )(x, g, wg, wu, wda, wdb)


def _sb_suffix(z, mask, u_ref):
    zb = z.astype(BF16)
    one, zero = jnp.asarray(1.0, BF16), jnp.asarray(0.0, BF16)
    sp = jnp.maximum(zb, zero) + jnp.log(one + jnp.exp2(jnp.abs(zb) * jnp.asarray(-LOG2E, BF16)))
    if mask is not None:
        sp = jnp.where(mask, sp, zero)
    return _dot(sp, u_ref[...])


def _sb_kernel(*refs, tq, tk, hd, n_meta, scale):
    z_scr, cs_scr, run_scr, acc_scr = refs[-4:]
    if n_meta:
        q_ref, k_ref, v_ref, km_ref, vm_ref, u_ref, o_ref = refs[:-4]
    else:
        q_ref, k_ref, v_ref, u_ref, o_ref = refs[:-4]
    s_len = q_ref.shape[1]
    n_hp = q_ref.shape[2] // LANES
    psl = lambda p: slice(p * LANES, (p + 1) * LANES)
    n_q = s_len // tq
    n_diag = tq // tk
    has_meta = 1 if n_meta else 0
    lane = lax.broadcasted_iota(jnp.int32, (tq, LANES), 1)
    head0 = lane < hd
    row_in = lax.broadcasted_iota(jnp.int32, (tq, tk), 0)
    col_in = lax.broadcasted_iota(jnp.int32, (tq, tk), 1)
    MKEYS = "meta"

    def q_tile(i, n_real, static_tiles):
        r0 = pl.multiple_of(i * tq, tq)
        qh = []
        for p in range(n_hp):
            q = q_ref[0, pl.ds(r0, tq), psl(p)] * scale
            qh += [jnp.where(head0, q, jnp.zeros_like(q)), jnp.where(head0, jnp.zeros_like(q), q)]

        def offset(n):
            return pl.multiple_of(r0 + (n_diag - 1 - n) * tk, tk)

        def mask_of(n):
            if n is MKEYS:
                return col_in < n_meta
            if isinstance(n, int) and n < n_diag:
                return (col_in + (n_diag - 1 - n) * tk) < row_in
            return None

        def load(n, ref, mref):
            return mref[...] if n is MKEYS else ref[0, pl.ds(offset(n), tk), :]

        def step(par, t1, t2, t3):
            if t3 is not None:
                v, m = load(t3, v_ref, vm_ref if n_meta else None), mask_of(t3)
                for h in range(2 * n_hp):
                    g = run_scr[h] + cs_scr[h]
                    a = jnp.exp(z_scr[par, h] + g)
                    if m is not None:
                        a = jnp.where(m, a, 0.0)
                    run_scr[h] = jnp.broadcast_to(g[:, 0:1], g.shape)
                    acc_scr[h] += _dot(a.astype(BF16), v[:, psl(h // 2)])
            if t2 is not None:
                m = mask_of(t2)
                for h in range(2 * n_hp):
                    cs_scr[h] = _sb_suffix(z_scr[1 - par, h], m, u_ref)
            if t1 is not None:
                k = load(t1, k_ref, km_ref if n_meta else None)
                for h in range(2 * n_hp):
                    z_scr[par, h] = _dot_nt(qh[h], k[:, psl(h // 2)])

        run_scr[...] = jnp.zeros_like(run_scr)
        acc_scr[...] = jnp.zeros_like(acc_scr)

        if static_tiles:
            tiles = list(range(n_real)) + ([MKEYS] if has_meta else [])
            for s in range(len(tiles) + 2):
                pick = lambda j: tiles[j] if 0 <= j < len(tiles) else None
                step(s % 2, pick(s), pick(s - 1), pick(s - 2))
        else:
            n_pro = n_diag + 2
            assert n_pro % 2 == 0
            for s in range(n_pro):
                pick = lambda j: j if j >= 0 else None
                step(s % 2, pick(s), pick(s - 1), pick(s - 2))

            def body(j, carry):
                s = n_pro + 2 * j
                step(0, s, s - 1, s - 2)
                step(1, s + 1, s, s - 1)
                return carry

            lax.fori_loop(0, (n_real - n_pro) // 2, body, 0)
            last = n_real - 1
            step(0, MKEYS if has_meta else None, last, last - 1)
            step(1, None, MKEYS if has_meta else None, last)
            if has_meta:
                step(0, None, None, MKEYS)

        for p in range(n_hp):
            o_ref[0, pl.ds(r0, tq), psl(p)] = jnp.where(
                head0, acc_scr[2 * p], acc_scr[2 * p + 1]).astype(o_ref.dtype)

    q_tile(0, n_diag, True)
    if n_q > 1:
        assert n_diag >= 2, "the pipelined prologue needs two diagonal key tiles per query tile"

        def outer(i, carry):
            q_tile(i, n_diag * (i + 1), False)
            return carry

        lax.fori_loop(1, n_q, outer, 0)


def _suffix_matrix(t):
    r = jnp.arange(t)
    return -(r[:, None] >= r[None, :]).astype(BF16)


def stickbreak_attention(qkv, qkv_meta, d_model, hd):
    b, s_len, _ = qkv.shape
    n_pair = d_model // LANES
    tq = min(SB_TQ, s_len)
    tk = min(ATT_TK, s_len)
    n_meta = 0 if qkv_meta is None else qkv_meta.shape[0]
    n_hp = ATT_PAIRS if n_pair % ATT_PAIRS == 0 else 1
    width = n_hp * LANES
    n_grp = n_pair // n_hp
    col = lambda which: (lambda bi, hp: (bi, 0, which * n_grp + hp))
    in_specs = [pl.BlockSpec((1, s_len, width), col(0)),
                pl.BlockSpec((1, s_len, width), col(1)),
                pl.BlockSpec((1, s_len, width), col(2))]
    args = [qkv, qkv, qkv]
    if n_meta:
        assert n_meta <= tk
        meta_pad = jnp.zeros((tk, qkv_meta.shape[1]), qkv_meta.dtype).at[:n_meta].set(qkv_meta)
        in_specs += [pl.BlockSpec((tk, width), lambda bi, hp: (0, n_grp + hp)),
                     pl.BlockSpec((tk, width), lambda bi, hp: (0, 2 * n_grp + hp))]
        args += [meta_pad, meta_pad]
    in_specs.append(pl.BlockSpec((tk, tk), lambda bi, hp: (0, 0)))
    args.append(_suffix_matrix(tk))
    kern = functools.partial(_sb_kernel, tq=tq, tk=tk, hd=hd, n_meta=n_meta, scale=hd ** -0.5)
    nh = 2 * n_hp
    return pl.pallas_call(
        kern,
        grid=(b, n_grp),
        in_specs=in_specs,
        out_specs=pl.BlockSpec((1, s_len, width), lambda bi, hp: (bi, 0, hp)),
        out_shape=jax.ShapeDtypeStruct((b, s_len, d_model), BF16),
        scratch_shapes=[pltpu.VMEM((2, nh, tq, tk), F32), pltpu.VMEM((nh, tq, tk), F32),
                        pltpu.VMEM((nh, tq, tk), F32), pltpu.VMEM((nh, tq, LANES), F32)],
        compiler_params=_cparams(2),
        name="stickbreak_attention",
    )(*args)


def _log_sigmoid(y):
    return jnp.minimum(y, 0.0) - jnp.log(1.0 + jnp.exp(-jnp.abs(y)))


def _tri_cumsum(tri, x):
    h1, h2, h3 = _split3(x)
    return _dot(tri, h1) + _dot(tri, h2) + _dot(tri, h3)


def _forget_cumsum_kernel(f_ref, fm_ref, b_ref, tri_ref, trim_ref, o_ref, om_ref, *, chunk, n_heads):
    bias = b_ref[...]

    def packed(f):
        lane = lax.broadcasted_iota(jnp.int32, f.shape, 1)
        out = None
        for a, part in enumerate(_split3(f)):
            pa = jnp.where(lane < n_heads, part.astype(F32), 0.0)
            pa = pa if a == 0 else pltpu.roll(pa, a * n_heads, 1)
            out = pa if out is None else out + pa
        return out.astype(BF16)

    fm = _tri_cumsum(trim_ref[...], _log_sigmoid(fm_ref[...] + bias))
    om_ref[...] = packed(fm)
    n_meta = fm.shape[0]
    carry = fm[n_meta - 1:n_meta, :]
    s_len = f_ref.shape[1]
    for c0 in range(0, s_len, chunk):
        fc = _tri_cumsum(tri_ref[...], _log_sigmoid(f_ref[0, c0:c0 + chunk, :] + bias)) + carry
        o_ref[0, c0:c0 + chunk, :] = packed(fc)
        carry = fc[chunk - 1:chunk, :]


def forget_cumsum(f_logit, f_logit_meta, bias, n_heads):
    b, s_len, w = f_logit.shape
    n_meta = f_logit_meta.shape[0]
    assert 3 * n_heads <= w
    chunk = min(CUM_CHUNK, s_len)
    tri = lambda t: (jnp.arange(t)[:, None] >= jnp.arange(t)[None, :]).astype(BF16)
    return pl.pallas_call(
        functools.partial(_forget_cumsum_kernel, chunk=chunk, n_heads=n_heads),
        grid=(b,),
        in_specs=[
            pl.BlockSpec((1, s_len, w), lambda bi: (bi, 0, 0)),
            pl.BlockSpec((n_meta, w), lambda bi: (0, 0)),
            pl.BlockSpec((1, w), lambda bi: (0, 0)),
            pl.BlockSpec((chunk, chunk), lambda bi: (0, 0)),
            pl.BlockSpec((n_meta, n_meta), lambda bi: (0, 0)),
        ],
        out_specs=[pl.BlockSpec((1, s_len, w), lambda bi: (bi, 0, 0)),
                   pl.BlockSpec((n_meta, w), lambda bi: (0, 0))],
        out_shape=[jax.ShapeDtypeStruct((b, s_len, w), BF16),
                   jax.ShapeDtypeStruct((n_meta, w), BF16)],
        compiler_params=_cparams(1),
        name="forget_cumsum",
    )(f_logit, f_logit_meta, bias, tri(chunk), tri(n_meta))


def _fox_kernel(qc_ref, kc_ref, vc_ref, f_ref, kmc_ref, vmc_ref, fm_ref, pq_ref, pk_ref, pv_ref,
                cq_ref, ck_ref, cv_ref, o_ref,
                q_ref, k_ref, v_ref, km_ref, vm_ref, z_scr, p_scr, m_scr, alpha_scr, acc_scr,
                *, tq, tk, hd, n_meta):
    s_len = qc_ref.shape[1]
    n_q = s_len // tq
    n_diag = tq // tk
    lane = lax.broadcasted_iota(jnp.int32, (tq, LANES), 1)
    head0 = lane < hd
    row_in = lax.broadcasted_iota(jnp.int32, (tq, tk), 0)
    col_in = lax.broadcasted_iota(jnp.int32, (tq, tk), 1)
    MKEYS = "meta"
    n_hp = qc_ref.shape[2] // LANES
    n_h = 2 * n_hp
    hsl = lambda h: slice(h * LANES, (h + 1) * LANES)
    psl = lambda p: slice(p * LANES, (p + 1) * LANES)
    xsl = lambda p: slice(2 * p * LANES, 2 * (p + 1) * LANES)

    def place(main, fpacked, p_mat, c_ref):
        lhs = main if fpacked is None else jnp.concatenate([main, fpacked], axis=1)
        return (_dot(lhs, p_mat) + c_ref[...]).astype(BF16)

    fpm = fm_ref[...]
    for p in range(n_hp):
        for c0 in range(0, s_len, tq):
            rows = slice(c0, c0 + tq)
            fp = f_ref[0, rows, :]
            q_ref[rows, xsl(p)] = place(qc_ref[0, rows, psl(p)], fp, pq_ref[p], cq_ref)
            k_ref[rows, xsl(p)] = place(kc_ref[0, rows, psl(p)], fp, pk_ref[p], ck_ref)
            v_ref[rows, xsl(p)] = place(vc_ref[0, rows, psl(p)], None, pv_ref[0], cv_ref)
        km_ref[:, xsl(p)] = place(kmc_ref[:, psl(p)], fpm, pk_ref[p], ck_ref)
        vm_ref[:, xsl(p)] = place(vmc_ref[:, psl(p)], None, pv_ref[0], cv_ref)

    def q_tile(i, n_real, static_tiles):
        r0 = pl.multiple_of(i * tq, tq)
        qh = [q_ref[pl.ds(r0, tq), hsl(h)] for h in range(n_h)]

        def mask_of(t):
            if t is MKEYS:
                return col_in < n_meta
            return None if t[1] is None else (col_in + t[1] * tk) <= row_in

        def load(t, ref, mref, h):
            if t is MKEYS:
                return mref[:, hsl(h)]
            return ref[pl.ds(pl.multiple_of(t[0] * tk, tk), tk), hsl(h)]

        def step(par, t1, t2, t3):
            if t3 is not None:
                for h in range(n_h):
                    acc_scr[h] = acc_scr[h] * alpha_scr[h] + _dot(p_scr[h], load(t3, v_ref, vm_ref, h))
            if t2 is not None:
                m = mask_of(t2)
                for h in range(n_h):
                    z = z_scr[1 - par, h]
                    if m is not None:
                        z = jnp.where(m, z, -jnp.inf)
                    m_old = m_scr[h]
                    m_new = jnp.maximum(m_old, jnp.max(z, axis=-1, keepdims=True))
                    p_scr[h] = jnp.exp(z - m_new).astype(BF16)
                    alpha_scr[h] = jnp.exp(m_old - m_new)[:, :LANES]
                    m_scr[h] = m_new
            if t1 is not None:
                for h in range(n_h):
                    z_scr[par, h] = _dot_nt(qh[h], load(t1, k_ref, km_ref, h))

        m_scr[...] = jnp.full(m_scr.shape, -jnp.inf, F32)
        acc_scr[...] = jnp.zeros_like(acc_scr)

        if static_tiles:
            tiles = [MKEYS] + [(n, n - (n_real - n_diag) if n >= n_real - n_diag else None)
                              for n in range(n_real)]
            for s in range(len(tiles) + 2):
                pick = lambda j: tiles[j] if 0 <= j < len(tiles) else None
                step(s % 2, pick(s), pick(s - 1), pick(s - 2))
        else:
            pos = lambda p: MKEYS if (isinstance(p, int) and p == 0) else (p - 1, None)
            n_pro = 4
            for s in range(n_pro):
                pick = lambda j: pos(j) if j >= 0 else None
                step(s % 2, pick(s), pick(s - 1), pick(s - 2))

            def body(j, carry):
                s = n_pro + 2 * j
                step(0, pos(s), pos(s - 1), pos(s - 2))
                step(1, pos(s + 1), pos(s), pos(s - 1))
                return carry

            lax.fori_loop(0, (n_real - n_pro) // 2, body, 0)
            a, b = n_real - 2, n_real - 1
            step(0, (b, None), (a, 0), (a - 1, None))
            step(1, None, (b, 1), (a, None))
            step(0, None, None, (b, None))

        for p in range(n_hp):
            res0, res1 = acc_scr[2 * p], acc_scr[2 * p + 1]
            rot0 = pltpu.roll(res0, hd, 1)
            rot1 = pltpu.roll(res1, hd, 1)
            o_ref[0, pl.ds(r0, tq), psl(p)] = jnp.where(
                head0, res0 / rot0, rot1 / res1).astype(o_ref.dtype)

    q_tile(0, n_diag, True)
    if n_q > 1:
        assert n_diag == 2, "the pipelined sweep is written for two diagonal key tiles per query tile"

        def outer(i, carry):
            q_tile(i, n_diag * (i + 1), False)
            return carry

        lax.fori_loop(1, n_q, outer, 0)


def _fox_placement(n_pair, hd):
    n_heads = 2 * n_pair
    pq = np.zeros((n_pair, 2 * LANES, 2 * LANES), np.float32)
    pk = np.zeros((n_pair, 2 * LANES, 2 * LANES), np.float32)
    pv = np.zeros((1, LANES, 2 * LANES), np.float32)
    cq = np.zeros((1, 2 * LANES), np.float32)
    ck = np.zeros((1, 2 * LANES), np.float32)
    cv = np.zeros((1, 2 * LANES), np.float32)
    for h in range(2):
        base = h * LANES
        for j in range(hd):
            pq[:, h * hd + j, base + j] = hd ** -0.5
            pk[:, h * hd + j, base + j] = 1.0
            pv[:, h * hd + j, base + j] = 1.0
        for p in range(3):
            for hp in range(n_pair):
                pq[hp, LANES + p * n_heads + 2 * hp + h, base + hd + p] = 1.0
                pk[hp, LANES + p * n_heads + 2 * hp + h, base + hd + 3 + p] = -1.0
        cq[0, base + hd + 3:base + hd + 6] = 1.0
        ck[0, base + hd:base + hd + 3] = 1.0
        cv[0, base + hd:base + LANES] = 1.0
    bf = lambda a: jnp.asarray(a, BF16)
    return bf(pq), bf(pk), bf(pv), jnp.asarray(cq), jnp.asarray(ck), jnp.asarray(cv)


def forgetting_attention(q, kv, kv_meta, fparts, fparts_meta, d_model, hd):
    b, s_len, _ = q.shape
    n_pair = d_model // LANES
    n_meta = kv_meta.shape[0]
    tq = min(ATT_TQ, s_len)
    tk = min(ATT_TK, s_len)
    assert n_meta <= tk and tk == LANES and 3 * 2 * n_pair <= LANES
    kvm = jnp.zeros((tk, kv_meta.shape[1]), kv_meta.dtype).at[:n_meta].set(kv_meta)
    fpm = jnp.zeros((tk, LANES), fparts_meta.dtype).at[:n_meta].set(fparts_meta)
    pq, pk, pv, cq, ck, cv = _fox_placement(n_pair, hd)
    n_hp = FOX_PAIRS if n_pair % FOX_PAIRS == 0 else 1
    width = n_hp * LANES
    n_grp = n_pair // n_hp
    nh = 2 * n_hp
    blk = lambda off: pl.BlockSpec((1, s_len, width), lambda bi, hp: (bi, 0, off + hp))
    mblk = lambda off: pl.BlockSpec((tk, width), lambda bi, hp: (0, off + hp))
    crow = pl.BlockSpec((1, 2 * LANES), lambda bi, hp: (0, 0))
    in_specs = [blk(0), blk(0), blk(n_grp),
                pl.BlockSpec((1, s_len, LANES), lambda bi, hp: (bi, 0, 0)),
                mblk(0), mblk(n_grp),
                pl.BlockSpec((tk, LANES), lambda bi, hp: (0, 0)),
                pl.BlockSpec((n_hp, 2 * LANES, 2 * LANES), lambda bi, hp: (hp, 0, 0)),
                pl.BlockSpec((n_hp, 2 * LANES, 2 * LANES), lambda bi, hp: (hp, 0, 0)),
                pl.BlockSpec((1, LANES, 2 * LANES), lambda bi, hp: (0, 0, 0)),
                crow, crow, crow]
    ext = lambda rows: pltpu.VMEM((rows, nh * LANES), BF16)
    return pl.pallas_call(
        functools.partial(_fox_kernel, tq=tq, tk=tk, hd=hd, n_meta=n_meta),
        grid=(b, n_grp),
        in_specs=in_specs,
        out_specs=pl.BlockSpec((1, s_len, width), lambda bi, hp: (bi, 0, hp)),
        out_shape=jax.ShapeDtypeStruct((b, s_len, d_model), BF16),
        scratch_shapes=[ext(s_len), ext(s_len), ext(s_len), ext(tk), ext(tk),
                        pltpu.VMEM((2, nh, tq, tk), F32), pltpu.VMEM((nh, tq, tk), BF16),
                        pltpu.VMEM((nh, tq, tk), F32), pltpu.VMEM((nh, tq, LANES), F32),
                        pltpu.VMEM((nh, tq, LANES), F32)],
        compiler_params=_cparams(2),
        name="forgetting_attention",
    )(q, kv, kv, fparts, kvm, kvm, fpm, pq, pk, pv, cq, ck, cv)


def _router_kernel(a_ref, wo_ref, r_ref, g_ref, wr_ref, h_ref, xn_ref, id_ref, g1_ref, g2_ref,
                   *, n_experts, n_chunk):
    a = a_ref[...]
    for c0 in range(0, wo_ref.shape[1], n_chunk):
        h_ref[:, c0:c0 + n_chunk] = r_ref[:, c0:c0 + n_chunk] + _dot(a, wo_ref[:, c0:c0 + n_chunk])
    xn = _rms(h_ref[...], g_ref[...])
    xn_ref[...] = xn.astype(BF16)
    a1, a2 = _split2(xn)
    b1, b2 = _split2(wr_ref[...])
    logits = _dot(a1, b1) + _dot(a1, b2) + _dot(a2, b1)
    lane = lax.broadcasted_iota(jnp.int32, logits.shape, 1)
    lg = jnp.where(lane < n_experts, logits, -jnp.inf)
    m1 = jnp.max(lg, axis=-1, keepdims=True)
    i1 = jnp.min(jnp.where(lg == m1, lane, LANES), axis=-1, keepdims=True)
    lg2 = jnp.where(lane == i1, -jnp.inf, lg)
    m2 = jnp.max(lg2, axis=-1, keepdims=True)
    i2 = jnp.min(jnp.where(lg2 == m2, lane, LANES), axis=-1, keepdims=True)
    e2 = jnp.exp(m2 - m1)
    den = 1.0 + e2
    id_ref[...] = jnp.where(lane == 0, i1, jnp.where(lane == 1, i2, 0))
    g1_ref[...] = jnp.broadcast_to(1.0 / den, logits.shape)
    g2_ref[...] = jnp.broadcast_to(e2 / den, logits.shape)


def moe_router(a, w_o, res, g, w_router):
    rows, d = res.shape
    k = a.shape[1]
    n_experts = w_router.shape[1]
    wr = jnp.zeros((d, LANES), F32).at[:, :n_experts].set(w_router)
    tm = _row_tile(rows)
    row_blk = lambda w: pl.BlockSpec((tm, w), lambda i: (i, 0))
    return pl.pallas_call(
        functools.partial(_router_kernel, n_experts=n_experts, n_chunk=_col_chunk(d)),
        grid=(rows // tm,),
        in_specs=[
            row_blk(k),
            pl.BlockSpec((k, d), lambda i: (0, 0)),
            row_blk(d),
            pl.BlockSpec((1, d), lambda i: (0, 0)),
            pl.BlockSpec((d, LANES), lambda i: (0, 0)),
        ],
        out_specs=[row_blk(d), row_blk(d), row_blk(LANES), row_blk(LANES), row_blk(LANES)],
        out_shape=[jax.ShapeDtypeStruct((rows, d), F32),
                   jax.ShapeDtypeStruct((rows, d), BF16),
                   jax.ShapeDtypeStruct((rows, LANES), jnp.int32),
                   jax.ShapeDtypeStruct((rows, LANES), F32),
                   jax.ShapeDtypeStruct((rows, LANES), F32)],
        compiler_params=_cparams(1),
        name="moe_router",
    )(a, w_o, res, g, wr)


def _moe_ffn_kernel(te_ref, nt_ref, x_ref, wg_ref, wu_ref, wda_ref, wdb_ref, o_ref, *, f_chunk):
    i = pl.program_id(0)

    @pl.when(i < nt_ref[0])
    def _():
        acc_a, acc_b = _swiglu_acc(x_ref[...], wg_ref, wu_ref, wda_ref, wdb_ref, f_chunk, widx=(0,))
        half = acc_a.shape[1]
        o_ref[:, :half] = acc_a.astype(o_ref.dtype)
        o_ref[:, half:] = acc_b.astype(o_ref.dtype)

    @pl.when(i >= nt_ref[0])
    def _():
        o_ref[...] = jnp.zeros_like(o_ref)


def moe_grouped_ffn(xs, tile_expert, n_tiles_used, wg, wu, wda, wdb):
    p_rows, d = xs.shape
    n_experts, _, ff = wg.shape
    tm = MOE_TM
    expert = lambda shape: pl.BlockSpec(shape, lambda i, te, nt: (te[i], 0, 0))
    grid_spec = pltpu.PrefetchScalarGridSpec(
        num_scalar_prefetch=2,
        grid=(p_rows // tm,),
        in_specs=[
            pl.BlockSpec((tm, d), lambda i, te, nt: (i, 0)),
            expert((1, d, ff)), expert((1, d, ff)), expert((1, ff, d // 2)), expert((1, ff, d // 2)),
        ],
        out_specs=pl.BlockSpec((tm, d), lambda i, te, nt: (i, 0)),
    )
    return pl.pallas_call(
        functools.partial(_moe_ffn_kernel, f_chunk=FF_CHUNK),
        grid_spec=grid_spec,
        out_shape=jax.ShapeDtypeStruct((p_rows, d), BF16),
        compiler_params=_cparams(1),
        name="moe_grouped_ffn",
    )(tile_expert, n_tiles_used, xs, wg, wu, wda, wdb)


def _combine_kernel(h_ref, y1_ref, y2_ref, g1_ref, g2_ref, o_ref):
    d = h_ref.shape[1]
    g1, g2 = g1_ref[...], g2_ref[...]
    for c0 in range(0, d, LANES):
        cs = slice(c0, c0 + LANES)
        o_ref[:, cs] = (h_ref[:, cs] + g1 * y1_ref[:, cs].astype(F32)
                        + g2 * y2_ref[:, cs].astype(F32))


def moe_block(a, w_o, res, g, w_router, ffn_weights):
    rows, d = res.shape
    n_experts = w_router.shape[1]
    tm = MOE_TM
    h, xn, ids, g1, g2 = moe_router(a, w_o, res, g, w_router)

    eid = ids[:, :TOP_K]
    sel = (eid[:, :, None] == jnp.arange(n_experts)[None, None, :]).any(axis=1)
    rank = jnp.cumsum(sel.astype(jnp.int32), axis=0) - 1
    cnt = rank[-1] + 1
    cnt_pad = ((cnt + tm - 1) // tm) * tm
    ends = jnp.cumsum(cnt_pad)
    offs = ends - cnt_pad
    pos = offs[eid] + jnp.take_along_axis(rank, eid, axis=1)
    p_rows = rows * TOP_K + n_experts * tm
    n_tiles = p_rows // tm
    tile_start = jnp.arange(n_tiles, dtype=jnp.int32) * tm
    tile_expert = jnp.minimum((tile_start[:, None] >= ends[None, :]).sum(axis=1), n_experts - 1)
    n_used = (ends[-1] // tm).astype(jnp.int32).reshape(1)

    tok = jnp.arange(rows, dtype=jnp.int32)[:, None]
    tok_c = jnp.sort((eid * rows + tok).reshape(-1)) % rows
    tok_c = jnp.concatenate([tok_c, jnp.zeros((p_rows - rows * TOP_K,), jnp.int32)])
    cstart = jnp.cumsum(cnt) - cnt
    slot = jnp.arange(p_rows, dtype=jnp.int32)
    tok_sorted = jnp.zeros((p_rows,), jnp.int32)
    for e in range(n_experts):
        inside = (slot >= offs[e]) & (slot < offs[e] + cnt[e])
        tok_sorted = jnp.where(inside, jnp.roll(tok_c, offs[e] - cstart[e]), tok_sorted)
    xs = xn.at[tok_sorted].get(mode="promise_in_bounds")
    ys = moe_grouped_ffn(xs, tile_expert.astype(jnp.int32), n_used, *ffn_weights)
    y1 = ys.at[pos[:, 0]].get(mode="promise_in_bounds", unique_indices=True)
    y2 = ys.at[pos[:, 1]].get(mode="promise_in_bounds", unique_indices=True)
    tr = _row_tile(rows)
    spec = pl.BlockSpec((tr, d), lambda i: (i, 0))
    gspec = pl.BlockSpec((tr, LANES), lambda i: (i, 0))
    return pl.pallas_call(
        _combine_kernel,
        grid=(rows // tr,),
        in_specs=[spec, spec, spec, gspec, gspec],
        out_specs=spec,
        out_shape=jax.ShapeDtypeStruct((rows, d), F32),
        compiler_params=_cparams(1),
        name="moe_combine",
    )(h, y1, y2, g1, g2)


def kernel(x, meta_tokens, norm_attn_a, w_qkv_a, w_o_a, norm_kv, w_kvf, b_f, k_norm, norm_attn_b,
           w_q_b, q_norm_b, w_o_b, norm_ffn_dense, w_gu_dense, w_down_dense, norm_ffn_moe,
           w_router, w_gu_moe, w_down_moe):
    b, s_len, d = x.shape
    n_heads = b_f.shape[0]
    hd = d // n_heads
    n_meta = meta_tokens.shape[0]
    depth = norm_attn_a.shape[0] + norm_attn_b.shape[0]
    n_a = norm_attn_a.shape[0]
    assert 2 * hd == LANES and d % LANES == 0
    row = lambda v: v.reshape(1, -1).astype(F32)

    h = x.reshape(b * s_len, d)
    hm = meta_tokens.astype(x.dtype)
    for layer in range(depth):
        last = layer == depth - 1
        if layer < n_a:
            w_qkv = to_bf16(w_qkv_a[layer])
            g = row(norm_attn_a[layer])
            qkv = norm_matmul(h, g, w_qkv, BF16).reshape(b, s_len, 3 * d)
            qkv_m = norm_matmul(hm, g, w_qkv, BF16)
            o = stickbreak_attention(qkv, qkv_m, d, hd).reshape(b * s_len, d)
            o_m = stickbreak_attention(qkv_m[None], None, d, hd)[0]
            w_o = to_bf16(w_o_a[layer])
            hm = matmul_residual(o_m, w_o, hm)
        else:
            if layer == n_a:
                g = row(norm_kv)
                w_kv = to_bf16(w_kvf[:, :2 * d])
                w_f = jnp.zeros((d, LANES), F32).at[:, :n_heads].set(w_kvf[:, 2 * d:]).astype(BF16)
                kgain = row(jnp.tile(k_norm, n_heads))
                shared = [(g, w_kv, BF16, kgain, hd), (g, w_f, F32, None, None)]
            i = layer - n_a
            q_proj = (row(norm_attn_b[i]), to_bf16(w_q_b[i]), BF16,
                      row(jnp.tile(q_norm_b[i], n_heads)), hd)
            if layer == n_a:
                kv, fl, q = norm_projections(h, shared + [q_proj])
                kv_m, fl_m = norm_projections(hm, shared)
                bias = jnp.zeros((1, LANES), F32).at[0, :n_heads].set(b_f.astype(F32))
                f_cum, f_cum_m = forget_cumsum(fl.reshape(b, s_len, LANES), fl_m, bias, n_heads)
                kv_sh = kv.reshape(b, s_len, 2 * d)
            else:
                q, = norm_projections(h, [q_proj])
            o = forgetting_attention(q.reshape(b, s_len, d), kv_sh, kv_m, f_cum, f_cum_m, d, hd)
            o = o.reshape(b * s_len, d)
            w_o = to_bf16(w_o_b[i])
            if not last:
                raise NotImplementedError("meta-row queries in forgetting layers before the last")
        j = layer // 2
        if layer % 2 == 0:
            ffn_w = to_bf16(w_gu_dense[j], splits=2) + to_bf16(w_down_dense[j], splits=2)
            g = row(norm_ffn_dense[j])
            h = dense_ffn(matmul_residual(o, w_o, h), g, *ffn_w)
            if not last:
                hm = dense_ffn(hm, g, *ffn_w)
        else:
            ffn_w = to_bf16(w_gu_moe[j], splits=2) + to_bf16(w_down_moe[j], splits=2)
            h = moe_block(o, w_o, h, row(norm_ffn_moe[j]), w_router[j], ffn_w)
            if not last:
                raise NotImplementedError("meta rows through a MoE layer before the last")
    return h.reshape(b, s_len, d)
```

```python
import functools

import jax
import jax.numpy as jnp
import numpy as np
from jax import lax
from jax.experimental import pallas as pl
from jax.experimental.pallas import tpu as pltpu

F32 = jnp.float32
BF16 = jnp.bfloat16

EPS = 1e-6
LOG2E = 1.4426950408889634
TOP_K = 2
LANES = 128
MXU_WIDTH = 256
VMEM_LIMIT = 56 * 1024 * 1024
ROW_TILE = 512
COL_CHUNK = 512
FF_CHUNK = 256
ATT_TQ = 256
ATT_TK = 128
ATT_PAIRS = 4
FOX_PAIRS = 4
MOE_TM = 512
CUM_CHUNK = 256
CAST_BLOCK_BYTES = 12 * 1024 * 1024


def _cparams(n_axes):
    return pltpu.CompilerParams(dimension_semantics=("arbitrary",) * n_axes,
                                vmem_limit_bytes=VMEM_LIMIT)


def _split3(x):
    h1 = x.astype(BF16)
    r1 = x - h1.astype(F32)
    h2 = r1.astype(BF16)
    h3 = (r1 - h2.astype(F32)).astype(BF16)
    return h1, h2, h3


def _split2(x):
    h1 = x.astype(BF16)
    h2 = (x - h1.astype(F32)).astype(BF16)
    return h1, h2


def _dot(a, b):
    return jnp.dot(a, b, preferred_element_type=F32)


def _dot_nt(a, b):
    return lax.dot_general(a, b, (((1,), (1,)), ((), ())), preferred_element_type=F32)


def _rms(xf, g):
    ms = jnp.mean(xf * xf, axis=-1, keepdims=True)
    return xf * lax.rsqrt(ms + EPS) * g


def _row_tile(rows, pref=ROW_TILE):
    return pref if rows % pref == 0 else rows


def _col_chunk(*widths):
    c = COL_CHUNK
    while any(w % c for w in widths):
        c //= 2
    assert c >= LANES
    return c


def _cast_kernel(*refs):
    n = len(refs) // 2
    for x_ref, o_ref in zip(refs[:n], refs[n:]):
        o_ref[...] = x_ref[...].astype(o_ref.dtype)


def to_bf16(w, splits=1):
    shape = w.shape
    cols = shape[-1]
    assert cols % (splits * LANES) == 0
    sc = cols // splits
    w2 = w.reshape(-1, cols)
    rows = w2.shape[0]
    rb = rows
    for cand in (2048, 1024, 512, 256, 128, 64):
        if rows % cand == 0 and cand * cols * 4 <= CAST_BLOCK_BYTES:
            rb = cand
            break
    spec = lambda s: pl.BlockSpec((rb, sc), functools.partial(lambda i, s: (i, s), s=s))
    outs = pl.pallas_call(
        _cast_kernel,
        grid=(rows // rb,),
        in_specs=[spec(s) for s in range(splits)],
        out_specs=[pl.BlockSpec((rb, sc), lambda i: (i, 0)) for _ in range(splits)],
        out_shape=[jax.ShapeDtypeStruct((rows, sc), BF16) for _ in range(splits)],
        compiler_params=_cparams(1),
        name="to_bf16",
    )(*([w2] * splits))
    outs = [o.reshape(shape[:-1] + (sc,)) for o in outs]
    return outs[0] if splits == 1 else outs


def _norm_matmul_kernel(*refs, cfg):
    n_p = len(cfg)
    x_ref, ins, outs = refs[0], refs[1:1 + 4 * n_p], refs[1 + 4 * n_p:]
    xf = x_ref[...]
    xs = xf * lax.rsqrt(jnp.mean(xf * xf, axis=-1, keepdims=True) + EPS)
    for pi, (n_chunk, head_cols) in enumerate(cfg):
        g_ref, w_ref, hg_ref, gm_ref = ins[4 * pi:4 * pi + 4]
        o_ref = outs[pi]
        xn = (xs * g_ref[...]).astype(BF16)
        n = w_ref.shape[1]
        for c0 in range(0, n, n_chunk):
            acc = _dot(xn, w_ref[:, c0:c0 + n_chunk])
            if c0 < head_cols:
                sq = (acc * acc).astype(BF16)
                gw = gm_ref.shape[0]
                ms = jnp.concatenate([_dot(sq[:, j:j + gw], gm_ref[...])
                                      for j in range(0, n_chunk, gw)], axis=1)
                acc = acc * lax.rsqrt(ms + EPS) * hg_ref[:, c0:c0 + n_chunk]
            o_ref[:, c0:c0 + n_chunk] = acc.astype(o_ref.dtype)


def norm_projections(x, projs):
    rows, d = x.shape
    tm = _row_tile(rows)
    const = lambda shape: pl.BlockSpec(shape, lambda i: (0,) * len(shape))
    cfg, args, in_specs, out_specs, out_shape = [], [x], [pl.BlockSpec((tm, d), lambda i: (i, 0))], [], []
    for g, w, out_dtype, head_gain, head_dim in projs:
        n = w.shape[1]
        if head_gain is None:
            n_chunk = _col_chunk(n)
            head_cols = 0
            head_gain = jnp.zeros((1, n_chunk), F32)
            head_dim = min(n_chunk, MXU_WIDTH)
        else:
            head_cols = head_gain.shape[1]
            n_chunk = _col_chunk(n, head_cols)
            assert n_chunk % head_dim == 0
        gw = min(n_chunk, MXU_WIDTH)
        assert gw % head_dim == 0 and n_chunk % gw == 0
        grp = jnp.arange(gw) // head_dim
        gm = ((grp[:, None] == grp[None, :]).astype(F32) / head_dim).astype(BF16)
        cfg.append((n_chunk, head_cols))
        args += [g, w, head_gain, gm]
        in_specs += [const((1, d)), const((d, n)), const(head_gain.shape), const((gw, gw))]
        out_specs.append(pl.BlockSpec((tm, n), lambda i: (i, 0)))
        out_shape.append(jax.ShapeDtypeStruct((rows, n), out_dtype))
    return pl.pallas_call(
        functools.partial(_norm_matmul_kernel, cfg=tuple(cfg)),
        grid=(rows // tm,),
        in_specs=in_specs,
        out_specs=out_specs,
        out_shape=out_shape,
        compiler_params=_cparams(1),
        name="norm_matmul",
    )(*args)


def norm_matmul(x, g, w, out_dtype, *, head_gain=None, head_dim=None):
    return norm_projections(x, [(g, w, out_dtype, head_gain, head_dim)])[0]


def _matmul_res_kernel(a_ref, w_ref, r_ref, o_ref, *, n_chunk):
    a = a_ref[...]
    n = w_ref.shape[1]
    for c0 in range(0, n, n_chunk):
        o_ref[:, c0:c0 + n_chunk] = r_ref[:, c0:c0 + n_chunk] + _dot(a, w_ref[:, c0:c0 + n_chunk])


def matmul_residual(a, w, res):
    rows, k = a.shape
    n = w.shape[1]
    tm = _row_tile(rows)
    n_chunk = _col_chunk(n)
    return pl.pallas_call(
        functools.partial(_matmul_res_kernel, n_chunk=n_chunk),
        grid=(rows // tm,),
        in_specs=[
            pl.BlockSpec((tm, k), lambda i: (i, 0)),
            pl.BlockSpec((k, n), lambda i: (0, 0)),
            pl.BlockSpec((tm, n), lambda i: (i, 0)),
        ],
        out_specs=pl.BlockSpec((tm, n), lambda i: (i, 0)),
        out_shape=jax.ShapeDtypeStruct((rows, n), F32),
        compiler_params=_cparams(1),
        name="matmul_residual",
    )(a, w, res)


def _swiglu_acc(xn, wg_ref, wu_ref, wda_ref, wdb_ref, f_chunk, widx=()):
    ff = wg_ref.shape[-1]
    acc_a = acc_b = None
    for c0 in range(0, ff, f_chunk):
        cols = widx + (slice(None), slice(c0, c0 + f_chunk))
        rows = widx + (slice(c0, c0 + f_chunk), slice(None))
        gt = _dot(xn, wg_ref[cols])
        up = _dot(xn, wu_ref[cols])
        act = (gt * (1.0 / (1.0 + jnp.exp(-gt))) * up).astype(BF16)
        pa, pb = _dot(act, wda_ref[rows]), _dot(act, wdb_ref[rows])
        acc_a = pa if acc_a is None else acc_a + pa
        acc_b = pb if acc_b is None else acc_b + pb
    return acc_a, acc_b


def _dense_ffn_kernel(x_ref, g_ref, wg_ref, wu_ref, wda_ref, wdb_ref, o_ref, *, f_chunk):
    xf = x_ref[...]
    xn = _rms(xf, g_ref[...]).astype(BF16)
    acc_a, acc_b = _swiglu_acc(xn, wg_ref, wu_ref, wda_ref, wdb_ref, f_chunk)
    half = acc_a.shape[1]
    o_ref[:, :half] = xf[:, :half] + acc_a
    o_ref[:, half:] = xf[:, half:] + acc_b


def dense_ffn(x, g, wg, wu, wda, wdb):
    rows, d = x.shape
    ff = wg.shape[1]
    tm = _row_tile(rows)
    const = lambda shape: pl.BlockSpec(shape, lambda i: (0, 0))
    return pl.pallas_call(
        functools.partial(_dense_ffn_kernel, f_chunk=FF_CHUNK),
        grid=(rows // tm,),
        in_specs=[
            pl.BlockSpec((tm, d), lambda i: (i, 0)),
            const((1, d)), const((d, ff)), const((d, ff)), const((ff, d // 2)), const((ff, d // 2)),
        ],
        out_specs=pl.BlockSpec((tm, d), lambda i: (i, 0)),
        out_shape=jax.ShapeDtypeStruct((rows, d), F32),
        compiler_params=_cparams(1),
        name="dense_ffn",
    )(x, g, wg, wu, wda, wdb)


def _sb_suffix(z, mask, u_ref):
    zb = z.astype(BF16)
    one, zero = jnp.asarray(1.0, BF16), jnp.asarray(0.0, BF16)
    sp = jnp.maximum(zb, zero) + jnp.log(one + jnp.exp2(jnp.abs(zb) * jnp.asarray(-LOG2E, BF16)))
    if mask is not None:
        sp = jnp.where(mask, sp, zero)
    return _dot(sp, u_ref[...])


def _sb_kernel(*refs, tq, tk, hd, n_meta, scale):
    z_scr, cs_scr, run_scr, acc_scr = refs[-4:]
    if n_meta:
        q_ref, k_ref, v_ref, km_ref, vm_ref, u_ref, o_ref = refs[:-4]
    else:
        q_ref, k_ref, v_ref, u_ref, o_ref = refs[:-4]
    s_len = q_ref.shape[1]
    n_hp = q_ref.shape[2] // LANES
    psl = lambda p: slice(p * LANES, (p + 1) * LANES)
    n_q = s_len // tq
    n_diag = tq // tk
    has_meta = 1 if n_meta else 0
    lane = lax.broadcasted_iota(jnp.int32, (tq, LANES), 1)
    head0 = lane < hd
    row_in = lax.broadcasted_iota(jnp.int32, (tq, tk), 0)
    col_in = lax.broadcasted_iota(jnp.int32, (tq, tk), 1)
    MKEYS = "meta"

    def q_tile(i, n_real, static_tiles):
        r0 = pl.multiple_of(i * tq, tq)
        qh = []
        for p in range(n_hp):
            q = q_ref[0, pl.ds(r0, tq), psl(p)] * scale
            qh += [jnp.where(head0, q, jnp.zeros_like(q)), jnp.where(head0, jnp.zeros_like(q), q)]

        def offset(n):
            return pl.multiple_of(r0 + (n_diag - 1 - n) * tk, tk)

        def mask_of(n):
            if n is MKEYS:
                return col_in < n_meta
            if isinstance(n, int) and n < n_diag:
                return (col_in + (n_diag - 1 - n) * tk) < row_in
            return None

        def load(n, ref, mref):
            return mref[...] if n is MKEYS else ref[0, pl.ds(offset(n), tk), :]

        def step(par, t1, t2, t3):
            if t3 is not None:
                v, m = load(t3, v_ref, vm_ref if n_meta else None), mask_of(t3)
                for h in range(2 * n_hp):
                    g = run_scr[h] + cs_scr[h]
                    a = jnp.exp(z_scr[par, h] + g)
                    if m is not None:
                        a = jnp.where(m, a, 0.0)
                    run_scr[h] = jnp.broadcast_to(g[:, 0:1], g.shape)
                    acc_scr[h] += _dot(a.astype(BF16), v[:, psl(h // 2)])
            if t2 is not None:
                m = mask_of(t2)
                for h in range(2 * n_hp):
                    cs_scr[h] = _sb_suffix(z_scr[1 - par, h], m, u_ref)
            if t1 is not None:
                k = load(t1, k_ref, km_ref if n_meta else None)
                for h in range(2 * n_hp):
                    z_scr[par, h] = _dot_nt(qh[h], k[:, psl(h // 2)])

        run_scr[...] = jnp.zeros_like(run_scr)
        acc_scr[...] = jnp.zeros_like(acc_scr)

        if static_tiles:
            tiles = list(range(n_real)) + ([MKEYS] if has_meta else [])
            for s in range(len(tiles) + 2):
                pick = lambda j: tiles[j] if 0 <= j < len(tiles) else None
                step(s % 2, pick(s), pick(s - 1), pick(s - 2))
        else:
            n_pro = n_diag + 2
            assert n_pro % 2 == 0
            for s in range(n_pro):
                pick = lambda j: j if j >= 0 else None
                step(s % 2, pick(s), pick(s - 1), pick(s - 2))

            def body(j, carry):
                s = n_pro + 2 * j
                step(0, s, s - 1, s - 2)
                step(1, s + 1, s, s - 1)
                return carry

            lax.fori_loop(0, (n_real - n_pro) // 2, body, 0)
            last = n_real - 1
            step(0, MKEYS if has_meta else None, last, last - 1)
            step(1, None, MKEYS if has_meta else None, last)
            if has_meta:
                step(0, None, None, MKEYS)

        for p in range(n_hp):
            o_ref[0, pl.ds(r0, tq), psl(p)] = jnp.where(
                head0, acc_scr[2 * p], acc_scr[2 * p + 1]).astype(o_ref.dtype)

    q_tile(0, n_diag, True)
    if n_q > 1:
        assert n_diag >= 2, "the pipelined prologue needs two diagonal key tiles per query tile"

        def outer(i, carry):
            q_tile(i, n_diag * (i + 1), False)
            return carry

        lax.fori_loop(1, n_q, outer, 0)


def _suffix_matrix(t):
    r = jnp.arange(t)
    return -(r[:, None] >= r[None, :]).astype(BF16)


def stickbreak_attention(qkv, qkv_meta, d_model, hd):
    b, s_len, _ = qkv.shape
    n_pair = d_model // LANES
    tq = min(ATT_TQ, s_len)
    tk = min(ATT_TK, s_len)
    n_meta = 0 if qkv_meta is None else qkv_meta.shape[0]
    n_hp = ATT_PAIRS if n_pair % ATT_PAIRS == 0 else 1
    width = n_hp * LANES
    n_grp = n_pair // n_hp
    col = lambda which: (lambda bi, hp: (bi, 0, which * n_grp + hp))
    in_specs = [pl.BlockSpec((1, s_len, width), col(0)),
                pl.BlockSpec((1, s_len, width), col(1)),
                pl.BlockSpec((1, s_len, width), col(2))]
    args = [qkv, qkv, qkv]
    if n_meta:
        assert n_meta <= tk
        meta_pad = jnp.zeros((tk, qkv_meta.shape[1]), qkv_meta.dtype).at[:n_meta].set(qkv_meta)
        in_specs += [pl.BlockSpec((tk, width), lambda bi, hp: (0, n_grp + hp)),
                     pl.BlockSpec((tk, width), lambda bi, hp: (0, 2 * n_grp + hp))]
        args += [meta_pad, meta_pad]
    in_specs.append(pl.BlockSpec((tk, tk), lambda bi, hp: (0, 0)))
    args.append(_suffix_matrix(tk))
    kern = functools.partial(_sb_kernel, tq=tq, tk=tk, hd=hd, n_meta=n_meta, scale=hd ** -0.5)
    nh = 2 * n_hp
    return pl.pallas_call(
        kern,
        grid=(b, n_grp),
        in_specs=in_specs,
        out_specs=pl.BlockSpec((1, s_len, width), lambda bi, hp: (bi, 0, hp)),
        out_shape=jax.ShapeDtypeStruct((b, s_len, d_model), BF16),
        scratch_shapes=[pltpu.VMEM((2, nh, tq, tk), F32), pltpu.VMEM((nh, tq, tk), F32),
                        pltpu.VMEM((nh, tq, tk), F32), pltpu.VMEM((nh, tq, LANES), F32)],
        compiler_params=_cparams(2),
        name="stickbreak_attention",
    )(*args)


def _log_sigmoid(y):
    return jnp.minimum(y, 0.0) - jnp.log(1.0 + jnp.exp(-jnp.abs(y)))


def _tri_cumsum(tri, x):
    h1, h2, h3 = _split3(x)
    return _dot(tri, h1) + _dot(tri, h2) + _dot(tri, h3)


def _forget_cumsum_kernel(f_ref, fm_ref, b_ref, tri_ref, trim_ref, o_ref, om_ref, *, chunk, n_heads):
    bias = b_ref[...]

    def packed(f):
        lane = lax.broadcasted_iota(jnp.int32, f.shape, 1)
        out = None
        for a, part in enumerate(_split3(f)):
            pa = jnp.where(lane < n_heads, part.astype(F32), 0.0)
            pa = pa if a == 0 else pltpu.roll(pa, a * n_heads, 1)
            out = pa if out is None else out + pa
        return out.astype(BF16)

    fm = _tri_cumsum(trim_ref[...], _log_sigmoid(fm_ref[...] + bias))
    om_ref[...] = packed(fm)
    n_meta = fm.shape[0]
    carry = fm[n_meta - 1:n_meta, :]
    s_len = f_ref.shape[1]
    for c0 in range(0, s_len, chunk):
        fc = _tri_cumsum(tri_ref[...], _log_sigmoid(f_ref[0, c0:c0 + chunk, :] + bias)) + carry
        o_ref[0, c0:c0 + chunk, :] = packed(fc)
        carry = fc[chunk - 1:chunk, :]


def forget_cumsum(f_logit, f_logit_meta, bias, n_heads):
    b, s_len, w = f_logit.shape
    n_meta = f_logit_meta.shape[0]
    assert 3 * n_heads <= w
    chunk = min(CUM_CHUNK, s_len)
    tri = lambda t: (jnp.arange(t)[:, None] >= jnp.arange(t)[None, :]).astype(BF16)
    return pl.pallas_call(
        functools.partial(_forget_cumsum_kernel, chunk=chunk, n_heads=n_heads),
        grid=(b,),
        in_specs=[
            pl.BlockSpec((1, s_len, w), lambda bi: (bi, 0, 0)),
            pl.BlockSpec((n_meta, w), lambda bi: (0, 0)),
            pl.BlockSpec((1, w), lambda bi: (0, 0)),
            pl.BlockSpec((chunk, chunk), lambda bi: (0, 0)),
            pl.BlockSpec((n_meta, n_meta), lambda bi: (0, 0)),
        ],
        out_specs=[pl.BlockSpec((1, s_len, w), lambda bi: (bi, 0, 0)),
                   pl.BlockSpec((n_meta, w), lambda bi: (0, 0))],
        out_shape=[jax.ShapeDtypeStruct((b, s_len, w), BF16),
                   jax.ShapeDtypeStruct((n_meta, w), BF16)],
        compiler_params=_cparams(1),
        name="forget_cumsum",
    )(f_logit, f_logit_meta, bias, tri(chunk), tri(n_meta))


def _fox_kernel(qc_ref, kc_ref, vc_ref, f_ref, kmc_ref, vmc_ref, fm_ref, pq_ref, pk_ref, pv_ref,
                cq_ref, ck_ref, cv_ref, o_ref,
                q_ref, k_ref, v_ref, km_ref, vm_ref, z_scr, p_scr, m_scr, alpha_scr, acc_scr,
                *, tq, tk, hd, n_meta):
    s_len = qc_ref.shape[1]
    n_q = s_len // tq
    n_diag = tq // tk
    lane = lax.broadcasted_iota(jnp.int32, (tq, LANES), 1)
    head0 = lane < hd
    row_in = lax.broadcasted_iota(jnp.int32, (tq, tk), 0)
    col_in = lax.broadcasted_iota(jnp.int32, (tq, tk), 1)
    MKEYS = "meta"
    n_hp = qc_ref.shape[2] // LANES
    n_h = 2 * n_hp
    hsl = lambda h: slice(h * LANES, (h + 1) * LANES)
    psl = lambda p: slice(p * LANES, (p + 1) * LANES)
    xsl = lambda p: slice(2 * p * LANES, 2 * (p + 1) * LANES)

    def place(main, fpacked, p_mat, c_ref):
        lhs = main if fpacked is None else jnp.concatenate([main, fpacked], axis=1)
        return (_dot(lhs, p_mat) + c_ref[...]).astype(BF16)

    fpm = fm_ref[...]
    for p in range(n_hp):
        for c0 in range(0, s_len, tq):
            rows = slice(c0, c0 + tq)
            fp = f_ref[0, rows, :]
            q_ref[rows, xsl(p)] = place(qc_ref[0, rows, psl(p)], fp, pq_ref[p], cq_ref)
            k_ref[rows, xsl(p)] = place(kc_ref[0, rows, psl(p)], fp, pk_ref[p], ck_ref)
            v_ref[rows, xsl(p)] = place(vc_ref[0, rows, psl(p)], None, pv_ref[0], cv_ref)
        km_ref[:, xsl(p)] = place(kmc_ref[:, psl(p)], fpm, pk_ref[p], ck_ref)
        vm_ref[:, xsl(p)] = place(vmc_ref[:, psl(p)], None, pv_ref[0], cv_ref)

    def q_tile(i, n_real, static_tiles):
        r0 = pl.multiple_of(i * tq, tq)
        qh = [q_ref[pl.ds(r0, tq), hsl(h)] for h in range(n_h)]

        def mask_of(t):
            if t is MKEYS:
                return col_in < n_meta
            return None if t[1] is None else (col_in + t[1] * tk) <= row_in

        def load(t, ref, mref, h):
            if t is MKEYS:
                return mref[:, hsl(h)]
            return ref[pl.ds(pl.multiple_of(t[0] * tk, tk), tk), hsl(h)]

        def step(par, t1, t2, t3):
            if t3 is not None:
                for h in range(n_h):
                    acc_scr[h] = acc_scr[h] * alpha_scr[h] + _dot(p_scr[h], load(t3, v_ref, vm_ref, h))
            if t2 is not None:
                m = mask_of(t2)
                for h in range(n_h):
                    z = z_scr[1 - par, h]
                    if m is not None:
                        z = jnp.where(m, z, -jnp.inf)
                    m_old = m_scr[h]
                    m_new = jnp.maximum(m_old, jnp.max(z, axis=-1, keepdims=True))
                    p_scr[h] = jnp.exp(z - m_new).astype(BF16)
                    alpha_scr[h] = jnp.exp(m_old - m_new)[:, :LANES]
                    m_scr[h] = m_new
            if t1 is not None:
                for h in range(n_h):
                    z_scr[par, h] = _dot_nt(qh[h], load(t1, k_ref, km_ref, h))

        m_scr[...] = jnp.full(m_scr.shape, -jnp.inf, F32)
        acc_scr[...] = jnp.zeros_like(acc_scr)

        if static_tiles:
            tiles = [MKEYS] + [(n, n - (n_real - n_diag) if n >= n_real - n_diag else None)
                              for n in range(n_real)]
            for s in range(len(tiles) + 2):
                pick = lambda j: tiles[j] if 0 <= j < len(tiles) else None
                step(s % 2, pick(s), pick(s - 1), pick(s - 2))
        else:
            pos = lambda p: MKEYS if (isinstance(p, int) and p == 0) else (p - 1, None)
            n_pro = 4
            for s in range(n_pro):
                pick = lambda j: pos(j) if j >= 0 else None
                step(s % 2, pick(s), pick(s - 1), pick(s - 2))

            def body(j, carry):
                s = n_pro + 2 * j
                step(0, pos(s), pos(s - 1), pos(s - 2))
                step(1, pos(s + 1), pos(s), pos(s - 1))
                return carry

            lax.fori_loop(0, (n_real - n_pro) // 2, body, 0)
            a, b = n_real - 2, n_real - 1
            step(0, (b, None), (a, 0), (a - 1, None))
            step(1, None, (b, 1), (a, None))
            step(0, None, None, (b, None))

        for p in range(n_hp):
            res0, res1 = acc_scr[2 * p], acc_scr[2 * p + 1]
            rot0 = pltpu.roll(res0, hd, 1)
            rot1 = pltpu.roll(res1, hd, 1)
            o_ref[0, pl.ds(r0, tq), psl(p)] = jnp.where(
                head0, res0 / rot0, rot1 / res1).astype(o_ref.dtype)

    q_tile(0, n_diag, True)
    if n_q > 1:
        assert n_diag == 2, "the pipelined sweep is written for two diagonal key tiles per query tile"

        def outer(i, carry):
            q_tile(i, n_diag * (i + 1), False)
            return carry

        lax.fori_loop(1, n_q, outer, 0)


def _fox_placement(n_pair, hd):
    n_heads = 2 * n_pair
    pq = np.zeros((n_pair, 2 * LANES, 2 * LANES), np.float32)
    pk = np.zeros((n_pair, 2 * LANES, 2 * LANES), np.float32)
    pv = np.zeros((1, LANES, 2 * LANES), np.float32)
    cq = np.zeros((1, 2 * LANES), np.float32)
    ck = np.zeros((1, 2 * LANES), np.float32)
    cv = np.zeros((1, 2 * LANES), np.float32)
    for h in range(2):
        base = h * LANES
        for j in range(hd):
            pq[:, h * hd + j, base + j] = hd ** -0.5
            pk[:, h * hd + j, base + j] = 1.0
            pv[:, h * hd + j, base + j] = 1.0
        for p in range(3):
            for hp in range(n_pair):
                pq[hp, LANES + p * n_heads + 2 * hp + h, base + hd + p] = 1.0
                pk[hp, LANES + p * n_heads + 2 * hp + h, base + hd + 3 + p] = -1.0
        cq[0, base + hd + 3:base + hd + 6] = 1.0
        ck[0, base + hd:base + hd + 3] = 1.0
        cv[0, base + hd:base + LANES] = 1.0
    bf = lambda a: jnp.asarray(a, BF16)
    return bf(pq), bf(pk), bf(pv), jnp.asarray(cq), jnp.asarray(ck), jnp.asarray(cv)


def forgetting_attention(q, kv, kv_meta, fparts, fparts_meta, d_model, hd):
    b, s_len, _ = q.shape
    n_pair = d_model // LANES
    n_meta = kv_meta.shape[0]
    tq = min(ATT_TQ, s_len)
    tk = min(ATT_TK, s_len)
    assert n_meta <= tk and tk == LANES and 3 * 2 * n_pair <= LANES
    kvm = jnp.zeros((tk, kv_meta.shape[1]), kv_meta.dtype).at[:n_meta].set(kv_meta)
    fpm = jnp.zeros((tk, LANES), fparts_meta.dtype).at[:n_meta].set(fparts_meta)
    pq, pk, pv, cq, ck, cv = _fox_placement(n_pair, hd)
    n_hp = FOX_PAIRS if n_pair % FOX_PAIRS == 0 else 1
    width = n_hp * LANES
    n_grp = n_pair // n_hp
    nh = 2 * n_hp
    blk = lambda off: pl.BlockSpec((1, s_len, width), lambda bi, hp: (bi, 0, off + hp))
    mblk = lambda off: pl.BlockSpec((tk, width), lambda bi, hp: (0, off + hp))
    crow = pl.BlockSpec((1, 2 * LANES), lambda bi, hp: (0, 0))
    in_specs = [blk(0), blk(0), blk(n_grp),
                pl.BlockSpec((1, s_len, LANES), lambda bi, hp: (bi, 0, 0)),
                mblk(0), mblk(n_grp),
                pl.BlockSpec((tk, LANES), lambda bi, hp: (0, 0)),
                pl.BlockSpec((n_hp, 2 * LANES, 2 * LANES), lambda bi, hp: (hp, 0, 0)),
                pl.BlockSpec((n_hp, 2 * LANES, 2 * LANES), lambda bi, hp: (hp, 0, 0)),
                pl.BlockSpec((1, LANES, 2 * LANES), lambda bi, hp: (0, 0, 0)),
                crow, crow, crow]
    ext = lambda rows: pltpu.VMEM((rows, nh * LANES), BF16)
    return pl.pallas_call(
        functools.partial(_fox_kernel, tq=tq, tk=tk, hd=hd, n_meta=n_meta),
        grid=(b, n_grp),
        in_specs=in_specs,
        out_specs=pl.BlockSpec((1, s_len, width), lambda bi, hp: (bi, 0, hp)),
        out_shape=jax.ShapeDtypeStruct((b, s_len, d_model), BF16),
        scratch_shapes=[ext(s_len), ext(s_len), ext(s_len), ext(tk), ext(tk),
                        pltpu.VMEM((2, nh, tq, tk), F32), pltpu.VMEM((nh, tq, tk), BF16),
                        pltpu.VMEM((nh, tq, tk), F32), pltpu.VMEM((nh, tq, LANES), F32),
                        pltpu.VMEM((nh, tq, LANES), F32)],
        compiler_params=_cparams(2),
        name="forgetting_attention",
    )(q, kv, kv, fparts, kvm, kvm, fpm, pq, pk, pv, cq, ck, cv)


def _router_kernel(a_ref, wo_ref, r_ref, g_ref, wr_ref, h_ref, xn_ref, id_ref, g1_ref, g2_ref,
                   *, n_experts, n_chunk):
    a = a_ref[...]
    for c0 in range(0, wo_ref.shape[1], n_chunk):
        h_ref[:, c0:c0 + n_chunk] = r_ref[:, c0:c0 + n_chunk] + _dot(a, wo_ref[:, c0:c0 + n_chunk])
    xn = _rms(h_ref[...], g_ref[...])
    xn_ref[...] = xn.astype(BF16)
    a1, a2 = _split2(xn)
    b1, b2 = _split2(wr_ref[...])
    logits = _dot(a1, b1) + _dot(a1, b2) + _dot(a2, b1)
    lane = lax.broadcasted_iota(jnp.int32, logits.shape, 1)
    lg = jnp.where(lane < n_experts, logits, -jnp.inf)
    m1 = jnp.max(lg, axis=-1, keepdims=True)
    i1 = jnp.min(jnp.where(lg == m1, lane, LANES), axis=-1, keepdims=True)
    lg2 = jnp.where(lane == i1, -jnp.inf, lg)
    m2 = jnp.max(lg2, axis=-1, keepdims=True)
    i2 = jnp.min(jnp.where(lg2 == m2, lane, LANES), axis=-1, keepdims=True)
    e2 = jnp.exp(m2 - m1)
    den = 1.0 + e2
    id_ref[...] = jnp.where(lane == 0, i1, jnp.where(lane == 1, i2, 0))
    g1_ref[...] = jnp.broadcast_to(1.0 / den, logits.shape)
    g2_ref[...] = jnp.broadcast_to(e2 / den, logits.shape)


def moe_router(a, w_o, res, g, w_router):
    rows, d = res.shape
    k = a.shape[1]
    n_experts = w_router.shape[1]
    wr = jnp.zeros((d, LANES), F32).at[:, :n_experts].set(w_router)
    tm = _row_tile(rows)
    row_blk = lambda w: pl.BlockSpec((tm, w), lambda i: (i, 0))
    return pl.pallas_call(
        functools.partial(_router_kernel, n_experts=n_experts, n_chunk=_col_chunk(d)),
        grid=(rows // tm,),
        in_specs=[
            row_blk(k),
            pl.BlockSpec((k, d), lambda i: (0, 0)),
            row_blk(d),
            pl.BlockSpec((1, d), lambda i: (0, 0)),
            pl.BlockSpec((d, LANES), lambda i: (0, 0)),
        ],
        out_specs=[row_blk(d), row_blk(d), row_blk(LANES), row_blk(LANES), row_blk(LANES)],
        out_shape=[jax.ShapeDtypeStruct((rows, d), F32),
                   jax.ShapeDtypeStruct((rows, d), BF16),
                   jax.ShapeDtypeStruct((rows, LANES), jnp.int32),
                   jax.ShapeDtypeStruct((rows, LANES), F32),
                   jax.ShapeDtypeStruct((rows, LANES), F32)],
        compiler_params=_cparams(1),
        name="moe_router",
    )(a, w_o, res, g, wr)


def _moe_ffn_kernel(te_ref, nt_ref, x_ref, wg_ref, wu_ref, wda_ref, wdb_ref, o_ref, *, f_chunk):
    i = pl.program_id(0)

    @pl.when(i < nt_ref[0])
    def _():
        acc_a, acc_b = _swiglu_acc(x_ref[...], wg_ref, wu_ref, wda_ref, wdb_ref, f_chunk, widx=(0,))
        half = acc_a.shape[1]
        o_ref[:, :half] = acc_a.astype(o_ref.dtype)
        o_ref[:, half:] = acc_b.astype(o_ref.dtype)

    @pl.when(i >= nt_ref[0])
    def _():
        o_ref[...] = jnp.zeros_like(o_ref)


def moe_grouped_ffn(xs, tile_expert, n_tiles_used, wg, wu, wda, wdb):
    p_rows, d = xs.shape
    n_experts, _, ff = wg.shape
    tm = MOE_TM
    expert = lambda shape: pl.BlockSpec(shape, lambda i, te, nt: (te[i], 0, 0))
    grid_spec = pltpu.PrefetchScalarGridSpec(
        num_scalar_prefetch=2,
        grid=(p_rows // tm,),
        in_specs=[
            pl.BlockSpec((tm, d), lambda i, te, nt: (i, 0)),
            expert((1, d, ff)), expert((1, d, ff)), expert((1, ff, d // 2)), expert((1, ff, d // 2)),
        ],
        out_specs=pl.BlockSpec((tm, d), lambda i, te, nt: (i, 0)),
    )
    return pl.pallas_call(
        functools.partial(_moe_ffn_kernel, f_chunk=FF_CHUNK),
        grid_spec=grid_spec,
        out_shape=jax.ShapeDtypeStruct((p_rows, d), BF16),
        compiler_params=_cparams(1),
        name="moe_grouped_ffn",
    )(tile_expert, n_tiles_used, xs, wg, wu, wda, wdb)


def _combine_kernel(h_ref, y1_ref, y2_ref, g1_ref, g2_ref, o_ref):
    d = h_ref.shape[1]
    g1, g2 = g1_ref[...], g2_ref[...]
    for c0 in range(0, d, LANES):
        cs = slice(c0, c0 + LANES)
        o_ref[:, cs] = (h_ref[:, cs] + g1 * y1_ref[:, cs].astype(F32)
                        + g2 * y2_ref[:, cs].astype(F32))


def moe_block(a, w_o, res, g, w_router, ffn_weights):
    rows, d = res.shape
    n_experts = w_router.shape[1]
    tm = MOE_TM
    h, xn, ids, g1, g2 = moe_router(a, w_o, res, g, w_router)

    eid = ids[:, :TOP_K]
    sel = (eid[:, :, None] == jnp.arange(n_experts)[None, None, :]).any(axis=1)
    rank = jnp.cumsum(sel.astype(jnp.int32), axis=0) - 1
    cnt = rank[-1] + 1
    cnt_pad = ((cnt + tm - 1) // tm) * tm
    ends = jnp.cumsum(cnt_pad)
    offs = ends - cnt_pad
    pos = offs[eid] + jnp.take_along_axis(rank, eid, axis=1)
    p_rows = rows * TOP_K + n_experts * tm
    n_tiles = p_rows // tm
    tile_start = jnp.arange(n_tiles, dtype=jnp.int32) * tm
    tile_expert = jnp.minimum((tile_start[:, None] >= ends[None, :]).sum(axis=1), n_experts - 1)
    n_used = (ends[-1] // tm).astype(jnp.int32).reshape(1)

    tok = jnp.arange(rows, dtype=jnp.int32)[:, None]
    tok_c = jnp.sort((eid * rows + tok).reshape(-1)) % rows
    tok_c = jnp.concatenate([tok_c, jnp.zeros((p_rows - rows * TOP_K,), jnp.int32)])
    cstart = jnp.cumsum(cnt) - cnt
    slot = jnp.arange(p_rows, dtype=jnp.int32)
    tok_sorted = jnp.zeros((p_rows,), jnp.int32)
    for e in range(n_experts):
        inside = (slot >= offs[e]) & (slot < offs[e] + cnt[e])
        tok_sorted = jnp.where(inside, jnp.roll(tok_c, offs[e] - cstart[e]), tok_sorted)
    xs = xn.at[tok_sorted].get(mode="promise_in_bounds")
    ys = moe_grouped_ffn(xs, tile_expert.astype(jnp.int32), n_used, *ffn_weights)
    y1 = ys.at[pos[:, 0]].get(mode="promise_in_bounds", unique_indices=True)
    y2 = ys.at[pos[:, 1]].get(mode="promise_in_bounds", unique_indices=True)
    tr = _row_tile(rows)
    spec = pl.BlockSpec((tr, d), lambda i: (i, 0))
    gspec = pl.BlockSpec((tr, LANES), lambda i: (i, 0))
    return pl.pallas_call(
        _combine_kernel,
        grid=(rows // tr,),
        in_specs=[spec, spec, spec, gspec, gspec],
        out_specs=spec,
        out_shape=jax.ShapeDtypeStruct((rows, d), F32),
        compiler_params=_cparams(1),
        name="moe_combine",
    )(h, y1, y2, g1, g2)


def kernel(x, meta_tokens, norm_attn_a, w_qkv_a, w_o_a, norm_kv, w_kvf, b_f, k_norm, norm_attn_b,
           w_q_b, q_norm_b, w_o_b, norm_ffn_dense, w_gu_dense, w_down_dense, norm_ffn_moe,
           w_router, w_gu_moe, w_down_moe):
    b, s_len, d = x.shape
    n_heads = b_f.shape[0]
    hd = d // n_heads
    n_meta = meta_tokens.shape[0]
    depth = norm_attn_a.shape[0] + norm_attn_b.shape[0]
    n_a = norm_attn_a.shape[0]
    assert 2 * hd == LANES and d % LANES == 0
    row = lambda v: v.reshape(1, -1).astype(F32)

    h = x.reshape(b * s_len, d)
    hm = meta_tokens.astype(x.dtype)
    for layer in range(depth):
        last = layer == depth - 1
        if layer < n_a:
            w_qkv = to_bf16(w_qkv_a[layer])
            g = row(norm_attn_a[layer])
            qkv = norm_matmul(h, g, w_qkv, BF16).reshape(b, s_len, 3 * d)
            qkv_m = norm_matmul(hm, g, w_qkv, BF16)
            o = stickbreak_attention(qkv, qkv_m, d, hd).reshape(b * s_len, d)
            o_m = stickbreak_attention(qkv_m[None], None, d, hd)[0]
            w_o = to_bf16(w_o_a[layer])
            hm = matmul_residual(o_m, w_o, hm)
        else:
            if layer == n_a:
                g = row(norm_kv)
                w_kv = to_bf16(w_kvf[:, :2 * d])
                w_f = jnp.zeros((d, LANES), F32).at[:, :n_heads].set(w_kvf[:, 2 * d:]).astype(BF16)
                kgain = row(jnp.tile(k_norm, n_heads))
                shared = [(g, w_kv, BF16, kgain, hd), (g, w_f, F32, None, None)]
            i = layer - n_a
            q_proj = (row(norm_attn_b[i]), to_bf16(w_q_b[i]), BF16,
                      row(jnp.tile(q_norm_b[i], n_heads)), hd)
            if layer == n_a:
                kv, fl, q = norm_projections(h, shared + [q_proj])
                kv_m, fl_m = norm_projections(hm, shared)
                bias = jnp.zeros((1, LANES), F32).at[0, :n_heads].set(b_f.astype(F32))
                f_cum, f_cum_m = forget_cumsum(fl.reshape(b, s_len, LANES), fl_m, bias, n_heads)
                kv_sh = kv.reshape(b, s_len, 2 * d)
            else:
                q, = norm_projections(h, [q_proj])
            o = forgetting_attention(q.reshape(b, s_len, d), kv_sh, kv_m, f_cum, f_cum_m, d, hd)
            o = o.reshape(b * s_len, d)
            w_o = to_bf16(w_o_b[i])
            if not last:
                raise NotImplementedError("meta-row queries in forgetting layers before the last")
        j = layer // 2
        if layer % 2 == 0:
            ffn_w = to_bf16(w_gu_dense[j], splits=2) + to_bf16(w_down_dense[j], splits=2)
            g = row(norm_ffn_dense[j])
            h = dense_ffn(matmul_residual(o, w_o, h), g, *ffn_w)
            if not last:
                hm = dense_ffn(hm, g, *ffn_w)
        else:
            ffn_w = to_bf16(w_gu_moe[j], splits=2) + to_bf16(w_down_moe[j], splits=2)
            h = moe_block(o, w_o, h, row(norm_ffn_moe[j]), w_router[j], ffn_w)
            if not last:
                raise NotImplementedError("meta rows through a MoE layer before the last")
    return h.reshape(b, s_len, d)
```

```python
import functools

import jax
import jax.numpy as jnp
import numpy as np
from jax import lax
from jax.experimental import pallas as pl
from jax.experimental.pallas import tpu as pltpu

F32 = jnp.float32
BF16 = jnp.bfloat16

EPS = 1e-6
LOG2E = 1.4426950408889634
TOP_K = 2
LANES = 128
MXU_WIDTH = 256
VMEM_LIMIT = 56 * 1024 * 1024
ROW_TILE = 512
COL_CHUNK = 512
FF_CHUNK = 256
ATT_TQ = 256
ATT_TK = 128
ATT_PAIRS = 4
FOX_PAIRS = 4
MOE_TM = 512
CUM_CHUNK = 256
CAST_BLOCK_BYTES = 6 * 1024 * 1024


def _cparams(n_axes):
    return pltpu.CompilerParams(dimension_semantics=("arbitrary",) * n_axes,
                                vmem_limit_bytes=VMEM_LIMIT)


def _split3(x):
    h1 = x.astype(BF16)
    r1 = x - h1.astype(F32)
    h2 = r1.astype(BF16)
    h3 = (r1 - h2.astype(F32)).astype(BF16)
    return h1, h2, h3


def _split2(x):
    h1 = x.astype(BF16)
    h2 = (x - h1.astype(F32)).astype(BF16)
    return h1, h2


def _dot(a, b):
    return jnp.dot(a, b, preferred_element_type=F32)


def _dot_nt(a, b):
    return lax.dot_general(a, b, (((1,), (1,)), ((), ())), preferred_element_type=F32)


def _rms(xf, g):
    ms = jnp.mean(xf * xf, axis=-1, keepdims=True)
    return xf * lax.rsqrt(ms + EPS) * g


def _row_tile(rows, pref=ROW_TILE):
    return pref if rows % pref == 0 else rows


def _col_chunk(*widths):
    c = COL_CHUNK
    while any(w % c for w in widths):
        c //= 2
    assert c >= LANES
    return c


def _cast_kernel(*refs):
    n = len(refs) // 2
    for x_ref, o_ref in zip(refs[:n], refs[n:]):
        o_ref[...] = x_ref[...].astype(o_ref.dtype)


def to_bf16(w, splits=1):
    shape = w.shape
    cols = shape[-1]
    assert cols % (splits * LANES) == 0
    sc = cols // splits
    w2 = w.reshape(-1, cols)
    rows = w2.shape[0]
    rb = rows
    for cand in (2048, 1024, 512, 256, 128, 64):
        if rows % cand == 0 and cand * cols * 4 <= CAST_BLOCK_BYTES:
            rb = cand
            break
    spec = lambda s: pl.BlockSpec((rb, sc), functools.partial(lambda i, s: (i, s), s=s))
    outs = pl.pallas_call(
        _cast_kernel,
        grid=(rows // rb,),
        in_specs=[spec(s) for s in range(splits)],
        out_specs=[pl.BlockSpec((rb, sc), lambda i: (i, 0)) for _ in range(splits)],
        out_shape=[jax.ShapeDtypeStruct((rows, sc), BF16) for _ in range(splits)],
        compiler_params=_cparams(1),
        name="to_bf16",
    )(*([w2] * splits))
    outs = [o.reshape(shape[:-1] + (sc,)) for o in outs]
    return outs[0] if splits == 1 else outs


def _norm_matmul_kernel(*refs, cfg):
    n_p = len(cfg)
    x_ref, ins, outs = refs[0], refs[1:1 + 4 * n_p], refs[1 + 4 * n_p:]
    xf = x_ref[...]
    xs = xf * lax.rsqrt(jnp.mean(xf * xf, axis=-1, keepdims=True) + EPS)
    for pi, (n_chunk, head_cols) in enumerate(cfg):
        g_ref, w_ref, hg_ref, gm_ref = ins[4 * pi:4 * pi + 4]
        o_ref = outs[pi]
        xn = (xs * g_ref[...]).astype(BF16)
        n = w_ref.shape[1]
        for c0 in range(0, n, n_chunk):
            acc = _dot(xn, w_ref[:, c0:c0 + n_chunk])
            if c0 < head_cols:
                sq = (acc * acc).astype(BF16)
                gw = gm_ref.shape[0]
                ms = jnp.concatenate([_dot(sq[:, j:j + gw], gm_ref[...])
                                      for j in range(0, n_chunk, gw)], axis=1)
                acc = acc * lax.rsqrt(ms + EPS) * hg_ref[:, c0:c0 + n_chunk]
            o_ref[:, c0:c0 + n_chunk] = acc.astype(o_ref.dtype)


def norm_projections(x, projs):
    rows, d = x.shape
    tm = _row_tile(rows)
    const = lambda shape: pl.BlockSpec(shape, lambda i: (0,) * len(shape))
    cfg, args, in_specs, out_specs, out_shape = [], [x], [pl.BlockSpec((tm, d), lambda i: (i, 0))], [], []
    for g, w, out_dtype, head_gain, head_dim in projs:
        n = w.shape[1]
        if head_gain is None:
            n_chunk = _col_chunk(n)
            head_cols = 0
            head_gain = jnp.zeros((1, n_chunk), F32)
            head_dim = min(n_chunk, MXU_WIDTH)
        else:
            head_cols = head_gain.shape[1]
            n_chunk = _col_chunk(n, head_cols)
            assert n_chunk % head_dim == 0
        gw = min(n_chunk, MXU_WIDTH)
        assert gw % head_dim == 0 and n_chunk % gw == 0
        grp = jnp.arange(gw) // head_dim
        gm = ((grp[:, None] == grp[None, :]).astype(F32) / head_dim).astype(BF16)
        cfg.append((n_chunk, head_cols))
        args += [g, w, head_gain, gm]
        in_specs += [const((1, d)), const((d, n)), const(head_gain.shape), const((gw, gw))]
        out_specs.append(pl.BlockSpec((tm, n), lambda i: (i, 0)))
        out_shape.append(jax.ShapeDtypeStruct((rows, n), out_dtype))
    return pl.pallas_call(
        functools.partial(_norm_matmul_kernel, cfg=tuple(cfg)),
        grid=(rows // tm,),
        in_specs=in_specs,
        out_specs=out_specs,
        out_shape=out_shape,
        compiler_params=_cparams(1),
        name="norm_matmul",
    )(*args)


def norm_matmul(x, g, w, out_dtype, *, head_gain=None, head_dim=None):
    return norm_projections(x, [(g, w, out_dtype, head_gain, head_dim)])[0]


def _matmul_res_kernel(a_ref, w_ref, r_ref, o_ref, *, n_chunk):
    a = a_ref[...]
    n = w_ref.shape[1]
    for c0 in range(0, n, n_chunk):
        o_ref[:, c0:c0 + n_chunk] = r_ref[:, c0:c0 + n_chunk] + _dot(a, w_ref[:, c0:c0 + n_chunk])


def matmul_residual(a, w, res):
    rows, k = a.shape
    n = w.shape[1]
    tm = _row_tile(rows)
    n_chunk = _col_chunk(n)
    return pl.pallas_call(
        functools.partial(_matmul_res_kernel, n_chunk=n_chunk),
        grid=(rows // tm,),
        in_specs=[
            pl.BlockSpec((tm, k), lambda i: (i, 0)),
            pl.BlockSpec((k, n), lambda i: (0, 0)),
            pl.BlockSpec((tm, n), lambda i: (i, 0)),
        ],
        out_specs=pl.BlockSpec((tm, n), lambda i: (i, 0)),
        out_shape=jax.ShapeDtypeStruct((rows, n), F32),
        compiler_params=_cparams(1),
        name="matmul_residual",
    )(a, w, res)


def _swiglu_acc(xn, wg_ref, wu_ref, wda_ref, wdb_ref, f_chunk, widx=()):
    ff = wg_ref.shape[-1]
    acc_a = acc_b = None
    for c0 in range(0, ff, f_chunk):
        cols = widx + (slice(None), slice(c0, c0 + f_chunk))
        rows = widx + (slice(c0, c0 + f_chunk), slice(None))
        gt = _dot(xn, wg_ref[cols])
        up = _dot(xn, wu_ref[cols])
        act = (gt * (1.0 / (1.0 + jnp.exp(-gt))) * up).astype(BF16)
        pa, pb = _dot(act, wda_ref[rows]), _dot(act, wdb_ref[rows])
        acc_a = pa if acc_a is None else acc_a + pa
        acc_b = pb if acc_b is None else acc_b + pb
    return acc_a, acc_b


def _dense_ffn_kernel(x_ref, g_ref, wg_ref, wu_ref, wda_ref, wdb_ref, o_ref, *, f_chunk):
    xf = x_ref[...]
    xn = _rms(xf, g_ref[...]).astype(BF16)
    acc_a, acc_b = _swiglu_acc(xn, wg_ref, wu_ref, wda_ref, wdb_ref, f_chunk)
    half = acc_a.shape[1]
    o_ref[:, :half] = xf[:, :half] + acc_a
    o_ref[:, half:] = xf[:, half:] + acc_b


def dense_ffn(x, g, wg, wu, wda, wdb):
    rows, d = x.shape
    ff = wg.shape[1]
    tm = _row_tile(rows)
    const = lambda shape: pl.BlockSpec(shape, lambda i: (0, 0))
    return pl.pallas_call(
        functools.partial(_dense_ffn_kernel, f_chunk=FF_CHUNK),
        grid=(rows // tm,),
        in_specs=[
            pl.BlockSpec((tm, d), lambda i: (i, 0)),
            const((1, d)), const((d, ff)), const((d, ff)), const((ff, d // 2)), const((ff, d // 2)),
        ],
        out_specs=pl.BlockSpec((tm, d), lambda i: (i, 0)),
        out_shape=jax.ShapeDtypeStruct((rows, d), F32),
        compiler_params=_cparams(1),
        name="dense_ffn",
    )(x, g, wg, wu, wda, wdb)


def _sb_suffix(z, mask, u_ref):
    zb = z.astype(BF16)
    one, zero = jnp.asarray(1.0, BF16), jnp.asarray(0.0, BF16)
    sp = jnp.maximum(zb, zero) + jnp.log(one + jnp.exp2(jnp.abs(zb) * jnp.asarray(-LOG2E, BF16)))
    if mask is not None:
        sp = jnp.where(mask, sp, zero)
    return _dot(sp, u_ref[...])


def _sb_kernel(*refs, tq, tk, hd, n_meta, scale):
    z_scr, cs_scr, run_scr, acc_scr = refs[-4:]
    if n_meta:
        q_ref, k_ref, v_ref, km_ref, vm_ref, u_ref, o_ref = refs[:-4]
    else:
        q_ref, k_ref, v_ref, u_ref, o_ref = refs[:-4]
    s_len = q_ref.shape[1]
    n_hp = q_ref.shape[2] // LANES
    psl = lambda p: slice(p * LANES, (p + 1) * LANES)
    n_q = s_len // tq
    n_diag = tq // tk
    has_meta = 1 if n_meta else 0
    lane = lax.broadcasted_iota(jnp.int32, (tq, LANES), 1)
    head0 = lane < hd
    row_in = lax.broadcasted_iota(jnp.int32, (tq, tk), 0)
    col_in = lax.broadcasted_iota(jnp.int32, (tq, tk), 1)
    MKEYS = "meta"

    def q_tile(i, n_real, static_tiles):
        r0 = pl.multiple_of(i * tq, tq)
        qh = []
        for p in range(n_hp):
            q = q_ref[0, pl.ds(r0, tq), psl(p)] * scale
            qh += [jnp.where(head0, q, jnp.zeros_like(q)), jnp.where(head0, jnp.zeros_like(q), q)]

        def offset(n):
            return pl.multiple_of(r0 + (n_diag - 1 - n) * tk, tk)

        def mask_of(n):
            if n is MKEYS:
                return col_in < n_meta
            if isinstance(n, int) and n < n_diag:
                return (col_in + (n_diag - 1 - n) * tk) < row_in
            return None

        def load(n, ref, mref):
            return mref[...] if n is MKEYS else ref[0, pl.ds(offset(n), tk), :]

        def step(par, t1, t2, t3):
            if t3 is not None:
                v, m = load(t3, v_ref, vm_ref if n_meta else None), mask_of(t3)
                for h in range(2 * n_hp):
                    g = run_scr[h] + cs_scr[h]
                    a = jnp.exp(z_scr[par, h] + g)
                    if m is not None:
                        a = jnp.where(m, a, 0.0)
                    run_scr[h] = jnp.broadcast_to(g[:, 0:1], g.shape)
                    acc_scr[h] += _dot(a.astype(BF16), v[:, psl(h // 2)])
            if t2 is not None:
                m = mask_of(t2)
                for h in range(2 * n_hp):
                    cs_scr[h] = _sb_suffix(z_scr[1 - par, h], m, u_ref)
            if t1 is not None:
                k = load(t1, k_ref, km_ref if n_meta else None)
                for h in range(2 * n_hp):
                    z_scr[par, h] = _dot_nt(qh[h], k[:, psl(h // 2)])

        run_scr[...] = jnp.zeros_like(run_scr)
        acc_scr[...] = jnp.zeros_like(acc_scr)

        if static_tiles:
            tiles = list(range(n_real)) + ([MKEYS] if has_meta else [])
            for s in range(len(tiles) + 2):
                pick = lambda j: tiles[j] if 0 <= j < len(tiles) else None
                step(s % 2, pick(s), pick(s - 1), pick(s - 2))
        else:
            n_pro = n_diag + 2
            assert n_pro % 2 == 0
            for s in range(n_pro):
                pick = lambda j: j if j >= 0 else None
                step(s % 2, pick(s), pick(s - 1), pick(s - 2))

            def body(j, carry):
                s = n_pro + 2 * j
                step(0, s, s - 1, s - 2)
                step(1, s + 1, s, s - 1)
                return carry

            lax.fori_loop(0, (n_real - n_pro) // 2, body, 0)
            last = n_real - 1
            step(0, MKEYS if has_meta else None, last, last - 1)
            step(1, None, MKEYS if has_meta else None, last)
            if has_meta:
                step(0, None, None, MKEYS)

        for p in range(n_hp):
            o_ref[0, pl.ds(r0, tq), psl(p)] = jnp.where(
                head0, acc_scr[2 * p], acc_scr[2 * p + 1]).astype(o_ref.dtype)

    q_tile(0, n_diag, True)
    if n_q > 1:
        assert n_diag >= 2, "the pipelined prologue needs two diagonal key tiles per query tile"

        def outer(i, carry):
            q_tile(i, n_diag * (i + 1), False)
            return carry

        lax.fori_loop(1, n_q, outer, 0)


def _suffix_matrix(t):
    r = jnp.arange(t)
    return -(r[:, None] >= r[None, :]).astype(BF16)


def stickbreak_attention(qkv, qkv_meta, d_model, hd):
    b, s_len, _ = qkv.shape
    n_pair = d_model // LANES
    tq = min(ATT_TQ, s_len)
    tk = min(ATT_TK, s_len)
    n_meta = 0 if qkv_meta is None else qkv_meta.shape[0]
    n_hp = ATT_PAIRS if n_pair % ATT_PAIRS == 0 else 1
    width = n_hp * LANES
    n_grp = n_pair // n_hp
    col = lambda which: (lambda bi, hp: (bi, 0, which * n_grp + hp))
    in_specs = [pl.BlockSpec((1, s_len, width), col(0)),
                pl.BlockSpec((1, s_len, width), col(1)),
                pl.BlockSpec((1, s_len, width), col(2))]
    args = [qkv, qkv, qkv]
    if n_meta:
        assert n_meta <= tk
        meta_pad = jnp.zeros((tk, qkv_meta.shape[1]), qkv_meta.dtype).at[:n_meta].set(qkv_meta)
        in_specs += [pl.BlockSpec((tk, width), lambda bi, hp: (0, n_grp + hp)),
                     pl.BlockSpec((tk, width), lambda bi, hp: (0, 2 * n_grp + hp))]
        args += [meta_pad, meta_pad]
    in_specs.append(pl.BlockSpec((tk, tk), lambda bi, hp: (0, 0)))
    args.append(_suffix_matrix(tk))
    kern = functools.partial(_sb_kernel, tq=tq, tk=tk, hd=hd, n_meta=n_meta, scale=hd ** -0.5)
    nh = 2 * n_hp
    return pl.pallas_call(
        kern,
        grid=(b, n_grp),
        in_specs=in_specs,
        out_specs=pl.BlockSpec((1, s_len, width), lambda bi, hp: (bi, 0, hp)),
        out_shape=jax.ShapeDtypeStruct((b, s_len, d_model), BF16),
        scratch_shapes=[pltpu.VMEM((2, nh, tq, tk), F32), pltpu.VMEM((nh, tq, tk), F32),
                        pltpu.VMEM((nh, tq, tk), F32), pltpu.VMEM((nh, tq, LANES), F32)],
        compiler_params=_cparams(2),
        name="stickbreak_attention",
    )(*args)


def _log_sigmoid(y):
    return jnp.minimum(y, 0.0) - jnp.log(1.0 + jnp.exp(-jnp.abs(y)))


def _tri_cumsum(tri, x):
    h1, h2, h3 = _split3(x)
    return _dot(tri, h1) + _dot(tri, h2) + _dot(tri, h3)


def _forget_cumsum_kernel(f_ref, fm_ref, b_ref, tri_ref, trim_ref, o_ref, om_ref, *, chunk, n_heads):
    bias = b_ref[...]

    def packed(f):
        lane = lax.broadcasted_iota(jnp.int32, f.shape, 1)
        out = None
        for a, part in enumerate(_split3(f)):
            pa = jnp.where(lane < n_heads, part.astype(F32), 0.0)
            pa = pa if a == 0 else pltpu.roll(pa, a * n_heads, 1)
            out = pa if out is None else out + pa
        return out.astype(BF16)

    fm = _tri_cumsum(trim_ref[...], _log_sigmoid(fm_ref[...] + bias))
    om_ref[...] = packed(fm)
    n_meta = fm.shape[0]
    carry = fm[n_meta - 1:n_meta, :]
    s_len = f_ref.shape[1]
    for c0 in range(0, s_len, chunk):
        fc = _tri_cumsum(tri_ref[...], _log_sigmoid(f_ref[0, c0:c0 + chunk, :] + bias)) + carry
        o_ref[0, c0:c0 + chunk, :] = packed(fc)
        carry = fc[chunk - 1:chunk, :]


def forget_cumsum(f_logit, f_logit_meta, bias, n_heads):
    b, s_len, w = f_logit.shape
    n_meta = f_logit_meta.shape[0]
    assert 3 * n_heads <= w
    chunk = min(CUM_CHUNK, s_len)
    tri = lambda t: (jnp.arange(t)[:, None] >= jnp.arange(t)[None, :]).astype(BF16)
    return pl.pallas_call(
        functools.partial(_forget_cumsum_kernel, chunk=chunk, n_heads=n_heads),
        grid=(b,),
        in_specs=[
            pl.BlockSpec((1, s_len, w), lambda bi: (bi, 0, 0)),
            pl.BlockSpec((n_meta, w), lambda bi: (0, 0)),
            pl.BlockSpec((1, w), lambda bi: (0, 0)),
            pl.BlockSpec((chunk, chunk), lambda bi: (0, 0)),
            pl.BlockSpec((n_meta, n_meta), lambda bi: (0, 0)),
        ],
        out_specs=[pl.BlockSpec((1, s_len, w), lambda bi: (bi, 0, 0)),
                   pl.BlockSpec((n_meta, w), lambda bi: (0, 0))],
        out_shape=[jax.ShapeDtypeStruct((b, s_len, w), BF16),
                   jax.ShapeDtypeStruct((n_meta, w), BF16)],
        compiler_params=_cparams(1),
        name="forget_cumsum",
    )(f_logit, f_logit_meta, bias, tri(chunk), tri(n_meta))


def _fox_kernel(qc_ref, kc_ref, vc_ref, f_ref, kmc_ref, vmc_ref, fm_ref, pq_ref, pk_ref, pv_ref,
                cq_ref, ck_ref, cv_ref, o_ref,
                q_ref, k_ref, v_ref, km_ref, vm_ref, z_scr, p_scr, m_scr, alpha_scr, acc_scr,
                *, tq, tk, hd, n_meta):
    s_len = qc_ref.shape[1]
    n_q = s_len // tq
    n_diag = tq // tk
    lane = lax.broadcasted_iota(jnp.int32, (tq, LANES), 1)
    head0 = lane < hd
    row_in = lax.broadcasted_iota(jnp.int32, (tq, tk), 0)
    col_in = lax.broadcasted_iota(jnp.int32, (tq, tk), 1)
    MKEYS = "meta"
    n_hp = qc_ref.shape[2] // LANES
    n_h = 2 * n_hp
    hsl = lambda h: slice(h * LANES, (h + 1) * LANES)
    psl = lambda p: slice(p * LANES, (p + 1) * LANES)
    xsl = lambda p: slice(2 * p * LANES, 2 * (p + 1) * LANES)

    def place(main, fpacked, p_mat, c_ref):
        lhs = main if fpacked is None else jnp.concatenate([main, fpacked], axis=1)
        return (_dot(lhs, p_mat) + c_ref[...]).astype(BF16)

    fpm = fm_ref[...]
    for p in range(n_hp):
        for c0 in range(0, s_len, tq):
            rows = slice(c0, c0 + tq)
            fp = f_ref[0, rows, :]
            q_ref[rows, xsl(p)] = place(qc_ref[0, rows, psl(p)], fp, pq_ref[p], cq_ref)
            k_ref[rows, xsl(p)] = place(kc_ref[0, rows, psl(p)], fp, pk_ref[p], ck_ref)
            v_ref[rows, xsl(p)] = place(vc_ref[0, rows, psl(p)], None, pv_ref[0], cv_ref)
        km_ref[:, xsl(p)] = place(kmc_ref[:, psl(p)], fpm, pk_ref[p], ck_ref)
        vm_ref[:, xsl(p)] = place(vmc_ref[:, psl(p)], None, pv_ref[0], cv_ref)

    def q_tile(i, n_real, static_tiles):
        r0 = pl.multiple_of(i * tq, tq)
        qh = [q_ref[pl.ds(r0, tq), hsl(h)] for h in range(n_h)]

        def mask_of(t):
            if t is MKEYS:
                return col_in < n_meta
            return None if t[1] is None else (col_in + t[1] * tk) <= row_in

        def load(t, ref, mref, h):
            if t is MKEYS:
                return mref[:, hsl(h)]
            return ref[pl.ds(pl.multiple_of(t[0] * tk, tk), tk), hsl(h)]

        def step(par, t1, t2, t3):
            if t3 is not None:
                for h in range(n_h):
                    acc_scr[h] = acc_scr[h] * alpha_scr[h] + _dot(p_scr[h], load(t3, v_ref, vm_ref, h))
            if t2 is not None:
                m = mask_of(t2)
                for h in range(n_h):
                    z = z_scr[1 - par, h]
                    if m is not None:
                        z = jnp.where(m, z, -jnp.inf)
                    m_old = m_scr[h]
                    m_new = jnp.maximum(m_old, jnp.max(z, axis=-1, keepdims=True))
                    p_scr[h] = jnp.exp(z - m_new).astype(BF16)
                    alpha_scr[h] = jnp.exp(m_old - m_new)[:, :LANES]
                    m_scr[h] = m_new
            if t1 is not None:
                for h in range(n_h):
                    z_scr[par, h] = _dot_nt(qh[h], load(t1, k_ref, km_ref, h))

        m_scr[...] = jnp.full(m_scr.shape, -jnp.inf, F32)
        acc_scr[...] = jnp.zeros_like(acc_scr)

        if static_tiles:
            tiles = [MKEYS] + [(n, n - (n_real - n_diag) if n >= n_real - n_diag else None)
                              for n in range(n_real)]
            for s in range(len(tiles) + 2):
                pick = lambda j: tiles[j] if 0 <= j < len(tiles) else None
                step(s % 2, pick(s), pick(s - 1), pick(s - 2))
        else:
            pos = lambda p: MKEYS if (isinstance(p, int) and p == 0) else (p - 1, None)
            n_pro = 4
            for s in range(n_pro):
                pick = lambda j: pos(j) if j >= 0 else None
                step(s % 2, pick(s), pick(s - 1), pick(s - 2))

            def body(j, carry):
                s = n_pro + 2 * j
                step(0, pos(s), pos(s - 1), pos(s - 2))
                step(1, pos(s + 1), pos(s), pos(s - 1))
                return carry

            lax.fori_loop(0, (n_real - n_pro) // 2, body, 0)
            a, b = n_real - 2, n_real - 1
            step(0, (b, None), (a, 0), (a - 1, None))
            step(1, None, (b, 1), (a, None))
            step(0, None, None, (b, None))

        for p in range(n_hp):
            res0, res1 = acc_scr[2 * p], acc_scr[2 * p + 1]
            rot0 = pltpu.roll(res0, hd, 1)
            rot1 = pltpu.roll(res1, hd, 1)
            o_ref[0, pl.ds(r0, tq), psl(p)] = jnp.where(
                head0, res0 / rot0, rot1 / res1).astype(o_ref.dtype)

    q_tile(0, n_diag, True)
    if n_q > 1:
        assert n_diag == 2, "the pipelined sweep is written for two diagonal key tiles per query tile"

        def outer(i, carry):
            q_tile(i, n_diag * (i + 1), False)
            return carry

        lax.fori_loop(1, n_q, outer, 0)


def _fox_placement(n_pair, hd):
    n_heads = 2 * n_pair
    pq = np.zeros((n_pair, 2 * LANES, 2 * LANES), np.float32)
    pk = np.zeros((n_pair, 2 * LANES, 2 * LANES), np.float32)
    pv = np.zeros((1, LANES, 2 * LANES), np.float32)
    cq = np.zeros((1, 2 * LANES), np.float32)
    ck = np.zeros((1, 2 * LANES), np.float32)
    cv = np.zeros((1, 2 * LANES), np.float32)
    for h in range(2):
        base = h * LANES
        for j in range(hd):
            pq[:, h * hd + j, base + j] = hd ** -0.5
            pk[:, h * hd + j, base + j] = 1.0
            pv[:, h * hd + j, base + j] = 1.0
        for p in range(3):
            for hp in range(n_pair):
                pq[hp, LANES + p * n_heads + 2 * hp + h, base + hd + p] = 1.0
                pk[hp, LANES + p * n_heads + 2 * hp + h, base + hd + 3 + p] = -1.0
        cq[0, base + hd + 3:base + hd + 6] = 1.0
        ck[0, base + hd:base + hd + 3] = 1.0
        cv[0, base + hd:base + LANES] = 1.0
    bf = lambda a: jnp.asarray(a, BF16)
    return bf(pq), bf(pk), bf(pv), jnp.asarray(cq), jnp.asarray(ck), jnp.asarray(cv)


def forgetting_attention(q, kv, kv_meta, fparts, fparts_meta, d_model, hd):
    b, s_len, _ = q.shape
    n_pair = d_model // LANES
    n_meta = kv_meta.shape[0]
    tq = min(ATT_TQ, s_len)
    tk = min(ATT_TK, s_len)
    assert n_meta <= tk and tk == LANES and 3 * 2 * n_pair <= LANES
    kvm = jnp.zeros((tk, kv_meta.shape[1]), kv_meta.dtype).at[:n_meta].set(kv_meta)
    fpm = jnp.zeros((tk, LANES), fparts_meta.dtype).at[:n_meta].set(fparts_meta)
    pq, pk, pv, cq, ck, cv = _fox_placement(n_pair, hd)
    n_hp = FOX_PAIRS if n_pair % FOX_PAIRS == 0 else 1
    width = n_hp * LANES
    n_grp = n_pair // n_hp
    nh = 2 * n_hp
    blk = lambda off: pl.BlockSpec((1, s_len, width), lambda bi, hp: (bi, 0, off + hp))
    mblk = lambda off: pl.BlockSpec((tk, width), lambda bi, hp: (0, off + hp))
    crow = pl.BlockSpec((1, 2 * LANES), lambda bi, hp: (0, 0))
    in_specs = [blk(0), blk(0), blk(n_grp),
                pl.BlockSpec((1, s_len, LANES), lambda bi, hp: (bi, 0, 0)),
                mblk(0), mblk(n_grp),
                pl.BlockSpec((tk, LANES), lambda bi, hp: (0, 0)),
                pl.BlockSpec((n_hp, 2 * LANES, 2 * LANES), lambda bi, hp: (hp, 0, 0)),
                pl.BlockSpec((n_hp, 2 * LANES, 2 * LANES), lambda bi, hp: (hp, 0, 0)),
                pl.BlockSpec((1, LANES, 2 * LANES), lambda bi, hp: (0, 0, 0)),
                crow, crow, crow]
    ext = lambda rows: pltpu.VMEM((rows, nh * LANES), BF16)
    return pl.pallas_call(
        functools.partial(_fox_kernel, tq=tq, tk=tk, hd=hd, n_meta=n_meta),
        grid=(b, n_grp),
        in_specs=in_specs,
        out_specs=pl.BlockSpec((1, s_len, width), lambda bi, hp: (bi, 0, hp)),
        out_shape=jax.ShapeDtypeStruct((b, s_len, d_model), BF16),
        scratch_shapes=[ext(s_len), ext(s_len), ext(s_len), ext(tk), ext(tk),
                        pltpu.VMEM((2, nh, tq, tk), F32), pltpu.VMEM((nh, tq, tk), BF16),
                        pltpu.VMEM((nh, tq, tk), F32), pltpu.VMEM((nh, tq, LANES), F32),
                        pltpu.VMEM((nh, tq, LANES), F32)],
        compiler_params=_cparams(2),
        name="forgetting_attention",
    )(q, kv, kv, fparts, kvm, kvm, fpm, pq, pk, pv, cq, ck, cv)


def _router_kernel(a_ref, wo_ref, r_ref, g_ref, wr_ref, h_ref, xn_ref, id_ref, g1_ref, g2_ref,
                   *, n_experts, n_chunk):
    a = a_ref[...]
    for c0 in range(0, wo_ref.shape[1], n_chunk):
        h_ref[:, c0:c0 + n_chunk] = r_ref[:, c0:c0 + n_chunk] + _dot(a, wo_ref[:, c0:c0 + n_chunk])
    xn = _rms(h_ref[...], g_ref[...])
    xn_ref[...] = xn.astype(BF16)
    a1, a2 = _split2(xn)
    b1, b2 = _split2(wr_ref[...])
    logits = _dot(a1, b1) + _dot(a1, b2) + _dot(a2, b1)
    lane = lax.broadcasted_iota(jnp.int32, logits.shape, 1)
    lg = jnp.where(lane < n_experts, logits, -jnp.inf)
    m1 = jnp.max(lg, axis=-1, keepdims=True)
    i1 = jnp.min(jnp.where(lg == m1, lane, LANES), axis=-1, keepdims=True)
    lg2 = jnp.where(lane == i1, -jnp.inf, lg)
    m2 = jnp.max(lg2, axis=-1, keepdims=True)
    i2 = jnp.min(jnp.where(lg2 == m2, lane, LANES), axis=-1, keepdims=True)
    e2 = jnp.exp(m2 - m1)
    den = 1.0 + e2
    ids = jnp.where(lane == 0, i1, jnp.where(lane == 1, i2, 0)).astype(F32)
    id_ref[...] = jnp.transpose(ids)[:id_ref.shape[0]].astype(jnp.int32)
    g1_ref[...] = jnp.broadcast_to(1.0 / den, logits.shape)
    g2_ref[...] = jnp.broadcast_to(e2 / den, logits.shape)


def moe_router(a, w_o, res, g, w_router):
    rows, d = res.shape
    k = a.shape[1]
    n_experts = w_router.shape[1]
    wr = jnp.zeros((d, LANES), F32).at[:, :n_experts].set(w_router)
    tm = _row_tile(rows)
    row_blk = lambda w: pl.BlockSpec((tm, w), lambda i: (i, 0))
    return pl.pallas_call(
        functools.partial(_router_kernel, n_experts=n_experts, n_chunk=_col_chunk(d)),
        grid=(rows // tm,),
        in_specs=[
            row_blk(k),
            pl.BlockSpec((k, d), lambda i: (0, 0)),
            row_blk(d),
            pl.BlockSpec((1, d), lambda i: (0, 0)),
            pl.BlockSpec((d, LANES), lambda i: (0, 0)),
        ],
        out_specs=[row_blk(d), row_blk(d), pl.BlockSpec((8, tm), lambda i: (0, i)),
                   row_blk(LANES), row_blk(LANES)],
        out_shape=[jax.ShapeDtypeStruct((rows, d), F32),
                   jax.ShapeDtypeStruct((rows, d), BF16),
                   jax.ShapeDtypeStruct((8, rows), jnp.int32),
                   jax.ShapeDtypeStruct((rows, LANES), F32),
                   jax.ShapeDtypeStruct((rows, LANES), F32)],
        compiler_params=_cparams(1),
        name="moe_router",
    )(a, w_o, res, g, wr)


def _moe_ffn_kernel(te_ref, nt_ref, x_ref, wg_ref, wu_ref, wda_ref, wdb_ref, o_ref, *, f_chunk):
    i = pl.program_id(0)

    @pl.when(i < nt_ref[0])
    def _():
        acc_a, acc_b = _swiglu_acc(x_ref[...], wg_ref, wu_ref, wda_ref, wdb_ref, f_chunk, widx=(0,))
        half = acc_a.shape[1]
        o_ref[:, :half] = acc_a.astype(o_ref.dtype)
        o_ref[:, half:] = acc_b.astype(o_ref.dtype)

    @pl.when(i >= nt_ref[0])
    def _():
        o_ref[...] = jnp.zeros_like(o_ref)


def moe_grouped_ffn(xs, tile_expert, n_tiles_used, wg, wu, wda, wdb):
    p_rows, d = xs.shape
    n_experts, _, ff = wg.shape
    tm = MOE_TM
    expert = lambda shape: pl.BlockSpec(shape, lambda i, te, nt: (te[i], 0, 0))
    grid_spec = pltpu.PrefetchScalarGridSpec(
        num_scalar_prefetch=2,
        grid=(p_rows // tm,),
        in_specs=[
            pl.BlockSpec((tm, d), lambda i, te, nt: (i, 0)),
            expert((1, d, ff)), expert((1, d, ff)), expert((1, ff, d // 2)), expert((1, ff, d // 2)),
        ],
        out_specs=pl.BlockSpec((tm, d), lambda i, te, nt: (i, 0)),
    )
    return pl.pallas_call(
        functools.partial(_moe_ffn_kernel, f_chunk=FF_CHUNK),
        grid_spec=grid_spec,
        out_shape=jax.ShapeDtypeStruct((p_rows, d), BF16),
        compiler_params=_cparams(1),
        name="moe_grouped_ffn",
    )(tile_expert, n_tiles_used, xs, wg, wu, wda, wdb)


def _combine_kernel(h_ref, y1_ref, y2_ref, g1_ref, g2_ref, o_ref):
    d = h_ref.shape[1]
    g1, g2 = g1_ref[...], g2_ref[...]
    for c0 in range(0, d, LANES):
        cs = slice(c0, c0 + LANES)
        o_ref[:, cs] = (h_ref[:, cs] + g1 * y1_ref[:, cs].astype(F32)
                        + g2 * y2_ref[:, cs].astype(F32))


def moe_block(a, w_o, res, g, w_router, ffn_weights):
    rows, d = res.shape
    n_experts = w_router.shape[1]
    tm = MOE_TM
    h, xn, ids, g1, g2 = moe_router(a, w_o, res, g, w_router)

    eid = ids[:TOP_K]
    ex = jnp.arange(n_experts, dtype=jnp.int32)[:, None]
    sel = (eid[0][None, :] == ex) | (eid[1][None, :] == ex)
    rank = jnp.cumsum(sel.astype(jnp.int32), axis=1) - 1
    cnt = rank[:, -1] + 1
    cnt_pad = ((cnt + tm - 1) // tm) * tm
    ends = jnp.cumsum(cnt_pad)
    offs = ends - cnt_pad
    slot_of = offs[:, None] + rank
    pos = [jnp.sum(jnp.where(eid[k][None, :] == ex, slot_of, 0), axis=0) for k in range(TOP_K)]
    p_rows = rows * TOP_K + n_experts * tm
    n_tiles = p_rows // tm
    tile_start = jnp.arange(n_tiles, dtype=jnp.int32) * tm
    tile_expert = jnp.minimum((tile_start[:, None] >= ends[None, :]).sum(axis=1), n_experts - 1)
    n_used = (ends[-1] // tm).astype(jnp.int32).reshape(1)

    tok = jnp.arange(rows, dtype=jnp.int32)[None, :]
    tok_c = jnp.sort((eid * rows + tok).reshape(-1)) % rows
    tok_c = jnp.concatenate([tok_c, jnp.zeros((p_rows - rows * TOP_K,), jnp.int32)])
    cstart = jnp.cumsum(cnt) - cnt
    slot = jnp.arange(p_rows, dtype=jnp.int32)
    tok_sorted = jnp.zeros((p_rows,), jnp.int32)
    for e in range(n_experts):
        inside = (slot >= offs[e]) & (slot < offs[e] + cnt[e])
        tok_sorted = jnp.where(inside, jnp.roll(tok_c, offs[e] - cstart[e]), tok_sorted)
    xs = xn.at[tok_sorted].get(mode="promise_in_bounds")
    ys = moe_grouped_ffn(xs, tile_expert.astype(jnp.int32), n_used, *ffn_weights)
    y1 = ys.at[pos[0]].get(mode="promise_in_bounds", unique_indices=True)
    y2 = ys.at[pos[1]].get(mode="promise_in_bounds", unique_indices=True)
    tr = _row_tile(rows)
    spec = pl.BlockSpec((tr, d), lambda i: (i, 0))
    gspec = pl.BlockSpec((tr, LANES), lambda i: (i, 0))
    return pl.pallas_call(
        _combine_kernel,
        grid=(rows // tr,),
        in_specs=[spec, spec, spec, gspec, gspec],
        out_specs=spec,
        out_shape=jax.ShapeDtypeStruct((rows, d), F32),
        compiler_params=_cparams(1),
        name="moe_combine",
    )(h, y1, y2, g1, g2)


def kernel(x, meta_tokens, norm_attn_a, w_qkv_a, w_o_a, norm_kv, w_kvf, b_f, k_norm, norm_attn_b,
           w_q_b, q_norm_b, w_o_b, norm_ffn_dense, w_gu_dense, w_down_dense, norm_ffn_moe,
           w_router, w_gu_moe, w_down_moe):
    b, s_len, d = x.shape
    n_heads = b_f.shape[0]
    hd = d // n_heads
    n_meta = meta_tokens.shape[0]
    depth = norm_attn_a.shape[0] + norm_attn_b.shape[0]
    n_a = norm_attn_a.shape[0]
    assert 2 * hd == LANES and d % LANES == 0
    row = lambda v: v.reshape(1, -1).astype(F32)

    h = x.reshape(b * s_len, d)
    hm = meta_tokens.astype(x.dtype)
    for layer in range(depth):
        last = layer == depth - 1
        if layer < n_a:
            w_qkv = to_bf16(w_qkv_a[layer])
            g = row(norm_attn_a[layer])
            qkv = norm_matmul(h, g, w_qkv, BF16).reshape(b, s_len, 3 * d)
            qkv_m = norm_matmul(hm, g, w_qkv, BF16)
            o = stickbreak_attention(qkv, qkv_m, d, hd).reshape(b * s_len, d)
            o_m = stickbreak_attention(qkv_m[None], None, d, hd)[0]
            w_o = to_bf16(w_o_a[layer])
            hm = matmul_residual(o_m, w_o, hm)
        else:
            if layer == n_a:
                g = row(norm_kv)
                w_kv = to_bf16(w_kvf[:, :2 * d])
                w_f = jnp.zeros((d, LANES), F32).at[:, :n_heads].set(w_kvf[:, 2 * d:]).astype(BF16)
                kgain = row(jnp.tile(k_norm, n_heads))
                shared = [(g, w_kv, BF16, kgain, hd), (g, w_f, F32, None, None)]
            i = layer - n_a
            q_proj = (row(norm_attn_b[i]), to_bf16(w_q_b[i]), BF16,
                      row(jnp.tile(q_norm_b[i], n_heads)), hd)
            if layer == n_a:
                kv, fl, q = norm_projections(h, shared + [q_proj])
                kv_m, fl_m = norm_projections(hm, shared)
                bias = jnp.zeros((1, LANES), F32).at[0, :n_heads].set(b_f.astype(F32))
                f_cum, f_cum_m = forget_cumsum(fl.reshape(b, s_len, LANES), fl_m, bias, n_heads)
                kv_sh = kv.reshape(b, s_len, 2 * d)
            else:
                q, = norm_projections(h, [q_proj])
            o = forgetting_attention(q.reshape(b, s_len, d), kv_sh, kv_m, f_cum, f_cum_m, d, hd)
            o = o.reshape(b * s_len, d)
            w_o = to_bf16(w_o_b[i])
            if not last:
                raise NotImplementedError("meta-row queries in forgetting layers before the last")
        j = layer // 2
        if layer % 2 == 0:
            ffn_w = to_bf16(w_gu_dense[j], splits=2) + to_bf16(w_down_dense[j], splits=2)
            g = row(norm_ffn_dense[j])
            h = dense_ffn(matmul_residual(o, w_o, h), g, *ffn_w)
            if not last:
                hm = dense_ffn(hm, g, *ffn_w)
        else:
            ffn_w = to_bf16(w_gu_moe[j], splits=2) + to_bf16(w_down_moe[j], splits=2)
            h = moe_block(o, w_o, h, row(norm_ffn_moe[j]), w_router[j], ffn_w)
            if not last:
                raise NotImplementedError("meta rows through a MoE layer before the last")
    return h.reshape(b, s_len, d)
```

```python
import functools

import jax
import jax.numpy as jnp
import numpy as np
from jax import lax
from jax.experimental import pallas as pl
from jax.experimental.pallas import tpu as pltpu

F32 = jnp.float32
BF16 = jnp.bfloat16

EPS = 1e-6
LOG2E = 1.4426950408889634
TOP_K = 2
LANES = 128
MXU_WIDTH = 256
VMEM_LIMIT = 56 * 1024 * 1024
ROW_TILE = 512
COL_CHUNK = 512
FF_CHUNK = 256
ATT_TQ = 256
ATT_TK = 128
ATT_PAIRS = 4
FOX_PAIRS = 4
MOE_TM = 512
CUM_CHUNK = 256
CAST_BLOCK_BYTES = 6 * 1024 * 1024
CAST_RING = 3


def _cparams(n_axes):
    return pltpu.CompilerParams(dimension_semantics=("arbitrary",) * n_axes,
                                vmem_limit_bytes=VMEM_LIMIT)


def _split3(x):
    h1 = x.astype(BF16)
    r1 = x - h1.astype(F32)
    h2 = r1.astype(BF16)
    h3 = (r1 - h2.astype(F32)).astype(BF16)
    return h1, h2, h3


def _split2(x):
    h1 = x.astype(BF16)
    h2 = (x - h1.astype(F32)).astype(BF16)
    return h1, h2


def _dot(a, b):
    return jnp.dot(a, b, preferred_element_type=F32)


def _dot_nt(a, b):
    return lax.dot_general(a, b, (((1,), (1,)), ((), ())), preferred_element_type=F32)


def _rms(xf, g):
    ms = jnp.mean(xf * xf, axis=-1, keepdims=True)
    return xf * lax.rsqrt(ms + EPS) * g


def _row_tile(rows, pref=ROW_TILE):
    return pref if rows % pref == 0 else rows


def _col_chunk(*widths):
    c = COL_CHUNK
    while any(w % c for w in widths):
        c //= 2
    assert c >= LANES
    return c


def _cast_kernel(w_hbm, *refs, n_steps, rb, sc):
    o_refs, buf, sem = refs[:-2], refs[-2], refs[-1]
    i = pl.program_id(0)
    ahead = CAST_RING - 1

    def copy(step, s):
        slot = step % CAST_RING
        return pltpu.make_async_copy(
            w_hbm.at[pl.ds(step * rb, rb), pl.ds(s * sc, sc)], buf.at[slot, s], sem.at[slot, s])

    @pl.when(i == 0)
    def _():
        for j in range(min(ahead, n_steps)):
            for s in range(len(o_refs)):
                copy(j, s).start()

    @pl.when(i + ahead < n_steps)
    def _():
        for s in range(len(o_refs)):
            copy(i + ahead, s).start()

    for s, o_ref in enumerate(o_refs):
        copy(i, s).wait()
        o_ref[...] = buf[i % CAST_RING, s].astype(o_ref.dtype)


def to_bf16(w, splits=1):
    shape = w.shape
    cols = shape[-1]
    assert cols % (splits * LANES) == 0
    sc = cols // splits
    w2 = w.reshape(-1, cols)
    rows = w2.shape[0]
    rb = rows
    for cand in (2048, 1024, 512, 256, 128, 64):
        if rows % cand == 0 and cand * cols * 4 <= CAST_BLOCK_BYTES:
            rb = cand
            break
    n_steps = rows // rb
    outs = pl.pallas_call(
        functools.partial(_cast_kernel, n_steps=n_steps, rb=rb, sc=sc),
        grid=(n_steps,),
        in_specs=[pl.BlockSpec(memory_space=pl.ANY)],
        out_specs=[pl.BlockSpec((rb, sc), lambda i: (i, 0)) for _ in range(splits)],
        out_shape=[jax.ShapeDtypeStruct((rows, sc), BF16) for _ in range(splits)],
        scratch_shapes=[pltpu.VMEM((CAST_RING, splits, rb, sc), F32),
                        pltpu.SemaphoreType.DMA((CAST_RING, splits))],
        compiler_params=_cparams(1),
        name="to_bf16",
    )(w2)
    outs = [o.reshape(shape[:-1] + (sc,)) for o in outs]
    return outs[0] if splits == 1 else outs


def _norm_matmul_kernel(*refs, cfg):
    n_p = len(cfg)
    x_ref, ins, outs = refs[0], refs[1:1 + 4 * n_p], refs[1 + 4 * n_p:]
    xf = x_ref[...]
    xs = xf * lax.rsqrt(jnp.mean(xf * xf, axis=-1, keepdims=True) + EPS)
    for pi, (n_chunk, head_cols) in enumerate(cfg):
        g_ref, w_ref, hg_ref, gm_ref = ins[4 * pi:4 * pi + 4]
        o_ref = outs[pi]
        xn = (xs * g_ref[...]).astype(BF16)
        n = w_ref.shape[1]
        for c0 in range(0, n, n_chunk):
            acc = _dot(xn, w_ref[:, c0:c0 + n_chunk])
            if c0 < head_cols:
                sq = (acc * acc).astype(BF16)
                gw = gm_ref.shape[0]
                ms = jnp.concatenate([_dot(sq[:, j:j + gw], gm_ref[...])
                                      for j in range(0, n_chunk, gw)], axis=1)
                acc = acc * lax.rsqrt(ms + EPS) * hg_ref[:, c0:c0 + n_chunk]
            o_ref[:, c0:c0 + n_chunk] = acc.astype(o_ref.dtype)


def norm_projections(x, projs):
    rows, d = x.shape
    tm = _row_tile(rows)
    const = lambda shape: pl.BlockSpec(shape, lambda i: (0,) * len(shape))
    cfg, args, in_specs, out_specs, out_shape = [], [x], [pl.BlockSpec((tm, d), lambda i: (i, 0))], [], []
    for g, w, out_dtype, head_gain, head_dim in projs:
        n = w.shape[1]
        if head_gain is None:
            n_chunk = _col_chunk(n)
            head_cols = 0
            head_gain = jnp.zeros((1, n_chunk), F32)
            head_dim = min(n_chunk, MXU_WIDTH)
        else:
            head_cols = head_gain.shape[1]
            n_chunk = _col_chunk(n, head_cols)
            assert n_chunk % head_dim == 0
        gw = min(n_chunk, MXU_WIDTH)
        assert gw % head_dim == 0 and n_chunk % gw == 0
        grp = jnp.arange(gw) // head_dim
        gm = ((grp[:, None] == grp[None, :]).astype(F32) / head_dim).astype(BF16)
        cfg.append((n_chunk, head_cols))
        args += [g, w, head_gain, gm]
        in_specs += [const((1, d)), const((d, n)), const(head_gain.shape), const((gw, gw))]
        out_specs.append(pl.BlockSpec((tm, n), lambda i: (i, 0)))
        out_shape.append(jax.ShapeDtypeStruct((rows, n), out_dtype))
    return pl.pallas_call(
        functools.partial(_norm_matmul_kernel, cfg=tuple(cfg)),
        grid=(rows // tm,),
        in_specs=in_specs,
        out_specs=out_specs,
        out_shape=out_shape,
        compiler_params=_cparams(1),
        name="norm_matmul",
    )(*args)


def norm_matmul(x, g, w, out_dtype, *, head_gain=None, head_dim=None):
    return norm_projections(x, [(g, w, out_dtype, head_gain, head_dim)])[0]


def _matmul_res_kernel(a_ref, w_ref, r_ref, o_ref, *, n_chunk):
    a = a_ref[...]
    n = w_ref.shape[1]
    for c0 in range(0, n, n_chunk):
        o_ref[:, c0:c0 + n_chunk] = r_ref[:, c0:c0 + n_chunk] + _dot(a, w_ref[:, c0:c0 + n_chunk])


def matmul_residual(a, w, res):
    rows, k = a.shape
    n = w.shape[1]
    tm = _row_tile(rows)
    n_chunk = _col_chunk(n)
    return pl.pallas_call(
        functools.partial(_matmul_res_kernel, n_chunk=n_chunk),
        grid=(rows // tm,),
        in_specs=[
            pl.BlockSpec((tm, k), lambda i: (i, 0)),
            pl.BlockSpec((k, n), lambda i: (0, 0)),
            pl.BlockSpec((tm, n), lambda i: (i, 0)),
        ],
        out_specs=pl.BlockSpec((tm, n), lambda i: (i, 0)),
        out_shape=jax.ShapeDtypeStruct((rows, n), F32),
        compiler_params=_cparams(1),
        name="matmul_residual",
    )(a, w, res)


def _swiglu_acc(xn, wg_ref, wu_ref, wda_ref, wdb_ref, f_chunk, widx=()):
    ff = wg_ref.shape[-1]
    acc_a = acc_b = None
    for c0 in range(0, ff, f_chunk):
        cols = widx + (slice(None), slice(c0, c0 + f_chunk))
        rows = widx + (slice(c0, c0 + f_chunk), slice(None))
        gt = _dot(xn, wg_ref[cols])
        up = _dot(xn, wu_ref[cols])
        act = (gt * (1.0 / (1.0 + jnp.exp(-gt))) * up).astype(BF16)
        pa, pb = _dot(act, wda_ref[rows]), _dot(act, wdb_ref[rows])
        acc_a = pa if acc_a is None else acc_a + pa
        acc_b = pb if acc_b is None else acc_b + pb
    return acc_a, acc_b


def _dense_ffn_kernel(x_ref, g_ref, wg_ref, wu_ref, wda_ref, wdb_ref, o_ref, *, f_chunk):
    xf = x_ref[...]
    xn = _rms(xf, g_ref[...]).astype(BF16)
    acc_a, acc_b = _swiglu_acc(xn, wg_ref, wu_ref, wda_ref, wdb_ref, f_chunk)
    half = acc_a.shape[1]
    o_ref[:, :half] = xf[:, :half] + acc_a
    o_ref[:, half:] = xf[:, half:] + acc_b


def dense_ffn(x, g, wg, wu, wda, wdb):
    rows, d = x.shape
    ff = wg.shape[1]
    tm = _row_tile(rows)
    const = lambda shape: pl.BlockSpec(shape, lambda i: (0, 0))
    return pl.pallas_call(
        functools.partial(_dense_ffn_kernel, f_chunk=FF_CHUNK),
        grid=(rows // tm,),
        in_specs=[
            pl.BlockSpec((tm, d), lambda i: (i, 0)),
            const((1, d)), const((d, ff)), const((d, ff)), const((ff, d // 2)), const((ff, d // 2)),
        ],
        out_specs=pl.BlockSpec((tm, d), lambda i: (i, 0)),
        out_shape=jax.ShapeDtypeStruct((rows, d), F32),
        compiler_params=_cparams(1),
        name="dense_ffn",
    )(x, g, wg, wu, wda, wdb)


def _sb_suffix(z, mask, u_ref):
    zb = z.astype(BF16)
    one, zero = jnp.asarray(1.0, BF16), jnp.asarray(0.0, BF16)
    sp = jnp.maximum(zb, zero) + jnp.log(one + jnp.exp2(jnp.abs(zb) * jnp.asarray(-LOG2E, BF16)))
    if mask is not None:
        sp = jnp.where(mask, sp, zero)
    return _dot(sp, u_ref[...])


def _sb_kernel(*refs, tq, tk, hd, n_meta, scale):
    z_scr, cs_scr, run_scr, acc_scr = refs[-4:]
    if n_meta:
        q_ref, k_ref, v_ref, km_ref, vm_ref, u_ref, o_ref = refs[:-4]
    else:
        q_ref, k_ref, v_ref, u_ref, o_ref = refs[:-4]
    s_len = q_ref.shape[1]
    n_hp = q_ref.shape[2] // LANES
    psl = lambda p: slice(p * LANES, (p + 1) * LANES)
    n_q = s_len // tq
    n_diag = tq // tk
    has_meta = 1 if n_meta else 0
    lane = lax.broadcasted_iota(jnp.int32, (tq, LANES), 1)
    head0 = lane < hd
    row_in = lax.broadcasted_iota(jnp.int32, (tq, tk), 0)
    col_in = lax.broadcasted_iota(jnp.int32, (tq, tk), 1)
    MKEYS = "meta"

    def q_tile(i, n_real, static_tiles):
        r0 = pl.multiple_of(i * tq, tq)
        qh = []
        for p in range(n_hp):
            q = q_ref[0, pl.ds(r0, tq), psl(p)] * scale
            qh += [jnp.where(head0, q, jnp.zeros_like(q)), jnp.where(head0, jnp.zeros_like(q), q)]

        def offset(n):
            return pl.multiple_of(r0 + (n_diag - 1 - n) * tk, tk)

        def mask_of(n):
            if n is MKEYS:
                return col_in < n_meta
            if isinstance(n, int) and n < n_diag:
                return (col_in + (n_diag - 1 - n) * tk) < row_in
            return None

        def load(n, ref, mref):
            return mref[...] if n is MKEYS else ref[0, pl.ds(offset(n), tk), :]

        def step(par, t1, t2, t3):
            if t3 is not None:
                v, m = load(t3, v_ref, vm_ref if n_meta else None), mask_of(t3)
                for h in range(2 * n_hp):
                    g = run_scr[h] + cs_scr[h]
                    a = jnp.exp(z_scr[par, h] + g)
                    if m is not None:
                        a = jnp.where(m, a, 0.0)
                    run_scr[h] = jnp.broadcast_to(g[:, 0:1], g.shape)
                    acc_scr[h] += _dot(a.astype(BF16), v[:, psl(h // 2)])
            if t2 is not None:
                m = mask_of(t2)
                for h in range(2 * n_hp):
                    cs_scr[h] = _sb_suffix(z_scr[1 - par, h], m, u_ref)
            if t1 is not None:
                k = load(t1, k_ref, km_ref if n_meta else None)
                for h in range(2 * n_hp):
                    z_scr[par, h] = _dot_nt(qh[h], k[:, psl(h // 2)])

        run_scr[...] = jnp.zeros_like(run_scr)
        acc_scr[...] = jnp.zeros_like(acc_scr)

        if static_tiles:
            tiles = list(range(n_real)) + ([MKEYS] if has_meta else [])
            for s in range(len(tiles) + 2):
                pick = lambda j: tiles[j] if 0 <= j < len(tiles) else None
                step(s % 2, pick(s), pick(s - 1), pick(s - 2))
        else:
            n_pro = n_diag + 2
            assert n_pro % 2 == 0
            for s in range(n_pro):
                pick = lambda j: j if j >= 0 else None
                step(s % 2, pick(s), pick(s - 1), pick(s - 2))

            def body(j, carry):
                s = n_pro + 2 * j
                step(0, s, s - 1, s - 2)
                step(1, s + 1, s, s - 1)
                return carry

            lax.fori_loop(0, (n_real - n_pro) // 2, body, 0)
            last = n_real - 1
            step(0, MKEYS if has_meta else None, last, last - 1)
            step(1, None, MKEYS if has_meta else None, last)
            if has_meta:
                step(0, None, None, MKEYS)

        for p in range(n_hp):
            o_ref[0, pl.ds(r0, tq), psl(p)] = jnp.where(
                head0, acc_scr[2 * p], acc_scr[2 * p + 1]).astype(o_ref.dtype)

    q_tile(0, n_diag, True)
    if n_q > 1:
        assert n_diag >= 2, "the pipelined prologue needs two diagonal key tiles per query tile"

        def outer(i, carry):
            q_tile(i, n_diag * (i + 1), False)
            return carry

        lax.fori_loop(1, n_q, outer, 0)


def _suffix_matrix(t):
    r = jnp.arange(t)
    return -(r[:, None] >= r[None, :]).astype(BF16)


def stickbreak_attention(qkv, qkv_meta, d_model, hd):
    b, s_len, _ = qkv.shape
    n_pair = d_model // LANES
    tq = min(ATT_TQ, s_len)
    tk = min(ATT_TK, s_len)
    n_meta = 0 if qkv_meta is None else qkv_meta.shape[0]
    n_hp = ATT_PAIRS if n_pair % ATT_PAIRS == 0 else 1
    width = n_hp * LANES
    n_grp = n_pair // n_hp
    col = lambda which: (lambda bi, hp: (bi, 0, which * n_grp + hp))
    in_specs = [pl.BlockSpec((1, s_len, width), col(0)),
                pl.BlockSpec((1, s_len, width), col(1)),
                pl.BlockSpec((1, s_len, width), col(2))]
    args = [qkv, qkv, qkv]
    if n_meta:
        assert n_meta <= tk
        meta_pad = jnp.zeros((tk, qkv_meta.shape[1]), qkv_meta.dtype).at[:n_meta].set(qkv_meta)
        in_specs += [pl.BlockSpec((tk, width), lambda bi, hp: (0, n_grp + hp)),
                     pl.BlockSpec((tk, width), lambda bi, hp: (0, 2 * n_grp + hp))]
        args += [meta_pad, meta_pad]
    in_specs.append(pl.BlockSpec((tk, tk), lambda bi, hp: (0, 0)))
    args.append(_suffix_matrix(tk))
    kern = functools.partial(_sb_kernel, tq=tq, tk=tk, hd=hd, n_meta=n_meta, scale=hd ** -0.5)
    nh = 2 * n_hp
    return pl.pallas_call(
        kern,
        grid=(b, n_grp),
        in_specs=in_specs,
        out_specs=pl.BlockSpec((1, s_len, width), lambda bi, hp: (bi, 0, hp)),
        out_shape=jax.ShapeDtypeStruct((b, s_len, d_model), BF16),
        scratch_shapes=[pltpu.VMEM((2, nh, tq, tk), F32), pltpu.VMEM((nh, tq, tk), F32),
                        pltpu.VMEM((nh, tq, tk), F32), pltpu.VMEM((nh, tq, LANES), F32)],
        compiler_params=_cparams(2),
        name="stickbreak_attention",
    )(*args)


def _log_sigmoid(y):
    return jnp.minimum(y, 0.0) - jnp.log(1.0 + jnp.exp(-jnp.abs(y)))


def _tri_cumsum(tri, x):
    h1, h2, h3 = _split3(x)
    return _dot(tri, h1) + _dot(tri, h2) + _dot(tri, h3)


def _forget_cumsum_kernel(f_ref, fm_ref, b_ref, tri_ref, trim_ref, o_ref, om_ref, *, chunk, n_heads):
    bias = b_ref[...]

    def packed(f):
        lane = lax.broadcasted_iota(jnp.int32, f.shape, 1)
        out = None
        for a, part in enumerate(_split3(f)):
            pa = jnp.where(lane < n_heads, part.astype(F32), 0.0)
            pa = pa if a == 0 else pltpu.roll(pa, a * n_heads, 1)
            out = pa if out is None else out + pa
        return out.astype(BF16)

    fm = _tri_cumsum(trim_ref[...], _log_sigmoid(fm_ref[...] + bias))
    om_ref[...] = packed(fm)
    n_meta = fm.shape[0]
    carry = fm[n_meta - 1:n_meta, :]
    s_len = f_ref.shape[1]
    for c0 in range(0, s_len, chunk):
        fc = _tri_cumsum(tri_ref[...], _log_sigmoid(f_ref[0, c0:c0 + chunk, :] + bias)) + carry
        o_ref[0, c0:c0 + chunk, :] = packed(fc)
        carry = fc[chunk - 1:chunk, :]


def forget_cumsum(f_logit, f_logit_meta, bias, n_heads):
    b, s_len, w = f_logit.shape
    n_meta = f_logit_meta.shape[0]
    assert 3 * n_heads <= w
    chunk = min(CUM_CHUNK, s_len)
    tri = lambda t: (jnp.arange(t)[:, None] >= jnp.arange(t)[None, :]).astype(BF16)
    return pl.pallas_call(
        functools.partial(_forget_cumsum_kernel, chunk=chunk, n_heads=n_heads),
        grid=(b,),
        in_specs=[
            pl.BlockSpec((1, s_len, w), lambda bi: (bi, 0, 0)),
            pl.BlockSpec((n_meta, w), lambda bi: (0, 0)),
            pl.BlockSpec((1, w), lambda bi: (0, 0)),
            pl.BlockSpec((chunk, chunk), lambda bi: (0, 0)),
            pl.BlockSpec((n_meta, n_meta), lambda bi: (0, 0)),
        ],
        out_specs=[pl.BlockSpec((1, s_len, w), lambda bi: (bi, 0, 0)),
                   pl.BlockSpec((n_meta, w), lambda bi: (0, 0))],
        out_shape=[jax.ShapeDtypeStruct((b, s_len, w), BF16),
                   jax.ShapeDtypeStruct((n_meta, w), BF16)],
        compiler_params=_cparams(1),
        name="forget_cumsum",
    )(f_logit, f_logit_meta, bias, tri(chunk), tri(n_meta))


def _fox_kernel(qc_ref, kc_ref, vc_ref, f_ref, kmc_ref, vmc_ref, fm_ref, pq_ref, pk_ref, pv_ref,
                cq_ref, ck_ref, cv_ref, o_ref,
                q_ref, k_ref, v_ref, km_ref, vm_ref, z_scr, p_scr, m_scr, alpha_scr, acc_scr,
                *, tq, tk, hd, n_meta):
    s_len = qc_ref.shape[1]
    n_q = s_len // tq
    n_diag = tq // tk
    lane = lax.broadcasted_iota(jnp.int32, (tq, LANES), 1)
    head0 = lane < hd
    row_in = lax.broadcasted_iota(jnp.int32, (tq, tk), 0)
    col_in = lax.broadcasted_iota(jnp.int32, (tq, tk), 1)
    MKEYS = "meta"
    n_hp = qc_ref.shape[2] // LANES
    n_h = 2 * n_hp
    hsl = lambda h: slice(h * LANES, (h + 1) * LANES)
    psl = lambda p: slice(p * LANES, (p + 1) * LANES)
    xsl = lambda p: slice(2 * p * LANES, 2 * (p + 1) * LANES)

    def place(main, fpacked, p_mat, c_ref):
        lhs = main if fpacked is None else jnp.concatenate([main, fpacked], axis=1)
        return (_dot(lhs, p_mat) + c_ref[...]).astype(BF16)

    fpm = fm_ref[...]
    for p in range(n_hp):
        for c0 in range(0, s_len, tq):
            rows = slice(c0, c0 + tq)
            fp = f_ref[0, rows, :]
            q_ref[rows, xsl(p)] = place(qc_ref[0, rows, psl(p)], fp, pq_ref[p], cq_ref)
            k_ref[rows, xsl(p)] = place(kc_ref[0, rows, psl(p)], fp, pk_ref[p], ck_ref)
            v_ref[rows, xsl(p)] = place(vc_ref[0, rows, psl(p)], None, pv_ref[0], cv_ref)
        km_ref[:, xsl(p)] = place(kmc_ref[:, psl(p)], fpm, pk_ref[p], ck_ref)
        vm_ref[:, xsl(p)] = place(vmc_ref[:, psl(p)], None, pv_ref[0], cv_ref)

    def q_tile(i, n_real, static_tiles):
        r0 = pl.multiple_of(i * tq, tq)
        qh = [q_ref[pl.ds(r0, tq), hsl(h)] for h in range(n_h)]

        def mask_of(t):
            if t is MKEYS:
                return col_in < n_meta
            return None if t[1] is None else (col_in + t[1] * tk) <= row_in

        def load(t, ref, mref, h):
            if t is MKEYS:
                return mref[:, hsl(h)]
            return ref[pl.ds(pl.multiple_of(t[0] * tk, tk), tk), hsl(h)]

        def step(par, t1, t2, t3):
            if t3 is not None:
                for h in range(n_h):
                    acc_scr[h] = acc_scr[h] * alpha_scr[h] + _dot(p_scr[h], load(t3, v_ref, vm_ref, h))
            if t2 is not None:
                m = mask_of(t2)
                for h in range(n_h):
                    z = z_scr[1 - par, h]
                    if m is not None:
                        z = jnp.where(m, z, -jnp.inf)
                    m_old = m_scr[h]
                    m_new = jnp.maximum(m_old, jnp.max(z, axis=-1, keepdims=True))
                    p_scr[h] = jnp.exp(z - m_new).astype(BF16)
                    alpha_scr[h] = jnp.exp(m_old - m_new)[:, :LANES]
                    m_scr[h] = m_new
            if t1 is not None:
                for h in range(n_h):
                    z_scr[par, h] = _dot_nt(qh[h], load(t1, k_ref, km_ref, h))

        m_scr[...] = jnp.full(m_scr.shape, -jnp.inf, F32)
        acc_scr[...] = jnp.zeros_like(acc_scr)

        if static_tiles:
            tiles = [MKEYS] + [(n, n - (n_real - n_diag) if n >= n_real - n_diag else None)
                              for n in range(n_real)]
            for s in range(len(tiles) + 2):
                pick = lambda j: tiles[j] if 0 <= j < len(tiles) else None
                step(s % 2, pick(s), pick(s - 1), pick(s - 2))
        else:
            pos = lambda p: MKEYS if (isinstance(p, int) and p == 0) else (p - 1, None)
            n_pro = 4
            for s in range(n_pro):
                pick = lambda j: pos(j) if j >= 0 else None
                step(s % 2, pick(s), pick(s - 1), pick(s - 2))

            def body(j, carry):
                s = n_pro + 2 * j
                step(0, pos(s), pos(s - 1), pos(s - 2))
                step(1, pos(s + 1), pos(s), pos(s - 1))
                return carry

            lax.fori_loop(0, (n_real - n_pro) // 2, body, 0)
            a, b = n_real - 2, n_real - 1
            step(0, (b, None), (a, 0), (a - 1, None))
            step(1, None, (b, 1), (a, None))
            step(0, None, None, (b, None))

        for p in range(n_hp):
            res0, res1 = acc_scr[2 * p], acc_scr[2 * p + 1]
            rot0 = pltpu.roll(res0, hd, 1)
            rot1 = pltpu.roll(res1, hd, 1)
            o_ref[0, pl.ds(r0, tq), psl(p)] = jnp.where(
                head0, res0 / rot0, rot1 / res1).astype(o_ref.dtype)

    q_tile(0, n_diag, True)
    if n_q > 1:
        assert n_diag == 2, "the pipelined sweep is written for two diagonal key tiles per query tile"

        def outer(i, carry):
            q_tile(i, n_diag * (i + 1), False)
            return carry

        lax.fori_loop(1, n_q, outer, 0)


def _fox_placement(n_pair, hd):
    n_heads = 2 * n_pair
    pq = np.zeros((n_pair, 2 * LANES, 2 * LANES), np.float32)
    pk = np.zeros((n_pair, 2 * LANES, 2 * LANES), np.float32)
    pv = np.zeros((1, LANES, 2 * LANES), np.float32)
    cq = np.zeros((1, 2 * LANES), np.float32)
    ck = np.zeros((1, 2 * LANES), np.float32)
    cv = np.zeros((1, 2 * LANES), np.float32)
    for h in range(2):
        base = h * LANES
        for j in range(hd):
            pq[:, h * hd + j, base + j] = hd ** -0.5
            pk[:, h * hd + j, base + j] = 1.0
            pv[:, h * hd + j, base + j] = 1.0
        for p in range(3):
            for hp in range(n_pair):
                pq[hp, LANES + p * n_heads + 2 * hp + h, base + hd + p] = 1.0
                pk[hp, LANES + p * n_heads + 2 * hp + h, base + hd + 3 + p] = -1.0
        cq[0, base + hd + 3:base + hd + 6] = 1.0
        ck[0, base + hd:base + hd + 3] = 1.0
        cv[0, base + hd:base + LANES] = 1.0
    bf = lambda a: jnp.asarray(a, BF16)
    return bf(pq), bf(pk), bf(pv), jnp.asarray(cq), jnp.asarray(ck), jnp.asarray(cv)


def forgetting_attention(q, kv, kv_meta, fparts, fparts_meta, d_model, hd):
    b, s_len, _ = q.shape
    n_pair = d_model // LANES
    n_meta = kv_meta.shape[0]
    tq = min(ATT_TQ, s_len)
    tk = min(ATT_TK, s_len)
    assert n_meta <= tk and tk == LANES and 3 * 2 * n_pair <= LANES
    kvm = jnp.zeros((tk, kv_meta.shape[1]), kv_meta.dtype).at[:n_meta].set(kv_meta)
    fpm = jnp.zeros((tk, LANES), fparts_meta.dtype).at[:n_meta].set(fparts_meta)
    pq, pk, pv, cq, ck, cv = _fox_placement(n_pair, hd)
    n_hp = FOX_PAIRS if n_pair % FOX_PAIRS == 0 else 1
    width = n_hp * LANES
    n_grp = n_pair // n_hp
    nh = 2 * n_hp
    blk = lambda off: pl.BlockSpec((1, s_len, width), lambda bi, hp: (bi, 0, off + hp))
    mblk = lambda off: pl.BlockSpec((tk, width), lambda bi, hp: (0, off + hp))
    crow = pl.BlockSpec((1, 2 * LANES), lambda bi, hp: (0, 0))
    in_specs = [blk(0), blk(0), blk(n_grp),
                pl.BlockSpec((1, s_len, LANES), lambda bi, hp: (bi, 0, 0)),
                mblk(0), mblk(n_grp),
                pl.BlockSpec((tk, LANES), lambda bi, hp: (0, 0)),
                pl.BlockSpec((n_hp, 2 * LANES, 2 * LANES), lambda bi, hp: (hp, 0, 0)),
                pl.BlockSpec((n_hp, 2 * LANES, 2 * LANES), lambda bi, hp: (hp, 0, 0)),
                pl.BlockSpec((1, LANES, 2 * LANES), lambda bi, hp: (0, 0, 0)),
                crow, crow, crow]
    ext = lambda rows: pltpu.VMEM((rows, nh * LANES), BF16)
    return pl.pallas_call(
        functools.partial(_fox_kernel, tq=tq, tk=tk, hd=hd, n_meta=n_meta),
        grid=(b, n_grp),
        in_specs=in_specs,
        out_specs=pl.BlockSpec((1, s_len, width), lambda bi, hp: (bi, 0, hp)),
        out_shape=jax.ShapeDtypeStruct((b, s_len, d_model), BF16),
        scratch_shapes=[ext(s_len), ext(s_len), ext(s_len), ext(tk), ext(tk),
                        pltpu.VMEM((2, nh, tq, tk), F32), pltpu.VMEM((nh, tq, tk), BF16),
                        pltpu.VMEM((nh, tq, tk), F32), pltpu.VMEM((nh, tq, LANES), F32),
                        pltpu.VMEM((nh, tq, LANES), F32)],
        compiler_params=_cparams(2),
        name="forgetting_attention",
    )(q, kv, kv, fparts, kvm, kvm, fpm, pq, pk, pv, cq, ck, cv)


def _router_kernel(a_ref, wo_ref, r_ref, g_ref, wr_ref, h_ref, xn_ref, id_ref, g1_ref, g2_ref,
                   *, n_experts, n_chunk):
    a = a_ref[...]
    for c0 in range(0, wo_ref.shape[1], n_chunk):
        h_ref[:, c0:c0 + n_chunk] = r_ref[:, c0:c0 + n_chunk] + _dot(a, wo_ref[:, c0:c0 + n_chunk])
    xn = _rms(h_ref[...], g_ref[...])
    xn_ref[...] = xn.astype(BF16)
    a1, a2 = _split2(xn)
    b1, b2 = _split2(wr_ref[...])
    logits = _dot(a1, b1) + _dot(a1, b2) + _dot(a2, b1)
    lane = lax.broadcasted_iota(jnp.int32, logits.shape, 1)
    lg = jnp.where(lane < n_experts, logits, -jnp.inf)
    m1 = jnp.max(lg, axis=-1, keepdims=True)
    i1 = jnp.min(jnp.where(lg == m1, lane, LANES), axis=-1, keepdims=True)
    lg2 = jnp.where(lane == i1, -jnp.inf, lg)
    m2 = jnp.max(lg2, axis=-1, keepdims=True)
    i2 = jnp.min(jnp.where(lg2 == m2, lane, LANES), axis=-1, keepdims=True)
    e2 = jnp.exp(m2 - m1)
    den = 1.0 + e2
    ids = jnp.where(lane == 0, i1, jnp.where(lane == 1, i2, 0)).astype(F32)
    id_ref[...] = jnp.transpose(ids)[:id_ref.shape[0]].astype(jnp.int32)
    g1_ref[...] = jnp.broadcast_to(1.0 / den, logits.shape)
    g2_ref[...] = jnp.broadcast_to(e2 / den, logits.shape)


def moe_router(a, w_o, res, g, w_router):
    rows, d = res.shape
    k = a.shape[1]
    n_experts = w_router.shape[1]
    wr = jnp.zeros((d, LANES), F32).at[:, :n_experts].set(w_router)
    tm = _row_tile(rows)
    row_blk = lambda w: pl.BlockSpec((tm, w), lambda i: (i, 0))
    return pl.pallas_call(
        functools.partial(_router_kernel, n_experts=n_experts, n_chunk=_col_chunk(d)),
        grid=(rows // tm,),
        in_specs=[
            row_blk(k),
            pl.BlockSpec((k, d), lambda i: (0, 0)),
            row_blk(d),
            pl.BlockSpec((1, d), lambda i: (0, 0)),
            pl.BlockSpec((d, LANES), lambda i: (0, 0)),
        ],
        out_specs=[row_blk(d), row_blk(d), pl.BlockSpec((8, tm), lambda i: (0, i)),
                   row_blk(LANES), row_blk(LANES)],
        out_shape=[jax.ShapeDtypeStruct((rows, d), F32),
                   jax.ShapeDtypeStruct((rows, d), BF16),
                   jax.ShapeDtypeStruct((8, rows), jnp.int32),
                   jax.ShapeDtypeStruct((rows, LANES), F32),
                   jax.ShapeDtypeStruct((rows, LANES), F32)],
        compiler_params=_cparams(1),
        name="moe_router",
    )(a, w_o, res, g, wr)


def _moe_ffn_kernel(te_ref, nt_ref, x_ref, wg_ref, wu_ref, wda_ref, wdb_ref, o_ref, *, f_chunk):
    i = pl.program_id(0)

    @pl.when(i < nt_ref[0])
    def _():
        acc_a, acc_b = _swiglu_acc(x_ref[...], wg_ref, wu_ref, wda_ref, wdb_ref, f_chunk, widx=(0,))
        half = acc_a.shape[1]
        o_ref[:, :half] = acc_a.astype(o_ref.dtype)
        o_ref[:, half:] = acc_b.astype(o_ref.dtype)

    @pl.when(i >= nt_ref[0])
    def _():
        o_ref[...] = jnp.zeros_like(o_ref)


def moe_grouped_ffn(xs, tile_expert, n_tiles_used, wg, wu, wda, wdb):
    p_rows, d = xs.shape
    n_experts, _, ff = wg.shape
    tm = MOE_TM
    expert = lambda shape: pl.BlockSpec(shape, lambda i, te, nt: (te[i], 0, 0))
    grid_spec = pltpu.PrefetchScalarGridSpec(
        num_scalar_prefetch=2,
        grid=(p_rows // tm,),
        in_specs=[
            pl.BlockSpec((tm, d), lambda i, te, nt: (i, 0)),
            expert((1, d, ff)), expert((1, d, ff)), expert((1, ff, d // 2)), expert((1, ff, d // 2)),
        ],
        out_specs=pl.BlockSpec((tm, d), lambda i, te, nt: (i, 0)),
    )
    return pl.pallas_call(
        functools.partial(_moe_ffn_kernel, f_chunk=FF_CHUNK),
        grid_spec=grid_spec,
        out_shape=jax.ShapeDtypeStruct((p_rows, d), BF16),
        compiler_params=_cparams(1),
        name="moe_grouped_ffn",
    )(tile_expert, n_tiles_used, xs, wg, wu, wda, wdb)


def _combine_kernel(h_ref, y1_ref, y2_ref, g1_ref, g2_ref, o_ref):
    d = h_ref.shape[1]
    g1, g2 = g1_ref[...], g2_ref[...]
    for c0 in range(0, d, LANES):
        cs = slice(c0, c0 + LANES)
        o_ref[:, cs] = (h_ref[:, cs] + g1 * y1_ref[:, cs].astype(F32)
                        + g2 * y2_ref[:, cs].astype(F32))


def moe_block(a, w_o, res, g, w_router, ffn_weights):
    rows, d = res.shape
    n_experts = w_router.shape[1]
    tm = MOE_TM
    h, xn, ids, g1, g2 = moe_router(a, w_o, res, g, w_router)

    eid = ids[:TOP_K]
    ex = jnp.arange(n_experts, dtype=jnp.int32)[:, None]
    sel = (eid[0][None, :] == ex) | (eid[1][None, :] == ex)
    rank = jnp.cumsum(sel.astype(jnp.int32), axis=1) - 1
    cnt = rank[:, -1] + 1
    cnt_pad = ((cnt + tm - 1) // tm) * tm
    ends = jnp.cumsum(cnt_pad)
    offs = ends - cnt_pad
    slot_of = offs[:, None] + rank
    pos = [jnp.sum(jnp.where(eid[k][None, :] == ex, slot_of, 0), axis=0) for k in range(TOP_K)]
    p_rows = rows * TOP_K + n_experts * tm
    n_tiles = p_rows // tm
    tile_start = jnp.arange(n_tiles, dtype=jnp.int32) * tm
    tile_expert = jnp.minimum((tile_start[:, None] >= ends[None, :]).sum(axis=1), n_experts - 1)
    n_used = (ends[-1] // tm).astype(jnp.int32).reshape(1)

    tok = jnp.arange(rows, dtype=jnp.int32)[None, :]
    tok_c = jnp.sort((eid * rows + tok).reshape(-1)) % rows
    tok_c = jnp.concatenate([tok_c, jnp.zeros((p_rows - rows * TOP_K,), jnp.int32)])
    cstart = jnp.cumsum(cnt) - cnt
    slot = jnp.arange(p_rows, dtype=jnp.int32)
    tok_sorted = jnp.zeros((p_rows,), jnp.int32)
    for e in range(n_experts):
        inside = (slot >= offs[e]) & (slot < offs[e] + cnt[e])
        tok_sorted = jnp.where(inside, jnp.roll(tok_c, offs[e] - cstart[e]), tok_sorted)
    xs = xn.at[tok_sorted].get(mode="promise_in_bounds")
    ys = moe_grouped_ffn(xs, tile_expert.astype(jnp.int32), n_used, *ffn_weights)
    y1 = ys.at[pos[0]].get(mode="promise_in_bounds", unique_indices=True)
    y2 = ys.at[pos[1]].get(mode="promise_in_bounds", unique_indices=True)
    tr = _row_tile(rows)
    spec = pl.BlockSpec((tr, d), lambda i: (i, 0))
    gspec = pl.BlockSpec((tr, LANES), lambda i: (i, 0))
    return pl.pallas_call(
        _combine_kernel,
        grid=(rows // tr,),
        in_specs=[spec, spec, spec, gspec, gspec],
        out_specs=spec,
        out_shape=jax.ShapeDtypeStruct((rows, d), F32),
        compiler_params=_cparams(1),
        name="moe_combine",
    )(h, y1, y2, g1, g2)


def kernel(x, meta_tokens, norm_attn_a, w_qkv_a, w_o_a, norm_kv, w_kvf, b_f, k_norm, norm_attn_b,
           w_q_b, q_norm_b, w_o_b, norm_ffn_dense, w_gu_dense, w_down_dense, norm_ffn_moe,
           w_router, w_gu_moe, w_down_moe):
    b, s_len, d = x.shape
    n_heads = b_f.shape[0]
    hd = d // n_heads
    n_meta = meta_tokens.shape[0]
    depth = norm_attn_a.shape[0] + norm_attn_b.shape[0]
    n_a = norm_attn_a.shape[0]
    assert 2 * hd == LANES and d % LANES == 0
    row = lambda v: v.reshape(1, -1).astype(F32)

    h = x.reshape(b * s_len, d)
    hm = meta_tokens.astype(x.dtype)
    for layer in range(depth):
        last = layer == depth - 1
        if layer < n_a:
            w_qkv = to_bf16(w_qkv_a[layer])
            g = row(norm_attn_a[layer])
            qkv = norm_matmul(h, g, w_qkv, BF16).reshape(b, s_len, 3 * d)
            qkv_m = norm_matmul(hm, g, w_qkv, BF16)
            o = stickbreak_attention(qkv, qkv_m, d, hd).reshape(b * s_len, d)
            o_m = stickbreak_attention(qkv_m[None], None, d, hd)[0]
            w_o = to_bf16(w_o_a[layer])
            hm = matmul_residual(o_m, w_o, hm)
        else:
            if layer == n_a:
                g = row(norm_kv)
                w_kv = to_bf16(w_kvf[:, :2 * d])
                w_f = jnp.zeros((d, LANES), F32).at[:, :n_heads].set(w_kvf[:, 2 * d:]).astype(BF16)
                kgain = row(jnp.tile(k_norm, n_heads))
                shared = [(g, w_kv, BF16, kgain, hd), (g, w_f, F32, None, None)]
            i = layer - n_a
            q_proj = (row(norm_attn_b[i]), to_bf16(w_q_b[i]), BF16,
                      row(jnp.tile(q_norm_b[i], n_heads)), hd)
            if layer == n_a:
                kv, fl, q = norm_projections(h, shared + [q_proj])
                kv_m, fl_m = norm_projections(hm, shared)
                bias = jnp.zeros((1, LANES), F32).at[0, :n_heads].set(b_f.astype(F32))
                f_cum, f_cum_m = forget_cumsum(fl.reshape(b, s_len, LANES), fl_m, bias, n_heads)
                kv_sh = kv.reshape(b, s_len, 2 * d)
            else:
                q, = norm_projections(h, [q_proj])
            o = forgetting_attention(q.reshape(b, s_len, d), kv_sh, kv_m, f_cum, f_cum_m, d, hd)
            o = o.reshape(b * s_len, d)
            w_o = to_bf16(w_o_b[i])
            if not last:
                raise NotImplementedError("meta-row queries in forgetting layers before the last")
        j = layer // 2
        if layer % 2 == 0:
            ffn_w = to_bf16(w_gu_dense[j], splits=2) + to_bf16(w_down_dense[j], splits=2)
            g = row(norm_ffn_dense[j])
            h = dense_ffn(matmul_residual(o, w_o, h), g, *ffn_w)
            if not last:
                hm = dense_ffn(hm, g, *ffn_w)
        else:
            ffn_w = to_bf16(w_gu_moe[j], splits=2) + to_bf16(w_down_moe[j], splits=2)
            h = moe_block(o, w_o, h, row(norm_ffn_moe[j]), w_router[j], ffn_w)
            if not last:
                raise NotImplementedError("meta rows through a MoE layer before the last")
    return h.reshape(b, s_len, d)
```
